```python
import math
import jax, jax.numpy as jnp
from jax import lax
import numpy as np

D_MODEL = 2048
BATCH = 16
SEQ = 2048
DEPTH = 4

HEAD_DIM = 128
BRANCH_W = D_MODEL // 2
N_BRANCH = 3
NSA_HEADS = BRANCH_W // HEAD_DIM
NSA_KV_HEADS = NSA_HEADS // 4
NSA_GROUP = NSA_HEADS // NSA_KV_HEADS
CMP_BLOCK = 32
CMP_STRIDE = 16
SEL_BLOCK = 64
SEL_TOPK = 16
WINDOW = 512
Q_BLOCK = 128
SEL_QCHUNK = 16
SB_HEADS = BRANCH_W // HEAD_DIM
HG_DK = 128
HG_DV = 128
HG_HEADS = BRANCH_W // HG_DV
HG_CHUNK = 64
D_FF = 256 * ((8 * D_MODEL // 3 + 255) // 256)
CONV_W = 3
REL_BUCKETS = 32
REL_MAX_DIST = 128
EPS = 1e-6
NEG = -1e30
TINY = 1e-30

NSA_Q = NSA_HEADS * HEAD_DIM
NSA_KV = NSA_KV_HEADS * HEAD_DIM
NSA_GATE = 3 * NSA_HEADS
SB_W = SB_HEADS * HEAD_DIM
HG_KW = HG_HEADS * HG_DK
HG_VW = HG_HEADS * HG_DV
SPLIT_SIZES = (NSA_Q,) + (NSA_KV,) * 6 + (NSA_GATE,) + (SB_W,) * 3 + (HG_KW, HG_KW, HG_VW, HG_VW, N_BRANCH * D_MODEL)
IN_COLS = sum(SPLIT_SIZES)

kernel_name = 'hybrid_nsa_stickbreaking_hgrn2_block'


def rmsnorm(x, g):
    xf = x.astype(jnp.float32)
    y = xf * lax.rsqrt(jnp.mean(xf * xf, axis=-1, keepdims=True) + EPS)
    return (y * g.astype(jnp.float32)).astype(x.dtype)


def masked_softmax(s, mask):
    s = jnp.where(mask, s, NEG)
    e = jnp.where(mask, jnp.exp(s - jnp.max(s, axis=-1, keepdims=True)), 0.0)
    return e / jnp.maximum(jnp.sum(e, axis=-1, keepdims=True), TINY)


def t5_bucket(dist):
    n = jnp.maximum(dist, 0)
    exact = REL_BUCKETS // 2
    big = exact + (jnp.log(jnp.maximum(n, 1).astype(jnp.float32) / exact)
                   / math.log(REL_MAX_DIST / exact) * (REL_BUCKETS - exact)).astype(jnp.int32)
    return jnp.where(n < exact, n, jnp.minimum(big, REL_BUCKETS - 1))


def nsa_mixer(q, kc, vc, ks, vs, kw, vw, g, q_gain, k_gain, cmp_pos, cmp_w1, cmp_w2, rel_bias):
    B, T, _ = q.shape
    G, R, Dh = NSA_KV_HEADS, NSA_GROUP, HEAD_DIM
    scale = HEAD_DIM ** -0.5
    q = rmsnorm(q.reshape(B, T, G, R, Dh), q_gain).transpose(0, 2, 3, 1, 4)
    kc, vc, ks, vs, kw, vw = (a.reshape(B, T, G, Dh).transpose(0, 2, 1, 3) for a in (kc, vc, ks, vs, kw, vw))
    ks = rmsnorm(ks, k_gain[1])
    kw = rmsnorm(kw, k_gain[2])
    tab_h = rel_bias.T
    tab_gr = tab_h.reshape(G, R, REL_BUCKETS)

    n_cmp = (T - CMP_BLOCK) // CMP_STRIDE + 1
    cidx = np.arange(n_cmp)[:, None] * CMP_STRIDE + np.arange(CMP_BLOCK)[None, :]

    def compress(a, j):
        blocks = a[:, :, cidx] + cmp_pos[j]
        hid = jax.nn.gelu(jnp.einsum('bgnld,lde->bgne', blocks, cmp_w1[j]))
        return hid @ cmp_w2[j]

    kcmp = rmsnorm(compress(kc, 0), k_gain[0])
    vcmp = compress(vc, 1)
    dist_c = np.arange(T)[:, None] - cidx[None, :, -1]
    bias_c = jnp.take(tab_h, t5_bucket(jnp.asarray(dist_c)), axis=1).reshape(G, R, T, n_cmp)
    s_c = jnp.einsum('bgrtd,bgnd->bgrtn', q, kcmp).astype(jnp.float32) * scale + bias_c
    p_cmp = masked_softmax(s_c, jnp.asarray(dist_c >= 0))
    o_cmp = jnp.einsum('bgrtn,bgnd->bgrtd', p_cmp, vcmp)

    n_sel = T // SEL_BLOCK
    c_start = cidx[:, 0]
    s_start = np.arange(n_sel) * SEL_BLOCK
    overlap = ((c_start[:, None] < s_start[None, :] + SEL_BLOCK)
               & (c_start[:, None] + CMP_BLOCK > s_start[None, :])).astype(np.float32)
    imp = jnp.einsum('bgrtn,nj->bgtj', p_cmp, jnp.asarray(overlap))
    qblk = np.arange(T) // SEL_BLOCK
    jb = np.arange(n_sel)
    causal_b = jb[None, :] <= qblk[:, None]
    forced = causal_b & ((jb[None, :] == 0) | (jb[None, :] >= qblk[:, None] - 1))
    score = jnp.where(jnp.asarray(forced), jnp.inf, jnp.where(jnp.asarray(causal_b), imp, -jnp.inf))
    n_top = min(SEL_TOPK, n_sel)
    top_val, top_idx = lax.top_k(score, n_top)
    sel_ok = top_val > -jnp.inf

    kb = ks.reshape(B, G, n_sel, SEL_BLOCK, Dh)
    vb = vs.reshape(B, G, n_sel, SEL_BLOCK, Dh)
    QC = SEL_QCHUNK
    nqc = T // QC
    q_c = q.reshape(B, G, R, nqc, QC, Dh).transpose(3, 0, 1, 2, 4, 5)
    idx_c = top_idx.reshape(B, G, nqc, QC, n_top).transpose(2, 0, 1, 3, 4)
    ok_c = sel_ok.reshape(B, G, nqc, QC, n_top).transpose(2, 0, 1, 3, 4)
    pos_c = jnp.arange(T, dtype=jnp.int32).reshape(nqc, QC)
    b_ix = jnp.arange(B)[:, None, None, None]
    g_ix = jnp.arange(G)[None, :, None, None]
    g6 = jnp.arange(G).reshape(1, G, 1, 1, 1, 1)
    r6 = jnp.arange(R).reshape(1, 1, R, 1, 1, 1)

    def sel_block(args):
        qq, ii, ok, pp = args
        kg = kb[b_ix, g_ix, ii]
        vg = vb[b_ix, g_ix, ii]
        kpos = ii[..., None] * SEL_BLOCK + jnp.arange(SEL_BLOCK, dtype=jnp.int32)
        dist = pp[None, None, :, None, None] - kpos
        mask = (ok[..., None] & (dist >= 0))[:, :, None]
        bias = tab_gr[g6, r6, t5_bucket(dist)[:, :, None]]
        s = jnp.einsum('bgrqd,bgqksd->bgrqks', qq, kg).astype(jnp.float32) * scale + bias
        flat = s.shape[:4] + (-1,)
        p = masked_softmax(s.reshape(flat), mask.reshape(mask.shape[:4] + (-1,)))
        return jnp.einsum('bgrqm,bgqmd->bgrqd', p, vg.reshape(B, G, QC, -1, Dh))

    o_sel = lax.map(sel_block, (q_c, idx_c, ok_c, pos_c))
    o_sel = o_sel.transpose(1, 2, 3, 0, 4, 5).reshape(B, G, R, T, Dh)

    nqb = T // Q_BLOCK
    widx = np.arange(nqb)[:, None] * Q_BLOCK + np.arange(Q_BLOCK + WINDOW)[None, :]
    kwin = jnp.pad(kw, ((0, 0), (0, 0), (WINDOW, 0), (0, 0)))[:, :, widx]
    vwin = jnp.pad(vw, ((0, 0), (0, 0), (WINDOW, 0), (0, 0)))[:, :, widx]
    qpos = np.arange(nqb)[:, None] * Q_BLOCK + np.arange(Q_BLOCK)[None, :]
    kpos = widx - WINDOW
    dist_w = qpos[:, :, None] - kpos[:, None, :]
    mask_w = (dist_w >= 0) & (dist_w < WINDOW) & (kpos[:, None, :] >= 0)
    bias_w = jnp.take(tab_h, t5_bucket(jnp.asarray(dist_w)), axis=1).reshape(G, R, nqb, Q_BLOCK, Q_BLOCK + WINDOW)
    s_w = jnp.einsum('bgrnqd,bgnkd->bgrnqk', q.reshape(B, G, R, nqb, Q_BLOCK, Dh), kwin).astype(jnp.float32) * scale
    p_w = masked_softmax(s_w + bias_w, jnp.asarray(mask_w))
    o_win = jnp.einsum('bgrnqk,bgnkd->bgrnqd', p_w, vwin).reshape(B, G, R, T, Dh)

    gate = jax.nn.sigmoid(g.astype(jnp.float32)).reshape(B, T, G, R, 3).transpose(0, 2, 3, 1, 4)
    o = gate[..., 0:1] * o_cmp + gate[..., 1:2] * o_sel + gate[..., 2:3] * o_win
    return o.transpose(0, 3, 1, 2, 4).reshape(B, T, NSA_HEADS * Dh)


def stick_breaking(q, k, v):
    B, T, _ = q.shape
    q, k, v = (a.reshape(B, T, SB_HEADS, HEAD_DIM).transpose(0, 2, 1, 3) for a in (q, k, v))
    scale = HEAD_DIM ** -0.5
    outs = []
    for i in range(T // Q_BLOCK):
        end = (i + 1) * Q_BLOCK
        z = jnp.einsum('bhqd,bhkd->bhqk', q[:, :, i * Q_BLOCK:end], k[:, :, :end]).astype(jnp.float32) * scale
        causal = jnp.asarray(np.arange(end)[None, :] < np.arange(i * Q_BLOCK, end)[:, None])
        log_1mb = jnp.where(causal, jax.nn.log_sigmoid(-z), 0.0)
        log_a = jax.nn.log_sigmoid(z) + lax.cumsum(log_1mb, axis=3, reverse=True) - log_1mb
        a = jnp.where(causal, jnp.exp(log_a), 0.0)
        outs.append(jnp.einsum('bhqk,bhkd->bhqd', a, v[:, :, :end]))
    o = jnp.concatenate(outs, axis=2)
    return o.transpose(0, 2, 1, 3).reshape(B, T, SB_HEADS * HEAD_DIM)


def hgrn2(q, f_pre, inp, g_out, lb, norm_gain):
    B, T, _ = q.shape
    H, C = HG_HEADS, HG_CHUNK
    nc = T // C
    f_pre = f_pre.astype(jnp.float32).reshape(B, T, H, HG_DK)
    lb = lb.reshape(H, HG_DK)
    log_f = jnp.logaddexp(jnp.log(lb), jnp.log1p(-lb) + jax.nn.log_sigmoid(f_pre))
    k = (1.0 - lb) * jax.nn.sigmoid(-f_pre)

    def chunks(a, d):
        return a.astype(jnp.float32).reshape(B, nc, C, H, d).transpose(1, 0, 3, 2, 4)

    xs = (chunks(q, HG_DK), chunks(k, HG_DK), chunks(inp, HG_DV), chunks(log_f, HG_DK))
    tri = jnp.asarray(np.tril(np.ones((C, C), dtype=bool)))

    def step(S, xs_c):
        qq, kk, vv, lf = xs_c
        b = jnp.cumsum(lf, axis=2)
        decay = jnp.exp(jnp.where(tri[:, :, None], b[:, :, :, None, :] - b[:, :, None, :, :], -jnp.inf))
        a = jnp.einsum('bhtd,bhsd,bhtsd->bhts', qq, kk, decay)
        o = jnp.einsum('bhts,bhse->bhte', a, vv) + jnp.einsum('bhtd,bhde->bhte', qq * jnp.exp(b), S)
        b_last = b[:, :, -1:, :]
        S = jnp.exp(b_last[:, :, 0, :, None]) * S + jnp.einsum('bhsd,bhse->bhde', kk * jnp.exp(b_last - b), vv)
        return S, o

    S0 = jnp.zeros((B, H, HG_DK, HG_DV), jnp.float32)
    _, o = lax.scan(step, S0, xs)
    o = o.transpose(1, 0, 3, 2, 4).reshape(B, T, H, HG_DV)
    gate = jax.nn.sigmoid(g_out.astype(jnp.float32).reshape(B, T, H, HG_DV))
    return (rmsnorm(o, norm_gain) * gate).reshape(B, T, H * HG_DV)


def conv_ffn(h, w_up, conv_w, conv_b, w_down):
    u = h @ w_up
    T = u.shape[1]
    up = jnp.pad(u, ((0, 0), (CONV_W - 1, 0), (0, 0)))
    c = conv_b
    for j in range(CONV_W):
        c = c + up[:, j:j + T] * conv_w[j]
    gate, val = jnp.split(c, 2, axis=-1)
    return (jax.nn.silu(gate) * val) @ w_down


def setup_inputs(seed: int = 0) -> dict:
    key = jax.random.key(seed)
    ks = jax.random.split(key, 18)

    def nrm(k, shape, scale):
        return jax.random.normal(k, shape, jnp.float32) * scale

    res = (2 * DEPTH) ** -0.5
    return {
        'x': nrm(ks[0], (BATCH, SEQ, D_MODEL), 1.0),
        'norm_attn': 1.0 + nrm(ks[1], (DEPTH, D_MODEL), 0.02),
        'w_in': nrm(ks[2], (DEPTH, D_MODEL, IN_COLS), D_MODEL ** -0.5),
        'nsa_q_gain': 1.0 + nrm(ks[3], (DEPTH, HEAD_DIM), 0.02),
        'nsa_k_gain': 1.0 + nrm(ks[4], (DEPTH, 3, HEAD_DIM), 0.02),
        'cmp_pos': nrm(ks[5], (DEPTH, 2, CMP_BLOCK, HEAD_DIM), 0.02),
        'cmp_w1': nrm(ks[6], (DEPTH, 2, CMP_BLOCK, HEAD_DIM, HEAD_DIM), (CMP_BLOCK * HEAD_DIM) ** -0.5),
        'cmp_w2': nrm(ks[7], (DEPTH, 2, HEAD_DIM, HEAD_DIM), HEAD_DIM ** -0.5),
        'rel_bias': nrm(ks[8], (REL_BUCKETS, NSA_HEADS), 0.1),
        'hg_lower_bound': nrm(ks[9], (DEPTH, HG_HEADS * HG_DK), 0.1),
        'hg_norm_gain': 1.0 + nrm(ks[10], (DEPTH, HG_DV), 0.02),
        'w_branch': nrm(ks[11], (DEPTH, N_BRANCH, BRANCH_W, D_MODEL), BRANCH_W ** -0.5),
        'w_out': nrm(ks[12], (DEPTH, D_MODEL, D_MODEL), D_MODEL ** -0.5 * res),
        'norm_ffn': 1.0 + nrm(ks[13], (DEPTH, D_MODEL), 0.02),
        'w_up': nrm(ks[14], (DEPTH, D_MODEL, 2 * D_FF), D_MODEL ** -0.5),
        'conv_w': nrm(ks[15], (DEPTH, CONV_W, 2 * D_FF), CONV_W ** -0.5),
        'conv_b': nrm(ks[16], (DEPTH, 2 * D_FF), 0.02),
        'w_down': nrm(ks[17], (DEPTH, D_FF, D_MODEL), D_FF ** -0.5 * res),
    }


def reference(x, norm_attn, w_in, nsa_q_gain, nsa_k_gain, cmp_pos, cmp_w1, cmp_w2, rel_bias,
              hg_lower_bound, hg_norm_gain, w_branch, w_out, norm_ffn, w_up, conv_w, conv_b, w_down):
    B, T, _ = x.shape
    lb_all = jnp.cumsum(jax.nn.softmax(hg_lower_bound.astype(jnp.float32), axis=0), axis=0)
    lb_all = jnp.maximum(lb_all - lb_all[0:1], 0.0)
    split_at = np.cumsum(SPLIT_SIZES)[:-1].tolist()
    for l in range(DEPTH):
        h = rmsnorm(x, norm_attn[l])
        (q_n, kc_n, vc_n, ks_n, vs_n, kw_n, vw_n, g_n, q_s, k_s, v_s,
         q_h, f_h, i_h, g_h, gates) = jnp.split(h @ w_in[l], split_at, axis=-1)
        o_nsa = nsa_mixer(q_n, kc_n, vc_n, ks_n, vs_n, kw_n, vw_n, g_n, nsa_q_gain[l], nsa_k_gain[l],
                          cmp_pos[l], cmp_w1[l], cmp_w2[l], rel_bias).astype(h.dtype)
        o_sb = stick_breaking(q_s, k_s, v_s).astype(h.dtype)
        o_hg = hgrn2(q_h, f_h, i_h, g_h, lb_all[l], hg_norm_gain[l]).astype(h.dtype)
        branches = jnp.stack([o_nsa, o_sb, o_hg], axis=2)
        y = jnp.einsum('btnc,ncd->btnd', branches, w_branch[l])
        gsig = jax.nn.sigmoid(gates.astype(jnp.float32)).reshape(B, T, N_BRANCH, D_MODEL)
        merged = jnp.sum(gsig * y, axis=2).astype(h.dtype)
        x = x + merged @ w_out[l]
        x = x + conv_ffn(rmsnorm(x, norm_ffn[l]), w_up[l], conv_w[l], conv_b[l], w_down[l])
    return x
```

```python
import functools
import math

import numpy as np
import jax
import jax.numpy as jnp
from jax import lax
from jax.experimental import pallas as pl
from jax.experimental.pallas import tpu as pltpu

D_MODEL = 2048
DEPTH = 4
HEAD_DIM = 128
BRANCH_W = D_MODEL // 2
N_BRANCH = 3
NSA_HEADS = BRANCH_W // HEAD_DIM
NSA_KV_HEADS = NSA_HEADS // 4
NSA_GROUP = NSA_HEADS // NSA_KV_HEADS
CMP_BLOCK = 32
CMP_STRIDE = 16
SEL_BLOCK = 64
SEL_TOPK = 16
WINDOW = 512
Q_BLOCK = 128
SB_HEADS = BRANCH_W // HEAD_DIM
HG_DK = 128
HG_DV = 128
HG_HEADS = BRANCH_W // HG_DV
HG_CHUNK = 64
HG_SUB = 16
D_FF = 256 * ((8 * D_MODEL // 3 + 255) // 256)
CONV_W = 3
REL_BUCKETS = 32
REL_MAX_DIST = 128
EPS = 1e-6
NEG = -1e30
TINY = 1e-30

NSA_Q = NSA_HEADS * HEAD_DIM
NSA_KV = NSA_KV_HEADS * HEAD_DIM
NSA_GATE = 3 * NSA_HEADS
SB_W = SB_HEADS * HEAD_DIM
HG_KW = HG_HEADS * HG_DK
HG_VW = HG_HEADS * HG_DV
SPLIT_SIZES = (NSA_Q,) + (NSA_KV,) * 6 + (NSA_GATE,) + (SB_W,) * 3 + (HG_KW, HG_KW, HG_VW, HG_VW, N_BRANCH * D_MODEL)
(_Q_N, _KC, _VC, _KS, _VS, _KW, _VW, _G_N, _Q_S, _K_S, _V_S, _Q_H, _F_H, _I_H, _G_H, _GATES) = range(16)
_SPLIT_OFF = np.concatenate([[0], np.cumsum(SPLIT_SIZES)]).tolist()

LANE = 128
_A_ORDER = (_Q_N, _KS, _VS, _KW, _VW, _Q_S, _K_S, _V_S, _Q_H, _I_H)
_A_BLK = {}
_off = 0
for _s in _A_ORDER:
    _A_BLK[_s] = _off // LANE
    _off += SPLIT_SIZES[_s]
A_COLS = _off
_B_ORDER = (_F_H, _G_H, _GATES)
_B_BLK = {}
_off = 0
for _s in _B_ORDER:
    _B_BLK[_s] = _off // LANE
    _off += SPLIT_SIZES[_s]
B_COLS = _off

VMEM_LIMIT = 48 * 1024 * 1024

_MXU = jnp.bfloat16
_F32 = jnp.float32


def _dot(a, b):
    return jnp.dot(a, b, preferred_element_type=_F32)


def _dot_nt(a, b):
    return lax.dot_general(a, b, (((1,), (1,)), ((), ())), preferred_element_type=_F32)


def _split(x, parts):
    out = []
    r = x
    for _ in range(parts):
        h = r.astype(_MXU)
        out.append(h)
        r = r - h.astype(_F32)
    return out


def _params(*sem):
    return pltpu.CompilerParams(dimension_semantics=sem, vmem_limit_bytes=VMEM_LIMIT)


def _rms(x, gain):
    return x * lax.rsqrt(jnp.mean(x * x, axis=-1, keepdims=True) + EPS) * gain


def _log_sigmoid_pair(z):
    lp = jnp.log1p(jnp.exp(-jnp.abs(z)))
    return jnp.minimum(z, 0.0) - lp, jnp.minimum(-z, 0.0) - lp


def _rmsnorm_kernel(x_ref, g_ref, o_ref):
    o_ref[...] = _rms(x_ref[...], g_ref[...]).astype(o_ref.dtype)


def rmsnorm_cast(x2d, gain):
    m, d = x2d.shape
    tm = 512
    return pl.pallas_call(
        _rmsnorm_kernel,
        grid=(m // tm,),
        in_specs=[pl.BlockSpec((tm, d), lambda i: (i, 0)),
                  pl.BlockSpec((1, d), lambda i: (0, 0))],
        out_specs=pl.BlockSpec((tm, d), lambda i: (i, 0)),
        out_shape=jax.ShapeDtypeStruct((m, d), _MXU),
        compiler_params=_params("parallel"),
        name="rmsnorm_cast",
    )(x2d, gain.reshape(1, d))


def _mm_kernel(a_ref, w_ref, o_ref):
    o_ref[...] = _dot(a_ref[...], w_ref[...]).astype(o_ref.dtype)


def _mm_res_kernel(a_ref, w_ref, r_ref, o_ref):
    o_ref[...] = (r_ref[...] + _dot(a_ref[...], w_ref[...])).astype(o_ref.dtype)


def matmul(a, w, out_dtype, residual=None, tm=1024, tn=512, name="matmul"):
    m, k = a.shape
    n = w.shape[1]
    tm = min(tm, m)
    tn = min(tn, n)
    in_specs = [pl.BlockSpec((tm, k), lambda i, j: (i, 0)),
                pl.BlockSpec((k, tn), lambda i, j: (0, j))]
    args = [a, w]
    kern = _mm_kernel
    if residual is not None:
        in_specs.append(pl.BlockSpec((tm, tn), lambda i, j: (i, j)))
        args.append(residual)
        kern = _mm_res_kernel
    return pl.pallas_call(
        kern,
        grid=(m // tm, n // tn),
        in_specs=in_specs,
        out_specs=pl.BlockSpec((tm, tn), lambda i, j: (i, j)),
        out_shape=jax.ShapeDtypeStruct((m, n), out_dtype),
        compiler_params=_params("parallel", "arbitrary"),
        name=name,
    )(*args)


def _compress_kernel(x_ref, wa_ref, wb_ref, pa_ref, pb_ref, w2_ref, gain_ref, o_ref, *, normalize):
    x = x_ref[0].astype(_F32)
    u = _dot((x + pa_ref[...]).astype(_MXU), wa_ref[...])
    v = _dot((x + pb_ref[...]).astype(_MXU), wb_ref[...])
    hid = jax.nn.gelu(u + pltpu.roll(v, v.shape[0] - 1, axis=0))
    out = _dot(hid.astype(_MXU), w2_ref[...])
    if normalize:
        out = jnp.concatenate(
            [_rms(out[:, g * HEAD_DIM:(g + 1) * HEAD_DIM], gain_ref[...]) for g in range(NSA_KV_HEADS)], axis=1)
    o_ref[0] = out.astype(o_ref.dtype)


def compress(x, w1, w2, pos, gain, normalize):
    bsz = x.shape[0]
    nrow = x.shape[1]
    kdim = x.shape[2]
    half = CMP_BLOCK // 2
    eye = jnp.eye(NSA_KV_HEADS, dtype=w1.dtype)

    def embed(w):
        return jnp.einsum('lde,gh->lgdhe', w, eye).reshape(kdim, NSA_KV).astype(_MXU)

    def tile_pos(p):
        return jnp.broadcast_to(p[:, None, :], (half, NSA_KV_HEADS, HEAD_DIM)).reshape(1, kdim)

    w2b = jnp.einsum('de,gh->gdhe', w2, eye).reshape(NSA_KV, NSA_KV).astype(_MXU)
    return pl.pallas_call(
        functools.partial(_compress_kernel, normalize=normalize),
        grid=(bsz,),
        in_specs=[pl.BlockSpec((1, nrow, kdim), lambda b: (b, 0, 0)),
                  pl.BlockSpec((kdim, NSA_KV), lambda b: (0, 0)),
                  pl.BlockSpec((kdim, NSA_KV), lambda b: (0, 0)),
                  pl.BlockSpec((1, kdim), lambda b: (0, 0)),
                  pl.BlockSpec((1, kdim), lambda b: (0, 0)),
                  pl.BlockSpec((NSA_KV, NSA_KV), lambda b: (0, 0)),
                  pl.BlockSpec((1, HEAD_DIM), lambda b: (0, 0))],
        out_specs=pl.BlockSpec((1, nrow, NSA_KV), lambda b: (b, 0, 0)),
        out_shape=jax.ShapeDtypeStruct((bsz, nrow, NSA_KV), _MXU),
        compiler_params=_params("parallel"),
        name="nsa_compress",
    )(x, embed(w1[:half]), embed(w1[half:]), tile_pos(pos[:half]), tile_pos(pos[half:]), w2b,
      gain.reshape(1, HEAD_DIM))


def _nsa_kernel(q_ref, kc_ref, vc_ref, ks_ref, vs_ref, kw_ref, vw_ref, gate_ref,
                bdiag_ref, boff_ref, bfar_ref, bcmp_ref, qg_ref, kg_ref, ov_ref,
                o_ref, ksn_ref, kwn_ref, m_ref, l_ref, acc_ref, ocmp_ref, *, seq):
    R = NSA_GROUP
    QB = Q_BLOCK
    i = pl.program_id(2)
    scale = HEAD_DIM ** -0.5

    @pl.when(i == 0)
    def _():
        def body(c, carry):
            off = pl.multiple_of(c * QB, QB)
            ksn_ref[pl.ds(off, QB), :] = _rms(ks_ref[pl.ds(off, QB), :].astype(_F32), kg_ref[1:2, :]).astype(_MXU)
            kwn_ref[pl.ds(off, QB), :] = _rms(kw_ref[pl.ds(off, QB), :].astype(_F32), kg_ref[2:3, :]).astype(_MXU)
            return carry
        lax.fori_loop(0, seq // QB, body, 0)

    row = lax.broadcasted_iota(jnp.int32, (QB, LANE), 0)
    col = lax.broadcasted_iota(jnp.int32, (QB, LANE), 1)
    t_abs = i * QB + row

    qn = [_rms(q_ref[:, r * HEAD_DIM:(r + 1) * HEAD_DIM].astype(_F32), qg_ref[...]).astype(_MXU) for r in range(R)]

    valid_c = t_abs >= CMP_STRIDE * col + (CMP_BLOCK - 1)
    kc = kc_ref[0]
    vc = vc_ref[0]
    psum = jnp.zeros((QB, LANE), _F32)
    for r in range(R):
        s = _dot_nt(qn[r], kc) * scale + bcmp_ref[r]
        s = jnp.where(valid_c, s, NEG)
        e = jnp.where(valid_c, jnp.exp(s - jnp.max(s, axis=-1, keepdims=True)), 0.0)
        p = e / jnp.maximum(jnp.sum(e, axis=-1, keepdims=True), TINY)
        ocmp_ref[r] = _dot(p.astype(_MXU), vc)
        psum = psum + p
    ov = ov_ref[...]
    imp = functools.reduce(lambda a, b: a + b, [_dot(h, ov) for h in _split(psum, 2)])

    qblk = lax.shift_right_arithmetic(t_abs, int(math.log2(SEL_BLOCK)))
    causal_b = col <= qblk
    forced = causal_b & ((col == 0) | (col >= qblk - 1))
    score = jnp.where(forced, jnp.inf, jnp.where(causal_b, imp, -jnp.inf))
    rank = jnp.zeros((QB, LANE), jnp.int32)
    for jj in range(seq // SEL_BLOCK):
        sc = score[:, jj:jj + 1]
        beats = (sc > score) | ((sc == score) & (col > jj))
        rank = rank + beats.astype(jnp.int32)
    selected = (rank < SEL_TOPK) & causal_b
    sel_f = jnp.where(selected, 1.0, 0.0).astype(_MXU)

    m_ref[...] = jnp.full(m_ref.shape, NEG, _F32)
    l_ref[...] = jnp.zeros(l_ref.shape, _F32)
    acc_ref[...] = jnp.zeros(acc_ref.shape, _F32)

    def update(slot, s, mask, vb):
        if mask is not None:
            s = jnp.where(mask, s, NEG)
        m_old = m_ref[slot]
        m_new = jnp.maximum(m_old, jnp.max(s, axis=-1, keepdims=True))
        alpha = jnp.exp(m_old - m_new)
        p = jnp.exp(s - m_new)
        if mask is not None:
            p = jnp.where(mask, p, 0.0)
        l_ref[slot] = alpha * l_ref[slot] + jnp.sum(p, axis=-1, keepdims=True)
        acc_ref[slot] = alpha * acc_ref[slot] + _dot(p.astype(_MXU), vb)
        m_ref[slot] = m_new

    def sel_mask(c):
        expand = jnp.where(row == 2 * c + lax.shift_right_arithmetic(col, int(math.log2(SEL_BLOCK))), 1.0, 0.0)
        return _dot(sel_f, expand.astype(_MXU)) > 0.5

    def sel_step(c, bias_of, extra_mask):
        off = pl.multiple_of(c * QB, QB)
        kb = ksn_ref[pl.ds(off, QB), :]
        vb = vs_ref[pl.ds(off, QB), :].astype(_MXU)
        mask = sel_mask(c)
        if extra_mask is not None:
            mask = mask & extra_mask
        for r in range(R):
            update(r, _dot_nt(qn[r], kb) * scale + bias_of(r), mask, vb)

    def win_step(c, bias_of, mask):
        off = pl.multiple_of(c * QB, QB)
        kb = kwn_ref[pl.ds(off, QB), :]
        vb = vw_ref[pl.ds(off, QB), :].astype(_MXU)
        for r in range(R):
            update(R + r, _dot_nt(qn[r], kb) * scale + bias_of(r), mask, vb)

    far = lambda r: bfar_ref[r]
    off1 = lambda r: boff_ref[r]
    diag = lambda r: bdiag_ref[r]
    lower = col <= row

    def far_body(c, carry):
        sel_step(c, far, None)
        return carry
    lax.fori_loop(0, jnp.maximum(i - 1, 0), far_body, 0)

    @pl.when(i >= 1)
    def _():
        sel_step(i - 1, off1, None)
        win_step(i - 1, off1, None)

    sel_step(i, diag, lower)
    win_step(i, diag, lower)

    for d in (2, 3):
        @pl.when(i >= d)
        def _():
            win_step(i - d, far, None)

    nwb = WINDOW // QB

    @pl.when(i >= nwb)
    def _():
        win_step(i - nwb, far, col > row)

    gate = jax.nn.sigmoid(gate_ref[...])
    for r in range(R):
        o_sel = acc_ref[r] / jnp.maximum(l_ref[r], TINY)
        o_win = acc_ref[R + r] / jnp.maximum(l_ref[R + r], TINY)
        o = (gate[:, 3 * r:3 * r + 1] * ocmp_ref[r] + gate[:, 3 * r + 1:3 * r + 2] * o_sel
             + gate[:, 3 * r + 2:3 * r + 3] * o_win)
        o_ref[:, r * HEAD_DIM:(r + 1) * HEAD_DIM] = o.astype(o_ref.dtype)


def nsa_attention(pa, kcmp, vcmp, pg, tables, q_gain, k_gain, bsz, seq):
    nqb = seq // Q_BLOCK
    G, R = NSA_KV_HEADS, NSA_GROUP
    bdiag, boff, bfar, bcmp, overlap = tables

    def seq_spec(seg):
        return pl.BlockSpec((seq, HEAD_DIM), lambda b, g, i, s=_A_BLK[seg]: (b, s + g))

    head_tile = pl.BlockSpec((R, Q_BLOCK, LANE), lambda b, g, i: (g, 0, 0))
    return pl.pallas_call(
        functools.partial(_nsa_kernel, seq=seq),
        grid=(bsz, G, nqb),
        in_specs=[pl.BlockSpec((Q_BLOCK, R * HEAD_DIM), lambda b, g, i: (b * nqb + i, g)),
                  pl.BlockSpec((1, seq // CMP_STRIDE, HEAD_DIM), lambda b, g, i: (b, 0, g)),
                  pl.BlockSpec((1, seq // CMP_STRIDE, HEAD_DIM), lambda b, g, i: (b, 0, g)),
                  seq_spec(_KS), seq_spec(_VS), seq_spec(_KW), seq_spec(_VW),
                  pl.BlockSpec((Q_BLOCK, LANE), lambda b, g, i: (b * nqb + i, g)),
                  head_tile, head_tile,
                  pl.BlockSpec((R, 1, LANE), lambda b, g, i: (g, 0, 0)),
                  pl.BlockSpec((R, Q_BLOCK, LANE), lambda b, g, i: (g, i, 0)),
                  pl.BlockSpec((1, HEAD_DIM), lambda b, g, i: (0, 0)),
                  pl.BlockSpec((3, HEAD_DIM), lambda b, g, i: (0, 0)),
                  pl.BlockSpec((LANE, LANE), lambda b, g, i: (0, 0))],
        out_specs=pl.BlockSpec((Q_BLOCK, R * HEAD_DIM), lambda b, g, i: (b * nqb + i, g)),
        out_shape=jax.ShapeDtypeStruct((bsz * seq, NSA_Q), _MXU),
        scratch_shapes=[pltpu.VMEM((seq, HEAD_DIM), _MXU),
                        pltpu.VMEM((seq, HEAD_DIM), _MXU),
                        pltpu.VMEM((2 * R, Q_BLOCK, 1), _F32),
                        pltpu.VMEM((2 * R, Q_BLOCK, 1), _F32),
                        pltpu.VMEM((2 * R, Q_BLOCK, HEAD_DIM), _F32),
                        pltpu.VMEM((R, Q_BLOCK, HEAD_DIM), _F32)],
        compiler_params=_params("parallel", "parallel", "arbitrary"),
        name="nsa_attention",
    )(pa, kcmp, vcmp, pa, pa, pa, pa, pg, bdiag, boff, bfar, bcmp,
      q_gain.reshape(1, HEAD_DIM), k_gain, overlap)


def _t5_bucket(dist):
    n = jnp.maximum(dist, 0)
    exact = REL_BUCKETS // 2
    big = exact + (jnp.log(jnp.maximum(n, 1).astype(jnp.float32) / exact)
                   / math.log(REL_MAX_DIST / exact) * (REL_BUCKETS - exact)).astype(jnp.int32)
    return jnp.where(n < exact, n, jnp.minimum(big, REL_BUCKETS - 1))


def nsa_tables(rel_bias, seq):
    tab_h = rel_bias.T.astype(_F32)
    r = np.arange(Q_BLOCK)[:, None]
    c = np.arange(LANE)[None, :]
    bdiag = jnp.take(tab_h, _t5_bucket(jnp.asarray(r - c)), axis=1)
    boff = jnp.take(tab_h, _t5_bucket(jnp.asarray(Q_BLOCK + r - c)), axis=1)
    bfar = jnp.broadcast_to(jnp.take(tab_h, _t5_bucket(jnp.asarray([2 * Q_BLOCK])), axis=1)[:, None, :],
                            (NSA_HEADS, 1, LANE))
    t = np.arange(seq)[:, None]
    dist_c = t - (CMP_STRIDE * c + CMP_BLOCK - 1)
    bcmp = jnp.take(tab_h, _t5_bucket(jnp.asarray(dist_c)), axis=1)
    n_cmp = (seq - CMP_BLOCK) // CMP_STRIDE + 1
    n_sel = seq // SEL_BLOCK
    c_start = np.arange(LANE) * CMP_STRIDE
    s_start = np.arange(LANE) * SEL_BLOCK
    overlap = ((c_start[:, None] < s_start[None, :] + SEL_BLOCK)
               & (c_start[:, None] + CMP_BLOCK > s_start[None, :])
               & (np.arange(LANE)[:, None] < n_cmp) & (np.arange(LANE)[None, :] < n_sel))
    return bdiag, boff, bfar, bcmp, jnp.asarray(overlap.astype(np.float32)).astype(_MXU)


def _sb_kernel(q_ref, k_ref, v_ref, o_ref):
    QB = Q_BLOCK
    i = pl.program_id(2)
    scale = HEAD_DIM ** -0.5
    q = q_ref[...]
    row = lax.broadcasted_iota(jnp.int32, (QB, QB), 0)
    col = lax.broadcasted_iota(jnp.int32, (QB, QB), 1)
    later = jnp.where(row > col, 1.0, 0.0).astype(_MXU)

    def block(c, rest, acc, strict):
        off = pl.multiple_of(c * QB, QB)
        z = _dot_nt(q, k_ref[pl.ds(off, QB), :]) * scale
        log_b, log_1mb = _log_sigmoid_pair(z)
        if strict is not None:
            log_1mb = jnp.where(strict, log_1mb, 0.0)
        between = functools.reduce(lambda a, b: a + b, [_dot(h, later) for h in _split(log_1mb, 2)])
        a = jnp.exp(log_b + between + rest)
        if strict is not None:
            a = jnp.where(strict, a, 0.0)
        acc = acc + _dot(a.astype(_MXU), v_ref[pl.ds(off, QB), :])
        rest = rest + jnp.sum(log_1mb, axis=-1, keepdims=True)
        return rest, acc

    rest, acc = block(i, jnp.zeros((QB, 1), _F32), jnp.zeros((QB, HEAD_DIM), _F32), col < row)

    def body(n, carry):
        return block(i - 1 - n, carry[0], carry[1], None)
    rest, acc = lax.fori_loop(0, i, body, (rest, acc))
    o_ref[...] = acc.astype(o_ref.dtype)


def stick_breaking(pa, bsz, seq):
    nqb = seq // Q_BLOCK

    def seq_spec(seg):
        return pl.BlockSpec((seq, HEAD_DIM), lambda b, h, i, s=_A_BLK[seg]: (b, s + h))

    return pl.pallas_call(
        _sb_kernel,
        grid=(bsz, SB_HEADS, nqb),
        in_specs=[pl.BlockSpec((Q_BLOCK, HEAD_DIM), lambda b, h, i, s=_A_BLK[_Q_S]: (b * nqb + i, s + h)),
                  seq_spec(_K_S), seq_spec(_V_S)],
        out_specs=pl.BlockSpec((Q_BLOCK, HEAD_DIM), lambda b, h, i: (b * nqb + i, h)),
        out_shape=jax.ShapeDtypeStruct((bsz * seq, SB_W), _MXU),
        compiler_params=_params("parallel", "parallel", "arbitrary"),
        name="stick_breaking",
    )(pa, pa, pa)


def _hgrn_kernel(q_ref, f_ref, i_ref, g_ref, llb_ref, l1lb_ref, omlb_ref, ng_ref, o_ref, *, seq):
    C, SUB = HG_CHUNK, HG_SUB
    nsub = C // SUB
    row = lax.broadcasted_iota(jnp.int32, (C, C), 0)
    col = lax.broadcasted_iota(jnp.int32, (C, C), 1)
    upto = jnp.where(col <= row, 1.0, 0.0).astype(_MXU)
    sub_t = lax.broadcasted_iota(jnp.int32, (SUB, 1), 0)
    sub_c = lax.broadcasted_iota(jnp.int32, (SUB, C), 1)

    def chunk(n, state_t):
        off = pl.multiple_of(n * C, C)
        fp = f_ref[pl.ds(off, C), :]
        log_sig, _ = _log_sigmoid_pair(fp)
        y = l1lb_ref[...] + log_sig
        a = llb_ref[...]
        log_f = jnp.maximum(a, y) + jnp.log1p(jnp.exp(-jnp.abs(a - y)))
        b = functools.reduce(lambda u, w: u + w, [_dot(upto, h) for h in _split(log_f, 3)])
        q = q_ref[pl.ds(off, C), :].astype(_F32)
        k = omlb_ref[...] * jax.nn.sigmoid(-fp)
        v = i_ref[pl.ds(off, C), :].astype(_F32)
        v_m = v.astype(_MXU)

        o = _dot_nt((q * jnp.exp(b)).astype(_MXU), state_t.astype(_MXU))

        rows = [jnp.zeros((SUB, C), _F32)]
        for s_i in range(1, nsub):
            lo = s_i * SUB
            b_ref = b[lo - 1:lo, :]
            q_i = q[lo:lo + SUB, :] * jnp.exp(b[lo:lo + SUB, :] - b_ref)
            k_i = k * jnp.exp(jnp.minimum(b_ref - b, 0.0))
            a_i = _dot_nt(q_i.astype(_MXU), k_i.astype(_MXU))
            rows.append(jnp.where(sub_c < lo, a_i, 0.0))
        o = o + _dot(jnp.concatenate(rows, axis=0).astype(_MXU), v_m)

        diag = []
        for s_i in range(nsub):
            lo = s_i * SUB
            b_i = b[lo:lo + SUB, :]
            q_i = q[lo:lo + SUB, :]
            o_d = jnp.zeros((SUB, HG_DV), _F32)
            for s in range(SUB):
                w = jnp.exp(jnp.minimum(b_i - b[lo + s:lo + s + 1, :], 0.0))
                a_col = jnp.sum(q_i * k[lo + s:lo + s + 1, :] * w, axis=-1, keepdims=True)
                a_col = jnp.where(sub_t >= s, a_col, 0.0)
                o_d = o_d + a_col * v[lo + s:lo + s + 1, :]
            diag.append(o_d)
        o = o + jnp.concatenate(diag, axis=0)

        b_last = b[C - 1:C, :]
        k_dec = (k * jnp.exp(b_last - b)).astype(_MXU)
        state_t = state_t * jnp.exp(b_last) + _dot(v.T.astype(_MXU), k_dec)
        gate = jax.nn.sigmoid(g_ref[pl.ds(off, C), :])
        o_ref[pl.ds(off, C), :] = (_rms(o, ng_ref[...]) * gate).astype(o_ref.dtype)
        return state_t

    lax.fori_loop(0, seq // C, chunk, jnp.zeros((HG_DV, HG_DK), _F32))


def hgrn2(pa, pb, lb, norm_gain, bsz, seq):
    lb = lb.reshape(1, HG_KW).astype(_F32)
    log_lb = jnp.log(lb)
    log_1mlb = jnp.log1p(-lb)
    one_mlb = 1.0 - lb

    def seq_spec(blk):
        return pl.BlockSpec((seq, HG_DK), lambda b, h, s=blk: (b, s + h))

    head_vec = pl.BlockSpec((1, HG_DK), lambda b, h: (0, h))
    return pl.pallas_call(
        functools.partial(_hgrn_kernel, seq=seq),
        grid=(bsz, HG_HEADS),
        in_specs=[seq_spec(_A_BLK[_Q_H]), seq_spec(_B_BLK[_F_H]), seq_spec(_A_BLK[_I_H]), seq_spec(_B_BLK[_G_H]),
                  head_vec, head_vec, head_vec,
                  pl.BlockSpec((1, HG_DV), lambda b, h: (0, 0))],
        out_specs=pl.BlockSpec((seq, HG_DV), lambda b, h: (b, h)),
        out_shape=jax.ShapeDtypeStruct((bsz * seq, HG_VW), _MXU),
        compiler_params=_params("parallel", "parallel"),
        name="hgrn2",
    )(pa, pb, pa, pb, log_lb, log_1mlb, one_mlb, norm_gain.reshape(1, HG_DV))


def _merge_kernel(o0_ref, o1_ref, o2_ref, w0_ref, w1_ref, w2_ref, g0_ref, g1_ref, g2_ref, out_ref):
    acc = jax.nn.sigmoid(g0_ref[...]) * _dot(o0_ref[...], w0_ref[0])
    acc = acc + jax.nn.sigmoid(g1_ref[...]) * _dot(o1_ref[...], w1_ref[0])
    acc = acc + jax.nn.sigmoid(g2_ref[...]) * _dot(o2_ref[...], w2_ref[0])
    out_ref[...] = acc.astype(out_ref.dtype)


def merge_branches(o_nsa, o_sb, o_hg, w_branch, pb, tm=1024, tn=512):
    m = o_nsa.shape[0]
    nj = D_MODEL // tn
    gblk = _B_BLK[_GATES] * LANE // tn

    def o_spec():
        return pl.BlockSpec((tm, BRANCH_W), lambda i, j: (i, 0))

    def w_spec(n):
        return pl.BlockSpec((1, BRANCH_W, tn), lambda i, j, n=n: (n, 0, j))

    def g_spec(n):
        return pl.BlockSpec((tm, tn), lambda i, j, n=n: (i, gblk + n * nj + j))

    return pl.pallas_call(
        _merge_kernel,
        grid=(m // tm, nj),
        in_specs=[o_spec(), o_spec(), o_spec(), w_spec(0), w_spec(1), w_spec(2), g_spec(0), g_spec(1), g_spec(2)],
        out_specs=pl.BlockSpec((tm, tn), lambda i, j: (i, j)),
        out_shape=jax.ShapeDtypeStruct((m, D_MODEL), _MXU),
        compiler_params=_params("parallel", "arbitrary"),
        name="merge_branches",
    )(o_nsa, o_sb, o_hg, w_branch, w_branch, w_branch, pb, pb, pb)


def _ffn_up_kernel(h_ref, halo_ref, wg_ref, wv_ref, cwg_ref, cwv_ref, cbg_ref, cbv_ref, o_ref, *, tiles_per_seq):
    i = pl.program_id(0)
    first = (i % tiles_per_seq) == 0
    h = h_ref[...]
    halo = halo_ref[...]
    tm = h.shape[0]
    hrows = halo.shape[0]
    row = lax.broadcasted_iota(jnp.int32, (tm, 1), 0)

    def conv(w_ref, cw_ref, cb_ref):
        w = w_ref[...]
        u = _dot(h, w)
        uh = jnp.where(first, 0.0, _dot(halo, w))
        u1 = jnp.where(row == 0, uh[hrows - 1:hrows, :], pltpu.roll(u, 1, axis=0))
        u2 = jnp.where(row == 0, uh[hrows - 2:hrows - 1, :],
                       jnp.where(row == 1, uh[hrows - 1:hrows, :], pltpu.roll(u, 2, axis=0)))
        return cb_ref[...] + u2 * cw_ref[0:1, :] + u1 * cw_ref[1:2, :] + u * cw_ref[2:3, :]

    gate = conv(wg_ref, cwg_ref, cbg_ref)
    val = conv(wv_ref, cwv_ref, cbv_ref)
    o_ref[...] = (jax.nn.silu(gate) * val).astype(o_ref.dtype)


def ffn_up(h, w_up, conv_w, conv_b, seq, tm=1024, tn=512):
    m, k = h.shape
    nj = D_FF // tn
    hrows = 16
    conv_b = conv_b.reshape(1, 2 * D_FF)
    return pl.pallas_call(
        functools.partial(_ffn_up_kernel, tiles_per_seq=seq // tm),
        grid=(m // tm, nj),
        in_specs=[pl.BlockSpec((tm, k), lambda i, j: (i, 0)),
                  pl.BlockSpec((hrows, k), lambda i, j: (jnp.maximum(i * (tm // hrows) - 1, 0), 0)),
                  pl.BlockSpec((k, tn), lambda i, j: (0, j)),
                  pl.BlockSpec((k, tn), lambda i, j: (0, nj + j)),
                  pl.BlockSpec((CONV_W, tn), lambda i, j: (0, j)),
                  pl.BlockSpec((CONV_W, tn), lambda i, j: (0, nj + j)),
                  pl.BlockSpec((1, tn), lambda i, j: (0, j)),
                  pl.BlockSpec((1, tn), lambda i, j: (0, nj + j))],
        out_specs=pl.BlockSpec((tm, tn), lambda i, j: (i, j)),
        out_shape=jax.ShapeDtypeStruct((m, D_FF), _MXU),
        compiler_params=_params("parallel", "arbitrary"),
        name="ffn_up_conv",
    )(h, h, w_up, w_up, conv_w, conv_w, conv_b, conv_b)


def _gather_cols(w, segs):
    return jnp.concatenate([w[:, _SPLIT_OFF[s]:_SPLIT_OFF[s + 1]] for s in segs], axis=1)


def _nsa_gate_cols(w):
    g = w[:, _SPLIT_OFF[_G_N]:_SPLIT_OFF[_G_N + 1]].reshape(-1, NSA_KV_HEADS, NSA_GROUP * 3)
    g = jnp.pad(g, ((0, 0), (0, 0), (0, LANE - NSA_GROUP * 3)))
    return g.reshape(-1, NSA_KV_HEADS * LANE)


def kernel(x, norm_attn, w_in, nsa_q_gain, nsa_k_gain, cmp_pos, cmp_w1, cmp_w2, rel_bias, hg_lower_bound,
           hg_norm_gain, w_branch, w_out, norm_ffn, w_up, conv_w, conv_b, w_down):
    bsz, seq, d = x.shape
    m = bsz * seq
    lb_all = jnp.cumsum(jax.nn.softmax(hg_lower_bound.astype(_F32), axis=0), axis=0)
    lb_all = jnp.maximum(lb_all - lb_all[0:1], 0.0)
    tables = nsa_tables(rel_bias, seq)
    xf = x.reshape(m, d)
    for l in range(w_in.shape[0]):
        wl = w_in[l]
        h = rmsnorm_cast(xf, norm_attn[l])
        pa = matmul(h, _gather_cols(wl, _A_ORDER).astype(_MXU), _MXU, name="proj_a")
        pb = matmul(h, _gather_cols(wl, _B_ORDER).astype(_MXU), _F32, name="proj_b")
        pkc = matmul(h, _gather_cols(wl, (_KC,)).astype(_MXU), _MXU, name="proj_kc")
        pvc = matmul(h, _gather_cols(wl, (_VC,)).astype(_MXU), _MXU, name="proj_vc")
        pg = matmul(h, _nsa_gate_cols(wl).astype(_MXU), _F32, name="proj_g")
        blk_rows = seq // CMP_STRIDE
        kcmp = compress(pkc.reshape(bsz, blk_rows, CMP_STRIDE * NSA_KV), cmp_w1[l, 0], cmp_w2[l, 0], cmp_pos[l, 0],
                        nsa_k_gain[l, 0], True)
        vcmp = compress(pvc.reshape(bsz, blk_rows, CMP_STRIDE * NSA_KV), cmp_w1[l, 1], cmp_w2[l, 1], cmp_pos[l, 1],
                        nsa_k_gain[l, 0], False)
        o_nsa = nsa_attention(pa, kcmp, vcmp, pg, tables, nsa_q_gain[l], nsa_k_gain[l], bsz, seq)
        o_sb = stick_breaking(pa, bsz, seq)
        o_hg = hgrn2(pa, pb, lb_all[l], hg_norm_gain[l], bsz, seq)
        merged = merge_branches(o_nsa, o_sb, o_hg, w_branch[l].astype(_MXU), pb)
        xf = matmul(merged, w_out[l].astype(_MXU), _F32, residual=xf, name="out_proj")
        h2 = rmsnorm_cast(xf, norm_ffn[l])
        act = ffn_up(h2, w_up[l].astype(_MXU), conv_w[l], conv_b[l], seq)
        xf = matmul(act, w_down[l].astype(_MXU), _F32, residual=xf, tm=512, name="ffn_down")
    return xf.reshape(bsz, seq, d)
```

```python
import functools
import math

import numpy as np
import jax
import jax.numpy as jnp
from jax import lax
from jax.experimental import pallas as pl
from jax.experimental.pallas import tpu as pltpu

D_MODEL = 2048
DEPTH = 4
HEAD_DIM = 128
BRANCH_W = D_MODEL // 2
N_BRANCH = 3
NSA_HEADS = BRANCH_W // HEAD_DIM
NSA_KV_HEADS = NSA_HEADS // 4
NSA_GROUP = NSA_HEADS // NSA_KV_HEADS
CMP_BLOCK = 32
CMP_STRIDE = 16
SEL_BLOCK = 64
SEL_TOPK = 16
WINDOW = 512
Q_BLOCK = 128
SB_HEADS = BRANCH_W // HEAD_DIM
HG_DK = 128
HG_DV = 128
HG_HEADS = BRANCH_W // HG_DV
HG_CHUNK = 64
HG_SUB = 16
D_FF = 256 * ((8 * D_MODEL // 3 + 255) // 256)
CONV_W = 3
REL_BUCKETS = 32
REL_MAX_DIST = 128
EPS = 1e-6
NEG = -1e30
TINY = 1e-30

NSA_Q = NSA_HEADS * HEAD_DIM
NSA_KV = NSA_KV_HEADS * HEAD_DIM
NSA_GATE = 3 * NSA_HEADS
SB_W = SB_HEADS * HEAD_DIM
HG_KW = HG_HEADS * HG_DK
HG_VW = HG_HEADS * HG_DV
SPLIT_SIZES = (NSA_Q,) + (NSA_KV,) * 6 + (NSA_GATE,) + (SB_W,) * 3 + (HG_KW, HG_KW, HG_VW, HG_VW, N_BRANCH * D_MODEL)
(_Q_N, _KC, _VC, _KS, _VS, _KW, _VW, _G_N, _Q_S, _K_S, _V_S, _Q_H, _F_H, _I_H, _G_H, _GATES) = range(16)
_SPLIT_OFF = np.concatenate([[0], np.cumsum(SPLIT_SIZES)]).tolist()

LANE = 128
_A_ORDER = (_Q_N, _Q_S, _K_S, _Q_H, _I_H, _KS, _KW)
_A_BLK = {}
_off = 0
for _s in _A_ORDER:
    _A_BLK[_s] = _off // LANE
    _off += SPLIT_SIZES[_s]
A_COLS = _off
_T_ORDER = (_V_S, _VS, _VW)
_T_BLK = {}
_off = 0
for _s in _T_ORDER:
    _T_BLK[_s] = _off // LANE
    _off += SPLIT_SIZES[_s]
T_ROWS = _off
_B_ORDER = (_F_H, _G_H, _GATES)
_B_BLK = {}
_off = 0
for _s in _B_ORDER:
    _B_BLK[_s] = _off // LANE
    _off += SPLIT_SIZES[_s]
B_COLS = _off

VMEM_LIMIT = 48 * 1024 * 1024

_MXU = jnp.bfloat16
_F32 = jnp.float32


def _dot(a, b):
    return jnp.dot(a, b, preferred_element_type=_F32)


def _dot_nt(a, b):
    return lax.dot_general(a, b, (((1,), (1,)), ((), ())), preferred_element_type=_F32)


def _split(x, parts):
    out = []
    r = x
    for _ in range(parts):
        h = r.astype(_MXU)
        out.append(h)
        r = r - h.astype(_F32)
    return out


def _params(*sem):
    return pltpu.CompilerParams(dimension_semantics=sem, vmem_limit_bytes=VMEM_LIMIT)


def _rms(x, gain):
    return x * lax.rsqrt(jnp.mean(x * x, axis=-1, keepdims=True) + EPS) * gain


def _log_sigmoid_pair(z):
    lp = jnp.log(1.0 + jnp.exp(-jnp.abs(z)))
    return jnp.minimum(z, 0.0) - lp, jnp.minimum(-z, 0.0) - lp


def _rmsnorm_kernel(x_ref, g_ref, o_ref):
    o_ref[...] = _rms(x_ref[...], g_ref[...]).astype(o_ref.dtype)


def rmsnorm_cast(x2d, gain):
    m, d = x2d.shape
    tm = 512
    return pl.pallas_call(
        _rmsnorm_kernel,
        grid=(m // tm,),
        in_specs=[pl.BlockSpec((tm, d), lambda i: (i, 0)),
                  pl.BlockSpec((1, d), lambda i: (0, 0))],
        out_specs=pl.BlockSpec((tm, d), lambda i: (i, 0)),
        out_shape=jax.ShapeDtypeStruct((m, d), _MXU),
        compiler_params=_params("parallel"),
        name="rmsnorm_cast",
    )(x2d, gain.reshape(1, d))


def _mm_kernel(a_ref, w_ref, o_ref):
    o_ref[...] = _dot(a_ref[...], w_ref[...]).astype(o_ref.dtype)


def _mm_res_kernel(a_ref, w_ref, r_ref, o_ref):
    o_ref[...] = (r_ref[...] + _dot(a_ref[...], w_ref[...])).astype(o_ref.dtype)


def matmul(a, w, out_dtype, residual=None, tm=1024, tn=512, name="matmul"):
    m, k = a.shape
    n = w.shape[1]
    tm = min(tm, m)
    tn = min(tn, n)
    in_specs = [pl.BlockSpec((tm, k), lambda i, j: (i, 0)),
                pl.BlockSpec((k, tn), lambda i, j: (0, j))]
    args = [a, w]
    kern = _mm_kernel
    if residual is not None:
        in_specs.append(pl.BlockSpec((tm, tn), lambda i, j: (i, j)))
        args.append(residual)
        kern = _mm_res_kernel
    return pl.pallas_call(
        kern,
        grid=(m // tm, n // tn),
        in_specs=in_specs,
        out_specs=pl.BlockSpec((tm, tn), lambda i, j: (i, j)),
        out_shape=jax.ShapeDtypeStruct((m, n), out_dtype),
        compiler_params=_params("parallel", "arbitrary"),
        name=name,
    )(*args)


def _mm_nt_kernel(wt_ref, a_ref, o_ref):
    o_ref[...] = _dot_nt(wt_ref[...], a_ref[...]).astype(o_ref.dtype)


def matmul_nt(wt, a, out_dtype, tm=1024, tn=512, name="matmul_nt"):
    n, k = wt.shape
    m = a.shape[0]
    tm = min(tm, m)
    tn = min(tn, n)
    return pl.pallas_call(
        _mm_nt_kernel,
        grid=(m // tm, n // tn),
        in_specs=[pl.BlockSpec((tn, k), lambda i, j: (j, 0)),
                  pl.BlockSpec((tm, k), lambda i, j: (i, 0))],
        out_specs=pl.BlockSpec((tn, tm), lambda i, j: (j, i)),
        out_shape=jax.ShapeDtypeStruct((n, m), out_dtype),
        compiler_params=_params("parallel", "arbitrary"),
        name=name,
    )(wt, a)


def _compress_kernel(x_ref, wa_ref, wb_ref, pa_ref, pb_ref, w2_ref, gain_ref, o_ref, *, normalize):
    x = x_ref[0].astype(_F32)
    u = _dot((x + pa_ref[...]).astype(_MXU), wa_ref[...])
    v = _dot((x + pb_ref[...]).astype(_MXU), wb_ref[...])
    hid = jax.nn.gelu(u + pltpu.roll(v, v.shape[0] - 1, axis=0)).astype(_MXU)
    if normalize:
        out = _dot(hid, w2_ref[...])
        out = jnp.concatenate(
            [_rms(out[:, g * HEAD_DIM:(g + 1) * HEAD_DIM], gain_ref[...]) for g in range(NSA_KV_HEADS)], axis=1)
    else:
        out = _dot_nt(w2_ref[...], hid)
    o_ref[0] = out.astype(o_ref.dtype)


def compress(x, w1, w2, pos, gain, normalize):
    bsz = x.shape[0]
    nrow = x.shape[1]
    kdim = x.shape[2]
    half = CMP_BLOCK // 2
    eye = jnp.eye(NSA_KV_HEADS, dtype=w1.dtype)

    def embed(w):
        return jnp.einsum('lde,gh->lgdhe', w, eye).reshape(kdim, NSA_KV).astype(_MXU)

    def tile_pos(p):
        return jnp.broadcast_to(p[:, None, :], (half, NSA_KV_HEADS, HEAD_DIM)).reshape(1, kdim)

    w2b = jnp.einsum('de,gh->gdhe', w2, eye).reshape(NSA_KV, NSA_KV).astype(_MXU)
    out_block = (1, nrow, NSA_KV)
    if not normalize:
        w2b = w2b.T
        out_block = (1, NSA_KV, nrow)
    return pl.pallas_call(
        functools.partial(_compress_kernel, normalize=normalize),
        grid=(bsz,),
        in_specs=[pl.BlockSpec((1, nrow, kdim), lambda b: (b, 0, 0)),
                  pl.BlockSpec((kdim, NSA_KV), lambda b: (0, 0)),
                  pl.BlockSpec((kdim, NSA_KV), lambda b: (0, 0)),
                  pl.BlockSpec((1, kdim), lambda b: (0, 0)),
                  pl.BlockSpec((1, kdim), lambda b: (0, 0)),
                  pl.BlockSpec((NSA_KV, NSA_KV), lambda b: (0, 0)),
                  pl.BlockSpec((1, HEAD_DIM), lambda b: (0, 0))],
        out_specs=pl.BlockSpec(out_block, lambda b: (b, 0, 0)),
        out_shape=jax.ShapeDtypeStruct((bsz,) + out_block[1:], _MXU),
        compiler_params=_params("parallel"),
        name="nsa_compress",
    )(x, embed(w1[:half]), embed(w1[half:]), tile_pos(pos[:half]), tile_pos(pos[half:]), w2b,
      gain.reshape(1, HEAD_DIM))


def _nsa_kernel(q_ref, kc_ref, vct_ref, ks_ref, vst_ref, kw_ref, vwt_ref, gate_ref,
                bdiag_ref, boff_ref, bfar_ref, bcmp_ref, qg_ref, kg_ref, ovt_ref,
                o_ref, ksn_ref, kwn_ref, m_ref, l_ref, acc_ref, ocmp_ref, sel_ref, *, seq):
    R = NSA_GROUP
    QB = Q_BLOCK
    W = R * QB
    i = pl.program_id(2)
    scale = HEAD_DIM ** -0.5
    sel_shift = int(math.log2(SEL_BLOCK))

    @pl.when(i == 0)
    def _():
        def body(c, carry):
            off = pl.multiple_of(c * QB, QB)
            ksn_ref[pl.ds(off, QB), :] = _rms(ks_ref[pl.ds(off, QB), :].astype(_F32), kg_ref[1:2, :]).astype(_MXU)
            kwn_ref[pl.ds(off, QB), :] = _rms(kw_ref[pl.ds(off, QB), :].astype(_F32), kg_ref[2:3, :]).astype(_MXU)
            return carry
        lax.fori_loop(0, seq // QB, body, 0)

    heads = [slice(r * QB, (r + 1) * QB) for r in range(R)]
    q_all = jnp.concatenate(
        [_rms(q_ref[:, hs].astype(_F32), qg_ref[...]).astype(_MXU) for hs in heads], axis=0)

    row = lax.broadcasted_iota(jnp.int32, (QB, QB), 0)
    col = lax.broadcasted_iota(jnp.int32, (QB, QB), 1)

    valid_c = i * QB + col >= CMP_STRIDE * row + (CMP_BLOCK - 1)
    s_c = _dot_nt(kc_ref[0], q_all) * scale + bcmp_ref[0, 0]
    p_parts = []
    psum = jnp.zeros((QB, QB), _F32)
    for hs in heads:
        s = jnp.where(valid_c, s_c[:, hs], NEG)
        e = jnp.where(valid_c, jnp.exp(s - jnp.max(s, axis=0, keepdims=True)), 0.0)
        p = e * (1.0 / jnp.maximum(jnp.sum(e, axis=0, keepdims=True), TINY))
        psum = psum + p
        p_parts.append(p.astype(_MXU))
    ocmp_ref[...] = _dot(vct_ref[0], jnp.concatenate(p_parts, axis=1))
    ovt = ovt_ref[...]
    imp = functools.reduce(lambda a, b: a + b, [_dot(ovt, h) for h in _split(psum, 2)])

    n_sel = seq // SEL_BLOCK
    jrow = lax.broadcasted_iota(jnp.int32, (n_sel, QB), 0)
    tcol = lax.broadcasted_iota(jnp.int32, (n_sel, QB), 1)
    qblk = lax.shift_right_arithmetic(i * QB + tcol, sel_shift)
    causal_b = jrow <= qblk
    forced = causal_b & ((jrow == 0) | (jrow >= qblk - 1))
    score = jnp.where(forced, jnp.inf, jnp.where(causal_b, imp[:n_sel, :], -jnp.inf))
    rank = jnp.zeros((n_sel, QB), jnp.int32)
    for jj in range(n_sel):
        sc = score[jj:jj + 1, :]
        beats = (sc > score) | ((sc == score) & (jrow > jj))
        rank = rank + beats.astype(jnp.int32)
    sel_ref[...] = jnp.where((rank < SEL_TOPK) & causal_b, 1.0, 0.0)

    m_ref[...] = jnp.full(m_ref.shape, NEG, _F32)
    l_ref[...] = jnp.zeros(l_ref.shape, _F32)
    acc_ref[...] = jnp.zeros(acc_ref.shape, _F32)

    def attend(jobs):
        staged = []
        for branch, s_all, mask, vt in jobs:
            p_parts, alphas = [], []
            for r, hs in enumerate(heads):
                slot = branch * R + r
                s = s_all[:, hs]
                if mask is not None:
                    s = jnp.where(mask, s, NEG)
                m_old = m_ref[slot]
                m_new = jnp.maximum(m_old, jnp.max(s, axis=0, keepdims=True))
                alpha = jnp.exp(m_old - m_new)
                p = jnp.exp(s - m_new)
                if mask is not None:
                    p = jnp.where(mask, p, 0.0)
                l_ref[slot] = alpha * l_ref[slot] + jnp.sum(p, axis=0, keepdims=True)
                m_ref[slot] = m_new
                p_parts.append(p.astype(_MXU))
                alphas.append(alpha)
            staged.append((branch, vt, jnp.concatenate(p_parts, axis=1), alphas))
        upds = [_dot(vt, p_all) for _, vt, p_all, _ in staged]
        for (branch, _, _, alphas), upd in zip(staged, upds):
            for r, hs in enumerate(heads):
                acc_ref[branch, :, hs] = alphas[r] * acc_ref[branch, :, hs] + upd[:, hs]

    def scores(k_ref, off, rows, bias):
        return _dot_nt(k_ref[pl.ds(off, rows), :], q_all) * scale + bias

    def sel_mask(c, nblk):
        rix = lax.broadcasted_iota(jnp.int32, (nblk * QB, QB), 0)
        per_key_block = QB // SEL_BLOCK
        flags = [sel_ref[pl.ds(per_key_block * c + j, 1), :] for j in range(per_key_block * nblk)]
        m = flags[-1]
        for j in reversed(range(len(flags) - 1)):
            m = jnp.where(rix < (j + 1) * SEL_BLOCK, flags[j], m)
        return m > 0.5

    lower = row <= col
    bfar, boff, bdiag = bfar_ref[0], boff_ref[0], bdiag_ref[0]
    nwb = WINDOW // QB

    n_far = jnp.maximum(i - 1, 0)

    def far_body(p, carry):
        off = pl.multiple_of(p * 2 * QB, 2 * QB)
        attend([(0, scores(ksn_ref, off, 2 * QB, bfar), sel_mask(2 * p, 2), vst_ref[:, pl.ds(off, 2 * QB)])])
        return carry
    lax.fori_loop(0, n_far // 2, far_body, 0)

    @pl.when(n_far % 2 == 1)
    def _():
        off = pl.multiple_of((n_far - 1) * QB, QB)
        attend([(0, scores(ksn_ref, off, QB, bfar), sel_mask(n_far - 1, 1), vst_ref[:, pl.ds(off, QB)])])

    @pl.when(i >= nwb)
    def _():
        rows = (nwb - 1) * QB
        off = pl.multiple_of((i - nwb) * QB, QB)
        rix = lax.broadcasted_iota(jnp.int32, (rows, QB), 0)
        tcol = lax.broadcasted_iota(jnp.int32, (rows, QB), 1)
        attend([(1, scores(kwn_ref, off, rows, bfar), rix > tcol, vwt_ref[:, pl.ds(off, rows)])])

    for d in range(nwb - 1, 1, -1):
        @pl.when((i >= d) & (i < nwb))
        def _():
            off = pl.multiple_of((i - d) * QB, QB)
            attend([(1, scores(kwn_ref, off, QB, bfar), None, vwt_ref[:, pl.ds(off, QB)])])

    @pl.when(i >= 1)
    def _():
        off = pl.multiple_of((i - 1) * QB, QB)
        bias = jnp.concatenate([boff, bdiag], axis=0)
        after_first = lax.broadcasted_iota(jnp.int32, (2 * QB, QB), 0) - QB
        tcol = lax.broadcasted_iota(jnp.int32, (2 * QB, QB), 1)
        causal = after_first <= tcol
        s_sel = scores(ksn_ref, off, 2 * QB, bias)
        s_win = scores(kwn_ref, off, 2 * QB, bias)
        attend([(0, s_sel, sel_mask(i - 1, 2) & causal, vst_ref[:, pl.ds(off, 2 * QB)]),
                (1, s_win, causal, vwt_ref[:, pl.ds(off, 2 * QB)])])

    @pl.when(i == 0)
    def _():
        s_sel = scores(ksn_ref, 0, QB, bdiag)
        s_win = scores(kwn_ref, 0, QB, bdiag)
        attend([(0, s_sel, sel_mask(0, 1) & lower, vst_ref[:, 0:QB]),
                (1, s_win, lower, vwt_ref[:, 0:QB])])

    gate_t = jax.nn.sigmoid(gate_ref[...].T[:4 * R, :])
    for r, hs in enumerate(heads):
        g_cmp = gate_t[3 * r:3 * r + 1, :]
        g_sel = gate_t[3 * r + 1:3 * r + 2, :] * (1.0 / jnp.maximum(l_ref[r], TINY))
        g_win = gate_t[3 * r + 2:3 * r + 3, :] * (1.0 / jnp.maximum(l_ref[R + r], TINY))
        o_t = g_cmp * ocmp_ref[:, hs] + g_sel * acc_ref[0, :, hs] + g_win * acc_ref[1, :, hs]
        o_ref[:, hs] = o_t.T.astype(o_ref.dtype)


def nsa_attention(pa, pvt, kcmp, vcmp_t, pg, tables, q_gain, k_gain, bsz, seq):
    nqb = seq // Q_BLOCK
    G, R = NSA_KV_HEADS, NSA_GROUP
    W = R * Q_BLOCK
    bdiag, boff, bfar, bcmp, overlap_t = tables
    n_cmp_rows = seq // CMP_STRIDE

    def k_spec(seg):
        return pl.BlockSpec((seq, HEAD_DIM), lambda b, g, i, s=_A_BLK[seg]: (b, s + g))

    def vt_spec(seg):
        return pl.BlockSpec((HEAD_DIM, seq), lambda b, g, i, s=_T_BLK[seg]: (s + g, b))

    group_tile = pl.BlockSpec((1, Q_BLOCK, W), lambda b, g, i: (g, 0, 0))
    return pl.pallas_call(
        functools.partial(_nsa_kernel, seq=seq),
        grid=(bsz, G, nqb),
        in_specs=[pl.BlockSpec((Q_BLOCK, W), lambda b, g, i: (b * nqb + i, g)),
                  pl.BlockSpec((1, n_cmp_rows, HEAD_DIM), lambda b, g, i: (b, 0, g)),
                  pl.BlockSpec((1, HEAD_DIM, n_cmp_rows), lambda b, g, i: (b, g, 0)),
                  k_spec(_KS), vt_spec(_VS), k_spec(_KW), vt_spec(_VW),
                  pl.BlockSpec((Q_BLOCK, LANE), lambda b, g, i: (b * nqb + i, g)),
                  group_tile, group_tile,
                  pl.BlockSpec((1, 1, W), lambda b, g, i: (g, 0, 0)),
                  pl.BlockSpec((1, 1, n_cmp_rows, W), lambda b, g, i: (g, i, 0, 0)),
                  pl.BlockSpec((1, HEAD_DIM), lambda b, g, i: (0, 0)),
                  pl.BlockSpec((3, HEAD_DIM), lambda b, g, i: (0, 0)),
                  pl.BlockSpec((LANE, LANE), lambda b, g, i: (0, 0))],
        out_specs=pl.BlockSpec((Q_BLOCK, W), lambda b, g, i: (b * nqb + i, g)),
        out_shape=jax.ShapeDtypeStruct((bsz * seq, NSA_Q), _MXU),
        scratch_shapes=[pltpu.VMEM((seq, HEAD_DIM), _MXU),
                        pltpu.VMEM((seq, HEAD_DIM), _MXU),
                        pltpu.VMEM((2 * R, 1, Q_BLOCK), _F32),
                        pltpu.VMEM((2 * R, 1, Q_BLOCK), _F32),
                        pltpu.VMEM((2, HEAD_DIM, W), _F32),
                        pltpu.VMEM((HEAD_DIM, W), _F32),
                        pltpu.VMEM((seq // SEL_BLOCK, Q_BLOCK), _F32)],
        compiler_params=_params("parallel", "parallel", "arbitrary"),
        name="nsa_attention",
    )(pa, kcmp, vcmp_t, pa, pvt, pa, pvt, pg, bdiag, boff, bfar, bcmp,
      q_gain.reshape(1, HEAD_DIM), k_gain, overlap_t)


def _t5_bucket(dist):
    n = jnp.maximum(dist, 0)
    exact = REL_BUCKETS // 2
    big = exact + (jnp.log(jnp.maximum(n, 1).astype(jnp.float32) / exact)
                   / math.log(REL_MAX_DIST / exact) * (REL_BUCKETS - exact)).astype(jnp.int32)
    return jnp.where(n < exact, n, jnp.minimum(big, REL_BUCKETS - 1))


def nsa_tables(rel_bias, seq):
    G, R = NSA_KV_HEADS, NSA_GROUP
    nqb = seq // Q_BLOCK
    tab_h = rel_bias.T.astype(_F32)
    s = np.arange(Q_BLOCK)[:, None]
    t = np.arange(Q_BLOCK)[None, :]

    def group_tiles(x, lead):
        nl = len(lead)
        rows = x.shape[-2]
        x = x.reshape((G, R) + lead + (rows, Q_BLOCK))
        perm = (0,) + tuple(range(2, 2 + nl)) + (2 + nl, 1, 3 + nl)
        return x.transpose(perm).reshape((G,) + lead + (rows, R * Q_BLOCK))

    bdiag = group_tiles(jnp.take(tab_h, _t5_bucket(jnp.asarray(t - s)), axis=1), ())
    boff = group_tiles(jnp.take(tab_h, _t5_bucket(jnp.asarray(Q_BLOCK + t - s)), axis=1), ())
    bfar = group_tiles(jnp.take(tab_h, _t5_bucket(jnp.full((1, Q_BLOCK), 2 * Q_BLOCK)), axis=1), ())
    n = np.arange(seq // CMP_STRIDE)[None, :, None]
    ib = np.arange(nqb)[:, None, None]
    dist_c = ib * Q_BLOCK + t[None] - (CMP_STRIDE * n + CMP_BLOCK - 1)
    bcmp = group_tiles(jnp.take(tab_h, _t5_bucket(jnp.asarray(dist_c)), axis=1), (nqb,))
    n_cmp = (seq - CMP_BLOCK) // CMP_STRIDE + 1
    n_sel = seq // SEL_BLOCK
    c_start = np.arange(LANE) * CMP_STRIDE
    s_start = np.arange(LANE) * SEL_BLOCK
    overlap = ((c_start[:, None] < s_start[None, :] + SEL_BLOCK)
               & (c_start[:, None] + CMP_BLOCK > s_start[None, :])
               & (np.arange(LANE)[:, None] < n_cmp) & (np.arange(LANE)[None, :] < n_sel))
    return bdiag, boff, bfar, bcmp, jnp.asarray(overlap.T.astype(np.float32)).astype(_MXU)


def _sb_kernel(q_ref, k_ref, vt_ref, o_ref, acc_ref, rest_ref, *, heads):
    QB = Q_BLOCK
    i = pl.program_id(2)
    scale = HEAD_DIM ** -0.5
    row = lax.broadcasted_iota(jnp.int32, (QB, QB), 0)
    col = lax.broadcasted_iota(jnp.int32, (QB, QB), 1)
    later = jnp.where(col > row, 1.0, 0.0).astype(_MXU)
    strict = row < col

    def block(c, first):
        off = pl.multiple_of(c * QB, QB)
        hss = [slice(h * HEAD_DIM, (h + 1) * HEAD_DIM) for h in range(heads)]
        zs = [_dot_nt(k_ref[pl.ds(off, QB), hs], q_ref[:, hs]) * scale for hs in hss]
        log_bs, pieces, totals = [], [], []
        for z in zs:
            log_b, log_1mb = _log_sigmoid_pair(z)
            if first:
                log_1mb = jnp.where(strict, log_1mb, 0.0)
            log_bs.append(log_b)
            pieces.append(_split(log_1mb, 2))
            totals.append(jnp.sum(log_1mb, axis=0, keepdims=True))
        betweens = [_dot(later, p[0]) + _dot(later, p[1]) for p in pieces]
        probs = []
        for h in range(heads):
            if first:
                probs.append(jnp.where(strict, jnp.exp(log_bs[h] + betweens[h]), 0.0).astype(_MXU))
                rest_ref[h] = totals[h]
            else:
                rest = rest_ref[h]
                probs.append(jnp.exp(log_bs[h] + betweens[h] + rest).astype(_MXU))
                rest_ref[h] = rest + totals[h]
        upds = [_dot(vt_ref[hs, pl.ds(off, QB)], probs[h]) for h, hs in enumerate(hss)]
        for h in range(heads):
            acc_ref[h] = upds[h] if first else acc_ref[h] + upds[h]

    block(i, True)

    def body(n, carry):
        block(i - 1 - n, False)
        return carry
    lax.fori_loop(0, i, body, 0)
    for h in range(heads):
        o_ref[:, h * HEAD_DIM:(h + 1) * HEAD_DIM] = acc_ref[h].T.astype(o_ref.dtype)


def stick_breaking(pa, pvt, bsz, seq, heads=SB_HEADS):
    nqb = seq // Q_BLOCK
    w = heads * HEAD_DIM
    qb, kb, vb = _A_BLK[_Q_S] * LANE // w, _A_BLK[_K_S] * LANE // w, _T_BLK[_V_S] * LANE // w
    return pl.pallas_call(
        functools.partial(_sb_kernel, heads=heads),
        grid=(bsz, SB_HEADS // heads, nqb),
        in_specs=[pl.BlockSpec((Q_BLOCK, w), lambda b, h, i: (b * nqb + i, qb + h)),
                  pl.BlockSpec((seq, w), lambda b, h, i: (b, kb + h)),
                  pl.BlockSpec((w, seq), lambda b, h, i: (vb + h, b))],
        out_specs=pl.BlockSpec((Q_BLOCK, w), lambda b, h, i: (b * nqb + i, h)),
        out_shape=jax.ShapeDtypeStruct((bsz * seq, SB_W), _MXU),
        scratch_shapes=[pltpu.VMEM((heads, HEAD_DIM, Q_BLOCK), _F32),
                        pltpu.VMEM((heads, 1, Q_BLOCK), _F32)],
        compiler_params=_params("parallel", "parallel", "arbitrary"),
        name="stick_breaking",
    )(pa, pa, pvt)


def _hgrn_kernel(q_ref, f_ref, i_ref, g_ref, llb_ref, l1lb_ref, omlb_ref, ng_ref, o_ref, *, seq):
    C, SUB = HG_CHUNK, HG_SUB
    nsub = C // SUB
    row = lax.broadcasted_iota(jnp.int32, (C, C), 0)
    col = lax.broadcasted_iota(jnp.int32, (C, C), 1)
    upto = jnp.where(col <= row, 1.0, 0.0).astype(_MXU)
    sub_t = lax.broadcasted_iota(jnp.int32, (SUB, 1), 0)
    sub_c = lax.broadcasted_iota(jnp.int32, (SUB, C), 1)

    def chunk(n, state_t):
        off = pl.multiple_of(n * C, C)
        fp = f_ref[pl.ds(off, C), :]
        log_sig, _ = _log_sigmoid_pair(fp)
        y = l1lb_ref[...] + log_sig
        a = llb_ref[...]
        log_f = jnp.maximum(a, y) + jnp.log1p(jnp.exp(-jnp.abs(a - y)))
        b = functools.reduce(lambda u, w: u + w, [_dot(upto, h) for h in _split(log_f, 3)])
        q = q_ref[pl.ds(off, C), :].astype(_F32)
        k = omlb_ref[...] * jax.nn.sigmoid(-fp)
        v = i_ref[pl.ds(off, C), :].astype(_F32)
        v_m = v.astype(_MXU)

        o = _dot_nt((q * jnp.exp(b)).astype(_MXU), state_t.astype(_MXU))

        rows = [jnp.zeros((SUB, C), _F32)]
        for s_i in range(1, nsub):
            lo = s_i * SUB
            b_ref = b[lo - 1:lo, :]
            q_i = q[lo:lo + SUB, :] * jnp.exp(b[lo:lo + SUB, :] - b_ref)
            k_i = k * jnp.exp(jnp.minimum(b_ref - b, 0.0))
            a_i = _dot_nt(q_i.astype(_MXU), k_i.astype(_MXU))
            rows.append(jnp.where(sub_c < lo, a_i, 0.0))
        o = o + _dot(jnp.concatenate(rows, axis=0).astype(_MXU), v_m)

        diag = []
        for s_i in range(nsub):
            lo = s_i * SUB
            b_i = b[lo:lo + SUB, :]
            q_i = q[lo:lo + SUB, :]
            o_d = jnp.zeros((SUB, HG_DV), _F32)
            for s in range(SUB):
                w = jnp.exp(jnp.minimum(b_i - b[lo + s:lo + s + 1, :], 0.0))
                a_col = jnp.sum(q_i * k[lo + s:lo + s + 1, :] * w, axis=-1, keepdims=True)
                a_col = jnp.where(sub_t >= s, a_col, 0.0)
                o_d = o_d + a_col * v[lo + s:lo + s + 1, :]
            diag.append(o_d)
        o = o + jnp.concatenate(diag, axis=0)

        b_last = b[C - 1:C, :]
        k_dec = (k * jnp.exp(b_last - b)).astype(_MXU)
        state_t = state_t * jnp.exp(b_last) + _dot(v.T.astype(_MXU), k_dec)
        gate = jax.nn.sigmoid(g_ref[pl.ds(off, C), :])
        o_ref[pl.ds(off, C), :] = (_rms(o, ng_ref[...]) * gate).astype(o_ref.dtype)
        return state_t

    lax.fori_loop(0, seq // C, chunk, jnp.zeros((HG_DV, HG_DK), _F32))


def hgrn2(pa, pb, lb, norm_gain, bsz, seq):
    lb = lb.reshape(1, HG_KW).astype(_F32)
    log_lb = jnp.log(lb)
    log_1mlb = jnp.log1p(-lb)
    one_mlb = 1.0 - lb

    def seq_spec(blk):
        return pl.BlockSpec((seq, HG_DK), lambda b, h, s=blk: (b, s + h))

    head_vec = pl.BlockSpec((1, HG_DK), lambda b, h: (0, h))
    return pl.pallas_call(
        functools.partial(_hgrn_kernel, seq=seq),
        grid=(bsz, HG_HEADS),
        in_specs=[seq_spec(_A_BLK[_Q_H]), seq_spec(_B_BLK[_F_H]), seq_spec(_A_BLK[_I_H]), seq_spec(_B_BLK[_G_H]),
                  head_vec, head_vec, head_vec,
                  pl.BlockSpec((1, HG_DV), lambda b, h: (0, 0))],
        out_specs=pl.BlockSpec((seq, HG_DV), lambda b, h: (b, h)),
        out_shape=jax.ShapeDtypeStruct((bsz * seq, HG_VW), _MXU),
        compiler_params=_params("parallel", "parallel"),
        name="hgrn2",
    )(pa, pb, pa, pb, log_lb, log_1mlb, one_mlb, norm_gain.reshape(1, HG_DV))


def _merge_kernel(o0_ref, o1_ref, o2_ref, w0_ref, w1_ref, w2_ref, g0_ref, g1_ref, g2_ref, out_ref):
    acc = jax.nn.sigmoid(g0_ref[...]) * _dot(o0_ref[...], w0_ref[0])
    acc = acc + jax.nn.sigmoid(g1_ref[...]) * _dot(o1_ref[...], w1_ref[0])
    acc = acc + jax.nn.sigmoid(g2_ref[...]) * _dot(o2_ref[...], w2_ref[0])
    out_ref[...] = acc.astype(out_ref.dtype)


def merge_branches(o_nsa, o_sb, o_hg, w_branch, pb, tm=1024, tn=512):
    m = o_nsa.shape[0]
    nj = D_MODEL // tn
    gblk = _B_BLK[_GATES] * LANE // tn

    def o_spec():
        return pl.BlockSpec((tm, BRANCH_W), lambda i, j: (i, 0))

    def w_spec(n):
        return pl.BlockSpec((1, BRANCH_W, tn), lambda i, j, n=n: (n, 0, j))

    def g_spec(n):
        return pl.BlockSpec((tm, tn), lambda i, j, n=n: (i, gblk + n * nj + j))

    return pl.pallas_call(
        _merge_kernel,
        grid=(m // tm, nj),
        in_specs=[o_spec(), o_spec(), o_spec(), w_spec(0), w_spec(1), w_spec(2), g_spec(0), g_spec(1), g_spec(2)],
        out_specs=pl.BlockSpec((tm, tn), lambda i, j: (i, j)),
        out_shape=jax.ShapeDtypeStruct((m, D_MODEL), _MXU),
        compiler_params=_params("parallel", "arbitrary"),
        name="merge_branches",
    )(o_nsa, o_sb, o_hg, w_branch, w_branch, w_branch, pb, pb, pb)


def _ffn_up_kernel(h_ref, halo_ref, wg_ref, wv_ref, cwg_ref, cwv_ref, cbg_ref, cbv_ref, o_ref, *, tiles_per_seq):
    i = pl.program_id(0)
    first = (i % tiles_per_seq) == 0
    h = h_ref[...]
    halo = halo_ref[...]
    tm = h.shape[0]
    hrows = halo.shape[0]
    row = lax.broadcasted_iota(jnp.int32, (tm, 1), 0)

    def conv(w_ref, cw_ref, cb_ref):
        w = w_ref[...]
        u = _dot(h, w)
        uh = jnp.where(first, 0.0, _dot(halo, w))
        u1 = jnp.where(row == 0, uh[hrows - 1:hrows, :], pltpu.roll(u, 1, axis=0))
        u2 = jnp.where(row == 0, uh[hrows - 2:hrows - 1, :],
                       jnp.where(row == 1, uh[hrows - 1:hrows, :], pltpu.roll(u, 2, axis=0)))
        return cb_ref[...] + u2 * cw_ref[0:1, :] + u1 * cw_ref[1:2, :] + u * cw_ref[2:3, :]

    gate = conv(wg_ref, cwg_ref, cbg_ref)
    val = conv(wv_ref, cwv_ref, cbv_ref)
    o_ref[...] = (jax.nn.silu(gate) * val).astype(o_ref.dtype)


def ffn_up(h, w_up, conv_w, conv_b, seq, tm=1024, tn=512):
    m, k = h.shape
    nj = D_FF // tn
    hrows = 16
    conv_b = conv_b.reshape(1, 2 * D_FF)
    return pl.pallas_call(
        functools.partial(_ffn_up_kernel, tiles_per_seq=seq // tm),
        grid=(m // tm, nj),
        in_specs=[pl.BlockSpec((tm, k), lambda i, j: (i, 0)),
                  pl.BlockSpec((hrows, k), lambda i, j: (jnp.maximum(i * (tm // hrows) - 1, 0), 0)),
                  pl.BlockSpec((k, tn), lambda i, j: (0, j)),
                  pl.BlockSpec((k, tn), lambda i, j: (0, nj + j)),
                  pl.BlockSpec((CONV_W, tn), lambda i, j: (0, j)),
                  pl.BlockSpec((CONV_W, tn), lambda i, j: (0, nj + j)),
                  pl.BlockSpec((1, tn), lambda i, j: (0, j)),
                  pl.BlockSpec((1, tn), lambda i, j: (0, nj + j))],
        out_specs=pl.BlockSpec((tm, tn), lambda i, j: (i, j)),
        out_shape=jax.ShapeDtypeStruct((m, D_FF), _MXU),
        compiler_params=_params("parallel", "arbitrary"),
        name="ffn_up_conv",
    )(h, h, w_up, w_up, conv_w, conv_w, conv_b, conv_b)


def _gather_cols(w, segs):
    return jnp.concatenate([w[:, _SPLIT_OFF[s]:_SPLIT_OFF[s + 1]] for s in segs], axis=1)


def _nsa_gate_cols(w):
    g = w[:, _SPLIT_OFF[_G_N]:_SPLIT_OFF[_G_N + 1]].reshape(-1, NSA_KV_HEADS, NSA_GROUP * 3)
    g = jnp.pad(g, ((0, 0), (0, 0), (0, LANE - NSA_GROUP * 3)))
    return g.reshape(-1, NSA_KV_HEADS * LANE)


def kernel(x, norm_attn, w_in, nsa_q_gain, nsa_k_gain, cmp_pos, cmp_w1, cmp_w2, rel_bias, hg_lower_bound,
           hg_norm_gain, w_branch, w_out, norm_ffn, w_up, conv_w, conv_b, w_down):
    bsz, seq, d = x.shape
    m = bsz * seq
    lb_all = jnp.cumsum(jax.nn.softmax(hg_lower_bound.astype(_F32), axis=0), axis=0)
    lb_all = jnp.maximum(lb_all - lb_all[0:1], 0.0)
    tables = nsa_tables(rel_bias, seq)
    xf = x.reshape(m, d)
    for l in range(w_in.shape[0]):
        wl = w_in[l]
        h = rmsnorm_cast(xf, norm_attn[l])
        pa = matmul(h, _gather_cols(wl, _A_ORDER).astype(_MXU), _MXU, name="proj_a")
        pb = matmul(h, _gather_cols(wl, _B_ORDER).astype(_MXU), _F32, name="proj_b")
        pkc = matmul(h, _gather_cols(wl, (_KC,)).astype(_MXU), _MXU, name="proj_kc")
        pvc = matmul(h, _gather_cols(wl, (_VC,)).astype(_MXU), _MXU, name="proj_vc")
        pg = matmul(h, _nsa_gate_cols(wl).astype(_MXU), _F32, name="proj_g")
        blk_rows = seq // CMP_STRIDE
        kcmp = compress(pkc.reshape(bsz, blk_rows, CMP_STRIDE * NSA_KV), cmp_w1[l, 0], cmp_w2[l, 0], cmp_pos[l, 0],
                        nsa_k_gain[l, 0], True)
        vcmp_t = compress(pvc.reshape(bsz, blk_rows, CMP_STRIDE * NSA_KV), cmp_w1[l, 1], cmp_w2[l, 1], cmp_pos[l, 1],
                          nsa_k_gain[l, 0], False)
        pvt = matmul_nt(_gather_cols(wl, _T_ORDER).T.astype(_MXU), h, _MXU, name="proj_vt")
        o_nsa = nsa_attention(pa, pvt, kcmp, vcmp_t, pg, tables, nsa_q_gain[l], nsa_k_gain[l], bsz, seq)
        o_sb = stick_breaking(pa, pvt, bsz, seq)
        o_hg = hgrn2(pa, pb, lb_all[l], hg_norm_gain[l], bsz, seq)
        merged = merge_branches(o_nsa, o_sb, o_hg, w_branch[l].astype(_MXU), pb)
        xf = matmul(merged, w_out[l].astype(_MXU), _F32, residual=xf, name="out_proj")
        h2 = rmsnorm_cast(xf, norm_ffn[l])
        act = ffn_up(h2, w_up[l].astype(_MXU), conv_w[l], conv_b[l], seq)
        xf = matmul(act, w_down[l].astype(_MXU), _F32, residual=xf, tm=512, name="ffn_down")
    return xf.reshape(bsz, seq, d)
```

```python
import functools
import math

import numpy as np
import jax
import jax.numpy as jnp
from jax import lax
from jax.experimental import pallas as pl
from jax.experimental.pallas import tpu as pltpu

D_MODEL = 2048
DEPTH = 4
HEAD_DIM = 128
BRANCH_W = D_MODEL // 2
N_BRANCH = 3
NSA_HEADS = BRANCH_W // HEAD_DIM
NSA_KV_HEADS = NSA_HEADS // 4
NSA_GROUP = NSA_HEADS // NSA_KV_HEADS
CMP_BLOCK = 32
CMP_STRIDE = 16
SEL_BLOCK = 64
SEL_TOPK = 16
WINDOW = 512
Q_BLOCK = 128
SB_HEADS = BRANCH_W // HEAD_DIM
HG_DK = 128
HG_DV = 128
HG_HEADS = BRANCH_W // HG_DV
HG_CHUNK = 64
HG_SUB = 16
D_FF = 256 * ((8 * D_MODEL // 3 + 255) // 256)
CONV_W = 3
REL_BUCKETS = 32
REL_MAX_DIST = 128
EPS = 1e-6
NEG = -1e30
TINY = 1e-30

NSA_Q = NSA_HEADS * HEAD_DIM
NSA_KV = NSA_KV_HEADS * HEAD_DIM
NSA_GATE = 3 * NSA_HEADS
SB_W = SB_HEADS * HEAD_DIM
HG_KW = HG_HEADS * HG_DK
HG_VW = HG_HEADS * HG_DV
SPLIT_SIZES = (NSA_Q,) + (NSA_KV,) * 6 + (NSA_GATE,) + (SB_W,) * 3 + (HG_KW, HG_KW, HG_VW, HG_VW, N_BRANCH * D_MODEL)
(_Q_N, _KC, _VC, _KS, _VS, _KW, _VW, _G_N, _Q_S, _K_S, _V_S, _Q_H, _F_H, _I_H, _G_H, _GATES) = range(16)
_SPLIT_OFF = np.concatenate([[0], np.cumsum(SPLIT_SIZES)]).tolist()

LANE = 128
_A_ORDER = (_Q_N, _Q_S, _K_S, _Q_H, _I_H, _KS, _KW)
_A_BLK = {}
_off = 0
for _s in _A_ORDER:
    _A_BLK[_s] = _off // LANE
    _off += SPLIT_SIZES[_s]
A_COLS = _off
_T_ORDER = (_V_S, _VS, _VW)
_T_BLK = {}
_off = 0
for _s in _T_ORDER:
    _T_BLK[_s] = _off // LANE
    _off += SPLIT_SIZES[_s]
T_ROWS = _off
_B_ORDER = (_F_H, _G_H, _GATES)
_B_BLK = {}
_off = 0
for _s in _B_ORDER:
    _B_BLK[_s] = _off // LANE
    _off += SPLIT_SIZES[_s]
B_COLS = _off

VMEM_LIMIT = 48 * 1024 * 1024

_MXU = jnp.bfloat16
_F32 = jnp.float32


def _dot(a, b):
    return jnp.dot(a, b, preferred_element_type=_F32)


def _dot_nt(a, b):
    return lax.dot_general(a, b, (((1,), (1,)), ((), ())), preferred_element_type=_F32)


def _split(x, parts):
    out = []
    r = x
    for _ in range(parts):
        h = r.astype(_MXU)
        out.append(h)
        r = r - h.astype(_F32)
    return out


def _params(*sem):
    return pltpu.CompilerParams(dimension_semantics=sem, vmem_limit_bytes=VMEM_LIMIT)


def _rms(x, gain):
    return x * lax.rsqrt(jnp.mean(x * x, axis=-1, keepdims=True) + EPS) * gain


def _log_sigmoid_pair(z):
    lp = jnp.log(1.0 + jnp.exp(-jnp.abs(z)))
    return jnp.minimum(z, 0.0) - lp, jnp.minimum(-z, 0.0) - lp


def _rmsnorm_kernel(x_ref, g_ref, o_ref):
    o_ref[...] = _rms(x_ref[...], g_ref[...]).astype(o_ref.dtype)


def rmsnorm_cast(x2d, gain):
    m, d = x2d.shape
    tm = 512
    return pl.pallas_call(
        _rmsnorm_kernel,
        grid=(m // tm,),
        in_specs=[pl.BlockSpec((tm, d), lambda i: (i, 0)),
                  pl.BlockSpec((1, d), lambda i: (0, 0))],
        out_specs=pl.BlockSpec((tm, d), lambda i: (i, 0)),
        out_shape=jax.ShapeDtypeStruct((m, d), _MXU),
        compiler_params=_params("parallel"),
        name="rmsnorm_cast",
    )(x2d, gain.reshape(1, d))


def _mm_kernel(a_ref, w_ref, o_ref):
    o_ref[...] = _dot(a_ref[...], w_ref[...]).astype(o_ref.dtype)


def _mm_res_kernel(a_ref, w_ref, r_ref, o_ref):
    o_ref[...] = (r_ref[...] + _dot(a_ref[...], w_ref[...])).astype(o_ref.dtype)


def matmul(a, w, out_dtype, residual=None, tm=1024, tn=512, name="matmul"):
    m, k = a.shape
    n = w.shape[1]
    tm = min(tm, m)
    tn = min(tn, n)
    in_specs = [pl.BlockSpec((tm, k), lambda i, j: (i, 0)),
                pl.BlockSpec((k, tn), lambda i, j: (0, j))]
    args = [a, w]
    kern = _mm_kernel
    if residual is not None:
        in_specs.append(pl.BlockSpec((tm, tn), lambda i, j: (i, j)))
        args.append(residual)
        kern = _mm_res_kernel
    return pl.pallas_call(
        kern,
        grid=(m // tm, n // tn),
        in_specs=in_specs,
        out_specs=pl.BlockSpec((tm, tn), lambda i, j: (i, j)),
        out_shape=jax.ShapeDtypeStruct((m, n), out_dtype),
        compiler_params=_params("parallel", "arbitrary"),
        name=name,
    )(*args)


def _mm_nt_kernel(wt_ref, a_ref, o_ref):
    o_ref[...] = _dot_nt(wt_ref[...], a_ref[...]).astype(o_ref.dtype)


def matmul_nt(wt, a, out_dtype, tm=1024, tn=512, name="matmul_nt"):
    n, k = wt.shape
    m = a.shape[0]
    tm = min(tm, m)
    tn = min(tn, n)
    return pl.pallas_call(
        _mm_nt_kernel,
        grid=(m // tm, n // tn),
        in_specs=[pl.BlockSpec((tn, k), lambda i, j: (j, 0)),
                  pl.BlockSpec((tm, k), lambda i, j: (i, 0))],
        out_specs=pl.BlockSpec((tn, tm), lambda i, j: (j, i)),
        out_shape=jax.ShapeDtypeStruct((n, m), out_dtype),
        compiler_params=_params("parallel", "arbitrary"),
        name=name,
    )(wt, a)


def _compress_kernel(x_ref, wa_ref, wb_ref, pa_ref, pb_ref, w2_ref, gain_ref, o_ref, *, normalize):
    x = x_ref[0].astype(_F32)
    u = _dot((x + pa_ref[...]).astype(_MXU), wa_ref[...])
    v = _dot((x + pb_ref[...]).astype(_MXU), wb_ref[...])
    hid = jax.nn.gelu(u + pltpu.roll(v, v.shape[0] - 1, axis=0)).astype(_MXU)
    if normalize:
        out = _dot(hid, w2_ref[...])
        out = jnp.concatenate(
            [_rms(out[:, g * HEAD_DIM:(g + 1) * HEAD_DIM], gain_ref[...]) for g in range(NSA_KV_HEADS)], axis=1)
    else:
        out = _dot_nt(w2_ref[...], hid)
    o_ref[0] = out.astype(o_ref.dtype)


def compress(x, w1, w2, pos, gain, normalize):
    bsz = x.shape[0]
    nrow = x.shape[1]
    kdim = x.shape[2]
    half = CMP_BLOCK // 2
    eye = jnp.eye(NSA_KV_HEADS, dtype=w1.dtype)

    def embed(w):
        return jnp.einsum('lde,gh->lgdhe', w, eye).reshape(kdim, NSA_KV).astype(_MXU)

    def tile_pos(p):
        return jnp.broadcast_to(p[:, None, :], (half, NSA_KV_HEADS, HEAD_DIM)).reshape(1, kdim)

    w2b = jnp.einsum('de,gh->gdhe', w2, eye).reshape(NSA_KV, NSA_KV).astype(_MXU)
    out_block = (1, nrow, NSA_KV)
    if not normalize:
        w2b = w2b.T
        out_block = (1, NSA_KV, nrow)
    return pl.pallas_call(
        functools.partial(_compress_kernel, normalize=normalize),
        grid=(bsz,),
        in_specs=[pl.BlockSpec((1, nrow, kdim), lambda b: (b, 0, 0)),
                  pl.BlockSpec((kdim, NSA_KV), lambda b: (0, 0)),
                  pl.BlockSpec((kdim, NSA_KV), lambda b: (0, 0)),
                  pl.BlockSpec((1, kdim), lambda b: (0, 0)),
                  pl.BlockSpec((1, kdim), lambda b: (0, 0)),
                  pl.BlockSpec((NSA_KV, NSA_KV), lambda b: (0, 0)),
                  pl.BlockSpec((1, HEAD_DIM), lambda b: (0, 0))],
        out_specs=pl.BlockSpec(out_block, lambda b: (b, 0, 0)),
        out_shape=jax.ShapeDtypeStruct((bsz,) + out_block[1:], _MXU),
        compiler_params=_params("parallel"),
        name="nsa_compress",
    )(x, embed(w1[:half]), embed(w1[half:]), tile_pos(pos[:half]), tile_pos(pos[half:]), w2b,
      gain.reshape(1, HEAD_DIM))


def _nsa_kernel(q_ref, kc_ref, vct_ref, ks_ref, vst_ref, kw_ref, vwt_ref, gate_ref,
                bdiag_ref, boff_ref, bfar_ref, bcmp_ref, qg_ref, kg_ref, ovt_ref,
                o_ref, ksn_ref, kwn_ref, m_ref, l_ref, acc_ref, ocmp_ref, sel_ref, *, seq):
    R = NSA_GROUP
    QB = Q_BLOCK
    W = R * QB
    i = pl.program_id(2)
    scale = HEAD_DIM ** -0.5
    sel_shift = int(math.log2(SEL_BLOCK))

    @pl.when(i == 0)
    def _():
        def body(c, carry):
            off = pl.multiple_of(c * QB, QB)
            ksn_ref[pl.ds(off, QB), :] = _rms(ks_ref[pl.ds(off, QB), :].astype(_F32), kg_ref[1:2, :]).astype(_MXU)
            kwn_ref[pl.ds(off, QB), :] = _rms(kw_ref[pl.ds(off, QB), :].astype(_F32), kg_ref[2:3, :]).astype(_MXU)
            return carry
        lax.fori_loop(0, seq // QB, body, 0)

    heads = [slice(r * QB, (r + 1) * QB) for r in range(R)]
    q_all = jnp.concatenate(
        [_rms(q_ref[:, hs].astype(_F32), qg_ref[...]).astype(_MXU) for hs in heads], axis=0)

    row = lax.broadcasted_iota(jnp.int32, (QB, QB), 0)
    col = lax.broadcasted_iota(jnp.int32, (QB, QB), 1)

    valid_c = i * QB + col >= CMP_STRIDE * row + (CMP_BLOCK - 1)
    shift = QB // CMP_STRIDE
    bias_c = bcmp_ref[0, pl.ds(pl.multiple_of((seq // QB - 1 - i) * shift, shift), seq // CMP_STRIDE), :]
    s_c = _dot_nt(kc_ref[0], q_all) * scale + bias_c
    p_parts = []
    psum = jnp.zeros((QB, QB), _F32)
    for hs in heads:
        s = jnp.where(valid_c, s_c[:, hs], NEG)
        e = jnp.where(valid_c, jnp.exp(s - jnp.max(s, axis=0, keepdims=True)), 0.0)
        p = e * (1.0 / jnp.maximum(jnp.sum(e, axis=0, keepdims=True), TINY))
        psum = psum + p
        p_parts.append(p.astype(_MXU))
    ocmp_ref[...] = _dot(vct_ref[0], jnp.concatenate(p_parts, axis=1))
    ovt = ovt_ref[...]
    imp = functools.reduce(lambda a, b: a + b, [_dot(ovt, h) for h in _split(psum, 2)])

    n_sel = seq // SEL_BLOCK
    jrow = lax.broadcasted_iota(jnp.int32, (n_sel, QB), 0)
    tcol = lax.broadcasted_iota(jnp.int32, (n_sel, QB), 1)
    qblk = lax.shift_right_arithmetic(i * QB + tcol, sel_shift)
    causal_b = jrow <= qblk
    forced = causal_b & ((jrow == 0) | (jrow >= qblk - 1))
    score = jnp.where(forced, jnp.inf, jnp.where(causal_b, imp[:n_sel, :], -jnp.inf))
    rank = jnp.zeros((n_sel, QB), jnp.int32)
    for jj in range(n_sel):
        sc = score[jj:jj + 1, :]
        beats = (sc > score) | ((sc == score) & (jrow > jj))
        rank = rank + beats.astype(jnp.int32)
    sel_ref[...] = jnp.where((rank < SEL_TOPK) & causal_b, 1.0, 0.0)

    m_ref[...] = jnp.full(m_ref.shape, NEG, _F32)
    l_ref[...] = jnp.zeros(l_ref.shape, _F32)
    acc_ref[...] = jnp.zeros(acc_ref.shape, _F32)

    def attend(jobs):
        staged = []
        for branch, s_all, mask, vt in jobs:
            p_parts, alphas = [], []
            for r, hs in enumerate(heads):
                slot = branch * R + r
                s = s_all[:, hs]
                if mask is not None:
                    s = jnp.where(mask, s, NEG)
                m_old = m_ref[slot]
                m_new = jnp.maximum(m_old, jnp.max(s, axis=0, keepdims=True))
                alpha = jnp.exp(m_old - m_new)
                p = jnp.exp(s - m_new)
                if mask is not None:
                    p = jnp.where(mask, p, 0.0)
                l_ref[slot] = alpha * l_ref[slot] + jnp.sum(p, axis=0, keepdims=True)
                m_ref[slot] = m_new
                p_parts.append(p.astype(_MXU))
                alphas.append(alpha)
            staged.append((branch, vt, jnp.concatenate(p_parts, axis=1), alphas))
        upds = [_dot(vt, p_all) for _, vt, p_all, _ in staged]
        for (branch, _, _, alphas), upd in zip(staged, upds):
            for r, hs in enumerate(heads):
                acc_ref[branch, :, hs] = alphas[r] * acc_ref[branch, :, hs] + upd[:, hs]

    def scores(k_ref, off, rows, bias):
        return _dot_nt(k_ref[pl.ds(off, rows), :], q_all) * scale + bias

    def sel_mask(c, nblk):
        rix = lax.broadcasted_iota(jnp.int32, (nblk * QB, QB), 0)
        per_key_block = QB // SEL_BLOCK
        flags = [sel_ref[pl.ds(per_key_block * c + j, 1), :] for j in range(per_key_block * nblk)]
        m = flags[-1]
        for j in reversed(range(len(flags) - 1)):
            m = jnp.where(rix < (j + 1) * SEL_BLOCK, flags[j], m)
        return m > 0.5

    lower = row <= col
    bfar, boff, bdiag = bfar_ref[0], boff_ref[0], bdiag_ref[0]
    nwb = WINDOW // QB

    n_far = jnp.maximum(i - 1, 0)

    def far_body(p, carry):
        off = pl.multiple_of(p * 2 * QB, 2 * QB)
        attend([(0, scores(ksn_ref, off, 2 * QB, bfar), sel_mask(2 * p, 2), vst_ref[:, pl.ds(off, 2 * QB)])])
        return carry
    lax.fori_loop(0, n_far // 2, far_body, 0)

    @pl.when(n_far % 2 == 1)
    def _():
        off = pl.multiple_of((n_far - 1) * QB, QB)
        attend([(0, scores(ksn_ref, off, QB, bfar), sel_mask(n_far - 1, 1), vst_ref[:, pl.ds(off, QB)])])

    @pl.when(i >= nwb)
    def _():
        rows = (nwb - 1) * QB
        off = pl.multiple_of((i - nwb) * QB, QB)
        rix = lax.broadcasted_iota(jnp.int32, (rows, QB), 0)
        tcol = lax.broadcasted_iota(jnp.int32, (rows, QB), 1)
        attend([(1, scores(kwn_ref, off, rows, bfar), rix > tcol, vwt_ref[:, pl.ds(off, rows)])])

    for d in range(nwb - 1, 1, -1):
        @pl.when((i >= d) & (i < nwb))
        def _():
            off = pl.multiple_of((i - d) * QB, QB)
            attend([(1, scores(kwn_ref, off, QB, bfar), None, vwt_ref[:, pl.ds(off, QB)])])

    @pl.when(i >= 1)
    def _():
        off = pl.multiple_of((i - 1) * QB, QB)
        bias = jnp.concatenate([boff, bdiag], axis=0)
        after_first = lax.broadcasted_iota(jnp.int32, (2 * QB, QB), 0) - QB
        tcol = lax.broadcasted_iota(jnp.int32, (2 * QB, QB), 1)
        causal = after_first <= tcol
        s_sel = scores(ksn_ref, off, 2 * QB, bias)
        s_win = scores(kwn_ref, off, 2 * QB, bias)
        attend([(0, s_sel, sel_mask(i - 1, 2) & causal, vst_ref[:, pl.ds(off, 2 * QB)]),
                (1, s_win, causal, vwt_ref[:, pl.ds(off, 2 * QB)])])

    @pl.when(i == 0)
    def _():
        s_sel = scores(ksn_ref, 0, QB, bdiag)
        s_win = scores(kwn_ref, 0, QB, bdiag)
        attend([(0, s_sel, sel_mask(0, 1) & lower, vst_ref[:, 0:QB]),
                (1, s_win, lower, vwt_ref[:, 0:QB])])

    gate_t = jax.nn.sigmoid(gate_ref[...].T[:4 * R, :])
    for r, hs in enumerate(heads):
        g_cmp = gate_t[3 * r:3 * r + 1, :]
        g_sel = gate_t[3 * r + 1:3 * r + 2, :] * (1.0 / jnp.maximum(l_ref[r], TINY))
        g_win = gate_t[3 * r + 2:3 * r + 3, :] * (1.0 / jnp.maximum(l_ref[R + r], TINY))
        o_t = g_cmp * ocmp_ref[:, hs] + g_sel * acc_ref[0, :, hs] + g_win * acc_ref[1, :, hs]
        o_ref[:, hs] = o_t.T.astype(o_ref.dtype)


def nsa_attention(pa, pvt, kcmp, vcmp_t, pg, tables, q_gain, k_gain, bsz, seq):
    nqb = seq // Q_BLOCK
    G, R = NSA_KV_HEADS, NSA_GROUP
    W = R * Q_BLOCK
    bdiag, boff, bfar, bcmp, overlap_t = tables
    n_cmp_rows = seq // CMP_STRIDE

    def k_spec(seg):
        return pl.BlockSpec((seq, HEAD_DIM), lambda b, g, i, s=_A_BLK[seg]: (b, s + g))

    def vt_spec(seg):
        return pl.BlockSpec((HEAD_DIM, seq), lambda b, g, i, s=_T_BLK[seg]: (s + g, b))

    group_tile = pl.BlockSpec((1, Q_BLOCK, W), lambda b, g, i: (g, 0, 0))
    return pl.pallas_call(
        functools.partial(_nsa_kernel, seq=seq),
        grid=(bsz, G, nqb),
        in_specs=[pl.BlockSpec((Q_BLOCK, W), lambda b, g, i: (b * nqb + i, g)),
                  pl.BlockSpec((1, n_cmp_rows, HEAD_DIM), lambda b, g, i: (b, 0, g)),
                  pl.BlockSpec((1, HEAD_DIM, n_cmp_rows), lambda b, g, i: (b, g, 0)),
                  k_spec(_KS), vt_spec(_VS), k_spec(_KW), vt_spec(_VW),
                  pl.BlockSpec((Q_BLOCK, LANE), lambda b, g, i: (b * nqb + i, g)),
                  group_tile, group_tile,
                  pl.BlockSpec((1, 1, W), lambda b, g, i: (g, 0, 0)),
                  pl.BlockSpec((1,) + bcmp.shape[1:], lambda b, g, i: (g, 0, 0)),
                  pl.BlockSpec((1, HEAD_DIM), lambda b, g, i: (0, 0)),
                  pl.BlockSpec((3, HEAD_DIM), lambda b, g, i: (0, 0)),
                  pl.BlockSpec((LANE, LANE), lambda b, g, i: (0, 0))],
        out_specs=pl.BlockSpec((Q_BLOCK, W), lambda b, g, i: (b * nqb + i, g)),
        out_shape=jax.ShapeDtypeStruct((bsz * seq, NSA_Q), _MXU),
        scratch_shapes=[pltpu.VMEM((seq, HEAD_DIM), _MXU),
                        pltpu.VMEM((seq, HEAD_DIM), _MXU),
                        pltpu.VMEM((2 * R, 1, Q_BLOCK), _F32),
                        pltpu.VMEM((2 * R, 1, Q_BLOCK), _F32),
                        pltpu.VMEM((2, HEAD_DIM, W), _F32),
                        pltpu.VMEM((HEAD_DIM, W), _F32),
                        pltpu.VMEM((seq // SEL_BLOCK, Q_BLOCK), _F32)],
        compiler_params=_params("parallel", "parallel", "arbitrary"),
        name="nsa_attention",
    )(pa, kcmp, vcmp_t, pa, pvt, pa, pvt, pg, bdiag, boff, bfar, bcmp,
      q_gain.reshape(1, HEAD_DIM), k_gain, overlap_t)


def _t5_bucket(dist):
    n = jnp.maximum(dist, 0)
    exact = REL_BUCKETS // 2
    big = exact + (jnp.log(jnp.maximum(n, 1).astype(jnp.float32) / exact)
                   / math.log(REL_MAX_DIST / exact) * (REL_BUCKETS - exact)).astype(jnp.int32)
    return jnp.where(n < exact, n, jnp.minimum(big, REL_BUCKETS - 1))


def nsa_tables(rel_bias, seq):
    G, R = NSA_KV_HEADS, NSA_GROUP
    nqb = seq // Q_BLOCK
    tab_h = rel_bias.T.astype(_F32)
    s = np.arange(Q_BLOCK)[:, None]
    t = np.arange(Q_BLOCK)[None, :]

    def group_tiles(x, lead):
        nl = len(lead)
        rows = x.shape[-2]
        x = x.reshape((G, R) + lead + (rows, Q_BLOCK))
        perm = (0,) + tuple(range(2, 2 + nl)) + (2 + nl, 1, 3 + nl)
        return x.transpose(perm).reshape((G,) + lead + (rows, R * Q_BLOCK))

    bdiag = group_tiles(jnp.take(tab_h, _t5_bucket(jnp.asarray(t - s)), axis=1), ())
    boff = group_tiles(jnp.take(tab_h, _t5_bucket(jnp.asarray(Q_BLOCK + t - s)), axis=1), ())
    bfar = group_tiles(jnp.take(tab_h, _t5_bucket(jnp.full((1, Q_BLOCK), 2 * Q_BLOCK)), axis=1), ())
    lead = (nqb - 1) * Q_BLOCK // CMP_STRIDE
    rows_c = -(-(seq // CMP_STRIDE + lead) // 8) * 8
    m = np.arange(rows_c)[:, None]
    dist_c = t - (CMP_STRIDE * (m - lead) + CMP_BLOCK - 1)
    bcmp = group_tiles(jnp.take(tab_h, _t5_bucket(jnp.asarray(dist_c)), axis=1), ())
    n_cmp = (seq - CMP_BLOCK) // CMP_STRIDE + 1
    n_sel = seq // SEL_BLOCK
    c_start = np.arange(LANE) * CMP_STRIDE
    s_start = np.arange(LANE) * SEL_BLOCK
    overlap = ((c_start[:, None] < s_start[None, :] + SEL_BLOCK)
               & (c_start[:, None] + CMP_BLOCK > s_start[None, :])
               & (np.arange(LANE)[:, None] < n_cmp) & (np.arange(LANE)[None, :] < n_sel))
    return bdiag, boff, bfar, bcmp, jnp.asarray(overlap.T.astype(np.float32)).astype(_MXU)


def _sb_kernel(q_ref, k_ref, vt_ref, o_ref, acc_ref, rest_ref, *, heads):
    QB = Q_BLOCK
    i = pl.program_id(2)
    scale = HEAD_DIM ** -0.5
    row = lax.broadcasted_iota(jnp.int32, (QB, QB), 0)
    col = lax.broadcasted_iota(jnp.int32, (QB, QB), 1)
    later = jnp.where(col > row, 1.0, 0.0).astype(_MXU)
    strict = row < col

    def block(c, first):
        off = pl.multiple_of(c * QB, QB)
        hss = [slice(h * HEAD_DIM, (h + 1) * HEAD_DIM) for h in range(heads)]
        zs = [_dot_nt(k_ref[pl.ds(off, QB), hs], q_ref[:, hs]) * scale for hs in hss]
        log_bs, pieces, totals = [], [], []
        for z in zs:
            log_b, log_1mb = _log_sigmoid_pair(z)
            if first:
                log_1mb = jnp.where(strict, log_1mb, 0.0)
            log_bs.append(log_b)
            pieces.append(_split(log_1mb, 2))
            totals.append(jnp.sum(log_1mb, axis=0, keepdims=True))
        betweens = [_dot(later, p[0]) + _dot(later, p[1]) for p in pieces]
        probs = []
        for h in range(heads):
            if first:
                probs.append(jnp.where(strict, jnp.exp(log_bs[h] + betweens[h]), 0.0).astype(_MXU))
                rest_ref[h] = totals[h]
            else:
                rest = rest_ref[h]
                probs.append(jnp.exp(log_bs[h] + betweens[h] + rest).astype(_MXU))
                rest_ref[h] = rest + totals[h]
        upds = [_dot(vt_ref[hs, pl.ds(off, QB)], probs[h]) for h, hs in enumerate(hss)]
        for h in range(heads):
            acc_ref[h] = upds[h] if first else acc_ref[h] + upds[h]

    block(i, True)

    def body(n, carry):
        block(i - 1 - n, False)
        return carry
    lax.fori_loop(0, i, body, 0)
    for h in range(heads):
        o_ref[:, h * HEAD_DIM:(h + 1) * HEAD_DIM] = acc_ref[h].T.astype(o_ref.dtype)


def stick_breaking(pa, pvt, bsz, seq, heads=SB_HEADS):
    nqb = seq // Q_BLOCK
    w = heads * HEAD_DIM
    qb, kb, vb = _A_BLK[_Q_S] * LANE // w, _A_BLK[_K_S] * LANE // w, _T_BLK[_V_S] * LANE // w
    return pl.pallas_call(
        functools.partial(_sb_kernel, heads=heads),
        grid=(bsz, SB_HEADS // heads, nqb),
        in_specs=[pl.BlockSpec((Q_BLOCK, w), lambda b, h, i: (b * nqb + i, qb + h)),
                  pl.BlockSpec((seq, w), lambda b, h, i: (b, kb + h)),
                  pl.BlockSpec((w, seq), lambda b, h, i: (vb + h, b))],
        out_specs=pl.BlockSpec((Q_BLOCK, w), lambda b, h, i: (b * nqb + i, h)),
        out_shape=jax.ShapeDtypeStruct((bsz * seq, SB_W), _MXU),
        scratch_shapes=[pltpu.VMEM((heads, HEAD_DIM, Q_BLOCK), _F32),
                        pltpu.VMEM((heads, 1, Q_BLOCK), _F32)],
        compiler_params=_params("parallel", "parallel", "arbitrary"),
        name="stick_breaking",
    )(pa, pa, pvt)


def _hgrn_kernel(q_ref, f_ref, i_ref, g_ref, llb_ref, l1lb_ref, omlb_ref, ng_ref, o_ref, state_ref, *, seq, heads):
    C, SUB = HG_CHUNK, HG_SUB
    nsub = C // SUB
    hss = [slice(h * HG_DK, (h + 1) * HG_DK) for h in range(heads)]
    row = lax.broadcasted_iota(jnp.int32, (C, C), 0)
    col = lax.broadcasted_iota(jnp.int32, (C, C), 1)
    upto = jnp.where(col <= row, 1.0, 0.0).astype(_MXU)
    sub_t = lax.broadcasted_iota(jnp.int32, (SUB, 1), 0)
    sub_c = lax.broadcasted_iota(jnp.int32, (SUB, C), 1)
    state_ref[...] = jnp.zeros(state_ref.shape, _F32)

    def chunk(n, carry):
        off = pl.multiple_of(n * C, C)
        fps = [f_ref[pl.ds(off, C), hs] for hs in hss]
        pieces = []
        for fp, hs in zip(fps, hss):
            log_sig, _ = _log_sigmoid_pair(fp)
            y = l1lb_ref[:, hs] + log_sig
            a = llb_ref[:, hs]
            log_f = jnp.maximum(a, y) + jnp.log1p(jnp.exp(-jnp.abs(a - y)))
            pieces.append(_split(log_f, 3))
        bs = [_dot(upto, p[0]) + _dot(upto, p[1]) + _dot(upto, p[2]) for p in pieces]
        qs = [q_ref[pl.ds(off, C), hs].astype(_F32) for hs in hss]
        ks = [omlb_ref[:, hs] * jax.nn.sigmoid(-fp) for fp, hs in zip(fps, hss)]
        vs = [i_ref[pl.ds(off, C), hs].astype(_F32) for hs in hss]

        outs = [_dot_nt((q * jnp.exp(b)).astype(_MXU), state_ref[h].astype(_MXU))
                for h, (q, b) in enumerate(zip(qs, bs))]

        lhs, rhs = [], []
        for q, k, b in zip(qs, ks, bs):
            for s_i in range(1, nsub):
                lo = s_i * SUB
                b_ref = b[lo - 1:lo, :]
                lhs.append((q[lo:lo + SUB, :] * jnp.exp(b[lo:lo + SUB, :] - b_ref)).astype(_MXU))
                rhs.append((k * jnp.exp(jnp.minimum(b_ref - b, 0.0))).astype(_MXU))
        a_blocks = [_dot_nt(x, y) for x, y in zip(lhs, rhs)]
        intra = []
        for h in range(heads):
            rows = [jnp.zeros((SUB, C), _F32)]
            for s_i in range(1, nsub):
                rows.append(jnp.where(sub_c < s_i * SUB, a_blocks[h * (nsub - 1) + s_i - 1], 0.0))
            intra.append(jnp.concatenate(rows, axis=0).astype(_MXU))
        outs = [o + _dot(a, v.astype(_MXU)) for o, a, v in zip(outs, intra, vs)]

        for h, (q, k, v, b) in enumerate(zip(qs, ks, vs, bs)):
            diag = []
            for s_i in range(nsub):
                lo = s_i * SUB
                b_i = b[lo:lo + SUB, :]
                q_i = q[lo:lo + SUB, :]
                o_d = jnp.zeros((SUB, HG_DV), _F32)
                for s in range(SUB):
                    w = jnp.exp(b_i - b[lo + s:lo + s + 1, :])
                    a_col = jnp.sum(q_i * k[lo + s:lo + s + 1, :] * w, axis=-1, keepdims=True)
                    a_col = jnp.where(sub_t >= s, a_col, 0.0)
                    o_d = o_d + a_col * v[lo + s:lo + s + 1, :]
                diag.append(o_d)
            outs[h] = outs[h] + jnp.concatenate(diag, axis=0)

        k_decs = [(k * jnp.exp(b[C - 1:C, :] - b)).astype(_MXU) for k, b in zip(ks, bs)]
        grown = [_dot(v.T.astype(_MXU), kd) for v, kd in zip(vs, k_decs)]
        for h, (b, hs) in enumerate(zip(bs, hss)):
            state_ref[h] = state_ref[h] * jnp.exp(b[C - 1:C, :]) + grown[h]
            gate = jax.nn.sigmoid(g_ref[pl.ds(off, C), hs])
            o_ref[pl.ds(off, C), hs] = (_rms(outs[h], ng_ref[...]) * gate).astype(o_ref.dtype)
        return carry

    lax.fori_loop(0, seq // C, chunk, 0)


def hgrn2(pa, pb, lb, norm_gain, bsz, seq, heads=4):
    lb = lb.reshape(1, HG_KW).astype(_F32)
    log_lb = jnp.log(lb)
    log_1mlb = jnp.log1p(-lb)
    one_mlb = 1.0 - lb
    w = heads * HG_DK

    def seq_spec(blk):
        return pl.BlockSpec((seq, w), lambda b, h, s=blk * LANE // w: (b, s + h))

    head_vec = pl.BlockSpec((1, w), lambda b, h: (0, h))
    return pl.pallas_call(
        functools.partial(_hgrn_kernel, seq=seq, heads=heads),
        grid=(bsz, HG_HEADS // heads),
        in_specs=[seq_spec(_A_BLK[_Q_H]), seq_spec(_B_BLK[_F_H]), seq_spec(_A_BLK[_I_H]), seq_spec(_B_BLK[_G_H]),
                  head_vec, head_vec, head_vec,
                  pl.BlockSpec((1, HG_DV), lambda b, h: (0, 0))],
        out_specs=pl.BlockSpec((seq, w), lambda b, h: (b, h)),
        out_shape=jax.ShapeDtypeStruct((bsz * seq, HG_VW), _MXU),
        scratch_shapes=[pltpu.VMEM((heads, HG_DV, HG_DK), _F32)],
        compiler_params=_params("parallel", "parallel"),
        name="hgrn2",
    )(pa, pb, pa, pb, log_lb, log_1mlb, one_mlb, norm_gain.reshape(1, HG_DV))


def _merge_kernel(o0_ref, o1_ref, o2_ref, w0_ref, w1_ref, w2_ref, g0_ref, g1_ref, g2_ref, out_ref):
    acc = jax.nn.sigmoid(g0_ref[...]) * _dot(o0_ref[...], w0_ref[0])
    acc = acc + jax.nn.sigmoid(g1_ref[...]) * _dot(o1_ref[...], w1_ref[0])
    acc = acc + jax.nn.sigmoid(g2_ref[...]) * _dot(o2_ref[...], w2_ref[0])
    out_ref[...] = acc.astype(out_ref.dtype)


def merge_branches(o_nsa, o_sb, o_hg, w_branch, pb, tm=1024, tn=512):
    m = o_nsa.shape[0]
    nj = D_MODEL // tn
    gblk = _B_BLK[_GATES] * LANE // tn

    def o_spec():
        return pl.BlockSpec((tm, BRANCH_W), lambda i, j: (i, 0))

    def w_spec(n):
        return pl.BlockSpec((1, BRANCH_W, tn), lambda i, j, n=n: (n, 0, j))

    def g_spec(n):
        return pl.BlockSpec((tm, tn), lambda i, j, n=n: (i, gblk + n * nj + j))

    return pl.pallas_call(
        _merge_kernel,
        grid=(m // tm, nj),
        in_specs=[o_spec(), o_spec(), o_spec(), w_spec(0), w_spec(1), w_spec(2), g_spec(0), g_spec(1), g_spec(2)],
        out_specs=pl.BlockSpec((tm, tn), lambda i, j: (i, j)),
        out_shape=jax.ShapeDtypeStruct((m, D_MODEL), _MXU),
        compiler_params=_params("parallel", "arbitrary"),
        name="merge_branches",
    )(o_nsa, o_sb, o_hg, w_branch, w_branch, w_branch, pb, pb, pb)


def _ffn_up_kernel(h_ref, halo_ref, wg_ref, wv_ref, cwg_ref, cwv_ref, cbg_ref, cbv_ref, o_ref, *, tiles_per_seq):
    i = pl.program_id(0)
    first = (i % tiles_per_seq) == 0
    h = h_ref[...]
    halo = halo_ref[...]
    tm = h.shape[0]
    hrows = halo.shape[0]
    row = lax.broadcasted_iota(jnp.int32, (tm, 1), 0)

    def conv(w_ref, cw_ref, cb_ref):
        w = w_ref[...]
        u = _dot(h, w)
        uh = jnp.where(first, 0.0, _dot(halo, w))
        u1 = jnp.where(row == 0, uh[hrows - 1:hrows, :], pltpu.roll(u, 1, axis=0))
        u2 = jnp.where(row == 0, uh[hrows - 2:hrows - 1, :],
                       jnp.where(row == 1, uh[hrows - 1:hrows, :], pltpu.roll(u, 2, axis=0)))
        return cb_ref[...] + u2 * cw_ref[0:1, :] + u1 * cw_ref[1:2, :] + u * cw_ref[2:3, :]

    gate = conv(wg_ref, cwg_ref, cbg_ref)
    val = conv(wv_ref, cwv_ref, cbv_ref)
    o_ref[...] = (jax.nn.silu(gate) * val).astype(o_ref.dtype)


def ffn_up(h, w_up, conv_w, conv_b, seq, tm=1024, tn=512):
    m, k = h.shape
    nj = D_FF // tn
    hrows = 16
    conv_b = conv_b.reshape(1, 2 * D_FF)
    return pl.pallas_call(
        functools.partial(_ffn_up_kernel, tiles_per_seq=seq // tm),
        grid=(m // tm, nj),
        in_specs=[pl.BlockSpec((tm, k), lambda i, j: (i, 0)),
                  pl.BlockSpec((hrows, k), lambda i, j: (jnp.maximum(i * (tm // hrows) - 1, 0), 0)),
                  pl.BlockSpec((k, tn), lambda i, j: (0, j)),
                  pl.BlockSpec((k, tn), lambda i, j: (0, nj + j)),
                  pl.BlockSpec((CONV_W, tn), lambda i, j: (0, j)),
                  pl.BlockSpec((CONV_W, tn), lambda i, j: (0, nj + j)),
                  pl.BlockSpec((1, tn), lambda i, j: (0, j)),
                  pl.BlockSpec((1, tn), lambda i, j: (0, nj + j))],
        out_specs=pl.BlockSpec((tm, tn), lambda i, j: (i, j)),
        out_shape=jax.ShapeDtypeStruct((m, D_FF), _MXU),
        compiler_params=_params("parallel", "arbitrary"),
        name="ffn_up_conv",
    )(h, h, w_up, w_up, conv_w, conv_w, conv_b, conv_b)


def _gather_cols(w, segs):
    return jnp.concatenate([w[:, _SPLIT_OFF[s]:_SPLIT_OFF[s + 1]] for s in segs], axis=1)


def _nsa_gate_cols(w):
    g = w[:, _SPLIT_OFF[_G_N]:_SPLIT_OFF[_G_N + 1]].reshape(-1, NSA_KV_HEADS, NSA_GROUP * 3)
    g = jnp.pad(g, ((0, 0), (0, 0), (0, LANE - NSA_GROUP * 3)))
    return g.reshape(-1, NSA_KV_HEADS * LANE)


def kernel(x, norm_attn, w_in, nsa_q_gain, nsa_k_gain, cmp_pos, cmp_w1, cmp_w2, rel_bias, hg_lower_bound,
           hg_norm_gain, w_branch, w_out, norm_ffn, w_up, conv_w, conv_b, w_down):
    bsz, seq, d = x.shape
    m = bsz * seq
    lb_all = jnp.cumsum(jax.nn.softmax(hg_lower_bound.astype(_F32), axis=0), axis=0)
    lb_all = jnp.maximum(lb_all - lb_all[0:1], 0.0)
    tables = nsa_tables(rel_bias, seq)
    xf = x.reshape(m, d)
    for l in range(w_in.shape[0]):
        wl = w_in[l]
        h = rmsnorm_cast(xf, norm_attn[l])
        pa = matmul(h, _gather_cols(wl, _A_ORDER).astype(_MXU), _MXU, name="proj_a")
        pb = matmul(h, _gather_cols(wl, _B_ORDER).astype(_MXU), _F32, tn=1024, name="proj_b")
        pkc = matmul(h, _gather_cols(wl, (_KC,)).astype(_MXU), _MXU, name="proj_kc")
        pvc = matmul(h, _gather_cols(wl, (_VC,)).astype(_MXU), _MXU, name="proj_vc")
        pg = matmul(h, _nsa_gate_cols(wl).astype(_MXU), _F32, name="proj_g")
        blk_rows = seq // CMP_STRIDE
        kcmp = compress(pkc.reshape(bsz, blk_rows, CMP_STRIDE * NSA_KV), cmp_w1[l, 0], cmp_w2[l, 0], cmp_pos[l, 0],
                        nsa_k_gain[l, 0], True)
        vcmp_t = compress(pvc.reshape(bsz, blk_rows, CMP_STRIDE * NSA_KV), cmp_w1[l, 1], cmp_w2[l, 1], cmp_pos[l, 1],
                          nsa_k_gain[l, 0], False)
        pvt = matmul_nt(_gather_cols(wl, _T_ORDER).T.astype(_MXU), h, _MXU, name="proj_vt")
        o_nsa = nsa_attention(pa, pvt, kcmp, vcmp_t, pg, tables, nsa_q_gain[l], nsa_k_gain[l], bsz, seq)
        o_sb = stick_breaking(pa, pvt, bsz, seq)
        o_hg = hgrn2(pa, pb, lb_all[l], hg_norm_gain[l], bsz, seq)
        merged = merge_branches(o_nsa, o_sb, o_hg, w_branch[l].astype(_MXU), pb)
        xf = matmul(merged, w_out[l].astype(_MXU), _F32, residual=xf, tn=1024, name="out_proj")
        h2 = rmsnorm_cast(xf, norm_ffn[l])
        act = ffn_up(h2, w_up[l].astype(_MXU), conv_w[l], conv_b[l], seq)
        xf = matmul(act, w_down[l].astype(_MXU), _F32, residual=xf, tm=512, name="ffn_down")
    return xf.reshape(bsz, seq, d)
```

```python
import functools
import math

import numpy as np
import jax
import jax.numpy as jnp
from jax import lax
from jax.experimental import pallas as pl
from jax.experimental.pallas import tpu as pltpu

D_MODEL = 2048
DEPTH = 4
HEAD_DIM = 128
BRANCH_W = D_MODEL // 2
N_BRANCH = 3
NSA_HEADS = BRANCH_W // HEAD_DIM
NSA_KV_HEADS = NSA_HEADS // 4
NSA_GROUP = NSA_HEADS // NSA_KV_HEADS
CMP_BLOCK = 32
CMP_STRIDE = 16
SEL_BLOCK = 64
SEL_TOPK = 16
WINDOW = 512
Q_BLOCK = 128
SB_HEADS = BRANCH_W // HEAD_DIM
HG_DK = 128
HG_DV = 128
HG_HEADS = BRANCH_W // HG_DV
HG_CHUNK = 64
HG_SUB = 8
D_FF = 256 * ((8 * D_MODEL // 3 + 255) // 256)
CONV_W = 3
REL_BUCKETS = 32
REL_MAX_DIST = 128
EPS = 1e-6
NEG = -1e30
TINY = 1e-30

NSA_Q = NSA_HEADS * HEAD_DIM
NSA_KV = NSA_KV_HEADS * HEAD_DIM
NSA_GATE = 3 * NSA_HEADS
SB_W = SB_HEADS * HEAD_DIM
HG_KW = HG_HEADS * HG_DK
HG_VW = HG_HEADS * HG_DV
SPLIT_SIZES = (NSA_Q,) + (NSA_KV,) * 6 + (NSA_GATE,) + (SB_W,) * 3 + (HG_KW, HG_KW, HG_VW, HG_VW, N_BRANCH * D_MODEL)
(_Q_N, _KC, _VC, _KS, _VS, _KW, _VW, _G_N, _Q_S, _K_S, _V_S, _Q_H, _F_H, _I_H, _G_H, _GATES) = range(16)
_SPLIT_OFF = np.concatenate([[0], np.cumsum(SPLIT_SIZES)]).tolist()

LANE = 128
_A_ORDER = (_Q_N, _Q_S, _K_S, _Q_H, _I_H, _KS, _KW)
_A_BLK = {}
_off = 0
for _s in _A_ORDER:
    _A_BLK[_s] = _off // LANE
    _off += SPLIT_SIZES[_s]
A_COLS = _off
_T_ORDER = (_V_S, _VS, _VW)
_T_BLK = {}
_off = 0
for _s in _T_ORDER:
    _T_BLK[_s] = _off // LANE
    _off += SPLIT_SIZES[_s]
T_ROWS = _off
_B_ORDER = (_F_H, _G_H, _GATES)
_B_BLK = {}
_off = 0
for _s in _B_ORDER:
    _B_BLK[_s] = _off // LANE
    _off += SPLIT_SIZES[_s]
B_COLS = _off

VMEM_LIMIT = 48 * 1024 * 1024

_MXU = jnp.bfloat16
_F32 = jnp.float32


def _dot(a, b):
    return jnp.dot(a, b, preferred_element_type=_F32)


def _dot_nt(a, b):
    return lax.dot_general(a, b, (((1,), (1,)), ((), ())), preferred_element_type=_F32)


def _split(x, parts):
    out = []
    r = x
    for _ in range(parts):
        h = r.astype(_MXU)
        out.append(h)
        r = r - h.astype(_F32)
    return out


def _params(*sem):
    return pltpu.CompilerParams(dimension_semantics=sem, vmem_limit_bytes=VMEM_LIMIT)


def _rms(x, gain):
    return x * lax.rsqrt(jnp.mean(x * x, axis=-1, keepdims=True) + EPS) * gain


def _log_sigmoid_pair(z):
    lp = jnp.log(1.0 + jnp.exp(-jnp.abs(z)))
    return jnp.minimum(z, 0.0) - lp, jnp.minimum(-z, 0.0) - lp


def _rmsnorm_kernel(x_ref, g_ref, o_ref):
    o_ref[...] = _rms(x_ref[...], g_ref[...]).astype(o_ref.dtype)


def rmsnorm_cast(x2d, gain):
    m, d = x2d.shape
    tm = 512
    return pl.pallas_call(
        _rmsnorm_kernel,
        grid=(m // tm,),
        in_specs=[pl.BlockSpec((tm, d), lambda i: (i, 0)),
                  pl.BlockSpec((1, d), lambda i: (0, 0))],
        out_specs=pl.BlockSpec((tm, d), lambda i: (i, 0)),
        out_shape=jax.ShapeDtypeStruct((m, d), _MXU),
        compiler_params=_params("parallel"),
        name="rmsnorm_cast",
    )(x2d, gain.reshape(1, d))


def _mm_kernel(a_ref, w_ref, o_ref):
    o_ref[...] = _dot(a_ref[...], w_ref[...]).astype(o_ref.dtype)


def _mm_res_kernel(a_ref, w_ref, r_ref, o_ref):
    o_ref[...] = (r_ref[...] + _dot(a_ref[...], w_ref[...])).astype(o_ref.dtype)


def matmul(a, w, out_dtype, residual=None, tm=1024, tn=512, name="matmul"):
    m, k = a.shape
    n = w.shape[1]
    tm = min(tm, m)
    tn = min(tn, n)
    in_specs = [pl.BlockSpec((tm, k), lambda i, j: (i, 0)),
                pl.BlockSpec((k, tn), lambda i, j: (0, j))]
    args = [a, w]
    kern = _mm_kernel
    if residual is not None:
        in_specs.append(pl.BlockSpec((tm, tn), lambda i, j: (i, j)))
        args.append(residual)
        kern = _mm_res_kernel
    return pl.pallas_call(
        kern,
        grid=(m // tm, n // tn),
        in_specs=in_specs,
        out_specs=pl.BlockSpec((tm, tn), lambda i, j: (i, j)),
        out_shape=jax.ShapeDtypeStruct((m, n), out_dtype),
        compiler_params=_params("parallel", "arbitrary"),
        name=name,
    )(*args)


def _mm_nt_kernel(wt_ref, a_ref, o_ref):
    o_ref[...] = _dot_nt(wt_ref[...], a_ref[...]).astype(o_ref.dtype)


def matmul_nt(wt, a, out_dtype, tm=1024, tn=512, name="matmul_nt"):
    n, k = wt.shape
    m = a.shape[0]
    tm = min(tm, m)
    tn = min(tn, n)
    return pl.pallas_call(
        _mm_nt_kernel,
        grid=(m // tm, n // tn),
        in_specs=[pl.BlockSpec((tn, k), lambda i, j: (j, 0)),
                  pl.BlockSpec((tm, k), lambda i, j: (i, 0))],
        out_specs=pl.BlockSpec((tn, tm), lambda i, j: (j, i)),
        out_shape=jax.ShapeDtypeStruct((n, m), out_dtype),
        compiler_params=_params("parallel", "arbitrary"),
        name=name,
    )(wt, a)


def _compress_kernel(x_ref, wa_ref, wb_ref, pa_ref, pb_ref, w2_ref, gain_ref, o_ref, *, normalize):
    x = x_ref[0].astype(_F32)
    u = _dot((x + pa_ref[...]).astype(_MXU), wa_ref[...])
    v = _dot((x + pb_ref[...]).astype(_MXU), wb_ref[...])
    hid = jax.nn.gelu(u + pltpu.roll(v, v.shape[0] - 1, axis=0)).astype(_MXU)
    if normalize:
        out = _dot(hid, w2_ref[...])
        out = jnp.concatenate(
            [_rms(out[:, g * HEAD_DIM:(g + 1) * HEAD_DIM], gain_ref[...]) for g in range(NSA_KV_HEADS)], axis=1)
    else:
        out = _dot_nt(w2_ref[...], hid)
    o_ref[0] = out.astype(o_ref.dtype)


def compress(x, w1, w2, pos, gain, normalize):
    bsz = x.shape[0]
    nrow = x.shape[1]
    kdim = x.shape[2]
    half = CMP_BLOCK // 2
    eye = jnp.eye(NSA_KV_HEADS, dtype=w1.dtype)

    def embed(w):
        return jnp.einsum('lde,gh->lgdhe', w, eye).reshape(kdim, NSA_KV).astype(_MXU)

    def tile_pos(p):
        return jnp.broadcast_to(p[:, None, :], (half, NSA_KV_HEADS, HEAD_DIM)).reshape(1, kdim)

    w2b = jnp.einsum('de,gh->gdhe', w2, eye).reshape(NSA_KV, NSA_KV).astype(_MXU)
    out_block = (1, nrow, NSA_KV)
    if not normalize:
        w2b = w2b.T
        out_block = (1, NSA_KV, nrow)
    return pl.pallas_call(
        functools.partial(_compress_kernel, normalize=normalize),
        grid=(bsz,),
        in_specs=[pl.BlockSpec((1, nrow, kdim), lambda b: (b, 0, 0)),
                  pl.BlockSpec((kdim, NSA_KV), lambda b: (0, 0)),
                  pl.BlockSpec((kdim, NSA_KV), lambda b: (0, 0)),
                  pl.BlockSpec((1, kdim), lambda b: (0, 0)),
                  pl.BlockSpec((1, kdim), lambda b: (0, 0)),
                  pl.BlockSpec((NSA_KV, NSA_KV), lambda b: (0, 0)),
                  pl.BlockSpec((1, HEAD_DIM), lambda b: (0, 0))],
        out_specs=pl.BlockSpec(out_block, lambda b: (b, 0, 0)),
        out_shape=jax.ShapeDtypeStruct((bsz,) + out_block[1:], _MXU),
        compiler_params=_params("parallel"),
        name="nsa_compress",
    )(x, embed(w1[:half]), embed(w1[half:]), tile_pos(pos[:half]), tile_pos(pos[half:]), w2b,
      gain.reshape(1, HEAD_DIM))


def _nsa_kernel(q_ref, kc_ref, vct_ref, ks_ref, vst_ref, kw_ref, vwt_ref, gate_ref,
                bdiag_ref, boff_ref, bfar_ref, bcmp_ref, qg_ref, kg_ref, ovt_ref,
                o_ref, ksn_ref, kwn_ref, m_ref, l_ref, acc_ref, ocmp_ref, sel_ref, *, seq):
    R = NSA_GROUP
    QB = Q_BLOCK
    W = R * QB
    i = pl.program_id(2)
    scale = HEAD_DIM ** -0.5
    sel_shift = int(math.log2(SEL_BLOCK))

    @pl.when(i == 0)
    def _():
        def body(c, carry):
            off = pl.multiple_of(c * QB, QB)
            ksn_ref[pl.ds(off, QB), :] = _rms(ks_ref[pl.ds(off, QB), :].astype(_F32), kg_ref[1:2, :]).astype(_MXU)
            kwn_ref[pl.ds(off, QB), :] = _rms(kw_ref[pl.ds(off, QB), :].astype(_F32), kg_ref[2:3, :]).astype(_MXU)
            return carry
        lax.fori_loop(0, seq // QB, body, 0)

    heads = [slice(r * QB, (r + 1) * QB) for r in range(R)]
    q_all = jnp.concatenate(
        [_rms(q_ref[:, hs].astype(_F32), qg_ref[...]).astype(_MXU) for hs in heads], axis=0)

    row = lax.broadcasted_iota(jnp.int32, (QB, QB), 0)
    col = lax.broadcasted_iota(jnp.int32, (QB, QB), 1)

    m_ref[...] = jnp.full(m_ref.shape, NEG, _F32)
    l_ref[...] = jnp.zeros(l_ref.shape, _F32)
    acc_ref[...] = jnp.zeros(acc_ref.shape, _F32)

    def attend_pre(jobs):
        staged = []
        for branch, s_all, mask, vt in jobs:
            p_parts, alphas = [], []
            for r, hs in enumerate(heads):
                slot = branch * R + r
                s = s_all[:, hs]
                if mask is not None:
                    s = jnp.where(mask, s, NEG)
                m_old = m_ref[slot]
                m_new = jnp.maximum(m_old, jnp.max(s, axis=0, keepdims=True))
                alpha = jnp.exp(m_old - m_new)
                p = jnp.exp(s - m_new)
                if mask is not None:
                    p = jnp.where(mask, p, 0.0)
                l_ref[slot] = alpha * l_ref[slot] + jnp.sum(p, axis=0, keepdims=True)
                m_ref[slot] = m_new
                p_parts.append(p.astype(_MXU))
                alphas.append(alpha)
            staged.append((branch, vt, jnp.concatenate(p_parts, axis=1), alphas))
        return staged

    def attend_post(staged):
        upds = [_dot(vt, p_all) for _, vt, p_all, _ in staged]
        for (branch, _, _, alphas), upd in zip(staged, upds):
            for r, hs in enumerate(heads):
                acc_ref[branch, :, hs] = alphas[r] * acc_ref[branch, :, hs] + upd[:, hs]

    def attend(jobs):
        attend_post(attend_pre(jobs))

    valid_c = i * QB + col >= CMP_STRIDE * row + (CMP_BLOCK - 1)
    shift = QB // CMP_STRIDE
    bias_c = bcmp_ref[0, pl.ds(pl.multiple_of((seq // QB - 1 - i) * shift, shift), seq // CMP_STRIDE), :]
    s_c = _dot_nt(kc_ref[0], q_all) * scale + bias_c

    nwb = WINDOW // QB
    rows_w = (nwb - 1) * QB
    start_w = jnp.maximum(i - nwb, 0)
    off_w = pl.multiple_of(start_w * QB, QB)
    key_w = start_w * QB + lax.broadcasted_iota(jnp.int32, (rows_w, QB), 0)
    qry_w = i * QB + lax.broadcasted_iota(jnp.int32, (rows_w, QB), 1)
    mask_w = (key_w > qry_w - WINDOW) & (key_w < (i - 1) * QB)
    s_w = _dot_nt(kwn_ref[pl.ds(off_w, rows_w), :], q_all) * scale + bfar_ref[0]

    p_parts = []
    psum = jnp.zeros((QB, QB), _F32)
    for hs in heads:
        s = jnp.where(valid_c, s_c[:, hs], NEG)
        e = jnp.where(valid_c, jnp.exp(s - jnp.max(s, axis=0, keepdims=True)), 0.0)
        p = e * (1.0 / jnp.maximum(jnp.sum(e, axis=0, keepdims=True), TINY))
        psum = psum + p
        p_parts.append(p.astype(_MXU))
    staged_w = attend_pre([(1, s_w, mask_w, vwt_ref[:, pl.ds(off_w, rows_w)])])
    ocmp_ref[...] = _dot(vct_ref[0], jnp.concatenate(p_parts, axis=1))
    ovt = ovt_ref[...]
    imp = functools.reduce(lambda a, b: a + b, [_dot(ovt, h) for h in _split(psum, 2)])
    attend_post(staged_w)

    n_sel = seq // SEL_BLOCK
    jrow = lax.broadcasted_iota(jnp.int32, (n_sel, QB), 0)
    tcol = lax.broadcasted_iota(jnp.int32, (n_sel, QB), 1)
    qblk = lax.shift_right_arithmetic(i * QB + tcol, sel_shift)
    causal_b = jrow <= qblk
    forced = causal_b & ((jrow == 0) | (jrow >= qblk - 1))
    score = jnp.where(forced, jnp.inf, jnp.where(causal_b, imp[:n_sel, :], -jnp.inf))
    rank = jnp.zeros((n_sel, QB), jnp.int32)
    for jj in range(n_sel):
        sc = score[jj:jj + 1, :]
        beats = (sc > score) | ((sc == score) & (jrow > jj))
        rank = rank + beats.astype(jnp.int32)
    sel_ref[...] = jnp.where((rank < SEL_TOPK) & causal_b, 1.0, 0.0)

    def scores(k_ref, off, rows, bias):
        return _dot_nt(k_ref[pl.ds(off, rows), :], q_all) * scale + bias

    def sel_mask(c, nblk):
        rix = lax.broadcasted_iota(jnp.int32, (nblk * QB, QB), 0)
        per_key_block = QB // SEL_BLOCK
        flags = [sel_ref[pl.ds(per_key_block * c + j, 1), :] for j in range(per_key_block * nblk)]
        m = flags[-1]
        for j in reversed(range(len(flags) - 1)):
            m = jnp.where(rix < (j + 1) * SEL_BLOCK, flags[j], m)
        return m > 0.5

    lower = row <= col
    bfar, boff, bdiag = bfar_ref[0], boff_ref[0], bdiag_ref[0]

    n_far = jnp.maximum(i - 1, 0)

    def far_body(p, carry):
        off = pl.multiple_of(p * 2 * QB, 2 * QB)
        attend([(0, scores(ksn_ref, off, 2 * QB, bfar), sel_mask(2 * p, 2), vst_ref[:, pl.ds(off, 2 * QB)])])
        return carry
    lax.fori_loop(0, n_far // 2, far_body, 0)

    @pl.when(n_far % 2 == 1)
    def _():
        off = pl.multiple_of((n_far - 1) * QB, QB)
        attend([(0, scores(ksn_ref, off, QB, bfar), sel_mask(n_far - 1, 1), vst_ref[:, pl.ds(off, QB)])])

    @pl.when(i >= 1)
    def _():
        off = pl.multiple_of((i - 1) * QB, QB)
        bias = jnp.concatenate([boff, bdiag], axis=0)
        after_first = lax.broadcasted_iota(jnp.int32, (2 * QB, QB), 0) - QB
        tcol = lax.broadcasted_iota(jnp.int32, (2 * QB, QB), 1)
        causal = after_first <= tcol
        s_sel = scores(ksn_ref, off, 2 * QB, bias)
        s_win = scores(kwn_ref, off, 2 * QB, bias)
        attend([(0, s_sel, sel_mask(i - 1, 2) & causal, vst_ref[:, pl.ds(off, 2 * QB)]),
                (1, s_win, causal, vwt_ref[:, pl.ds(off, 2 * QB)])])

    @pl.when(i == 0)
    def _():
        s_sel = scores(ksn_ref, 0, QB, bdiag)
        s_win = scores(kwn_ref, 0, QB, bdiag)
        attend([(0, s_sel, sel_mask(0, 1) & lower, vst_ref[:, 0:QB]),
                (1, s_win, lower, vwt_ref[:, 0:QB])])

    gate_t = jax.nn.sigmoid(gate_ref[...].T[:4 * R, :])
    for r, hs in enumerate(heads):
        g_cmp = gate_t[3 * r:3 * r + 1, :]
        g_sel = gate_t[3 * r + 1:3 * r + 2, :] * (1.0 / jnp.maximum(l_ref[r], TINY))
        g_win = gate_t[3 * r + 2:3 * r + 3, :] * (1.0 / jnp.maximum(l_ref[R + r], TINY))
        o_t = g_cmp * ocmp_ref[:, hs] + g_sel * acc_ref[0, :, hs] + g_win * acc_ref[1, :, hs]
        o_ref[:, hs] = o_t.T.astype(o_ref.dtype)


def nsa_attention(pa, pvt, kcmp, vcmp_t, pg, tables, q_gain, k_gain, bsz, seq):
    nqb = seq // Q_BLOCK
    G, R = NSA_KV_HEADS, NSA_GROUP
    W = R * Q_BLOCK
    bdiag, boff, bfar, bcmp, overlap_t = tables
    n_cmp_rows = seq // CMP_STRIDE

    def k_spec(seg):
        return pl.BlockSpec((seq, HEAD_DIM), lambda b, g, i, s=_A_BLK[seg]: (b, s + g))

    def vt_spec(seg):
        return pl.BlockSpec((HEAD_DIM, seq), lambda b, g, i, s=_T_BLK[seg]: (s + g, b))

    group_tile = pl.BlockSpec((1, Q_BLOCK, W), lambda b, g, i: (g, 0, 0))
    return pl.pallas_call(
        functools.partial(_nsa_kernel, seq=seq),
        grid=(bsz, G, nqb),
        in_specs=[pl.BlockSpec((Q_BLOCK, W), lambda b, g, i: (b * nqb + i, g)),
                  pl.BlockSpec((1, n_cmp_rows, HEAD_DIM), lambda b, g, i: (b, 0, g)),
                  pl.BlockSpec((1, HEAD_DIM, n_cmp_rows), lambda b, g, i: (b, g, 0)),
                  k_spec(_KS), vt_spec(_VS), k_spec(_KW), vt_spec(_VW),
                  pl.BlockSpec((Q_BLOCK, LANE), lambda b, g, i: (b * nqb + i, g)),
                  group_tile, group_tile,
                  pl.BlockSpec((1, 1, W), lambda b, g, i: (g, 0, 0)),
                  pl.BlockSpec((1,) + bcmp.shape[1:], lambda b, g, i: (g, 0, 0)),
                  pl.BlockSpec((1, HEAD_DIM), lambda b, g, i: (0, 0)),
                  pl.BlockSpec((3, HEAD_DIM), lambda b, g, i: (0, 0)),
                  pl.BlockSpec((LANE, LANE), lambda b, g, i: (0, 0))],
        out_specs=pl.BlockSpec((Q_BLOCK, W), lambda b, g, i: (b * nqb + i, g)),
        out_shape=jax.ShapeDtypeStruct((bsz * seq, NSA_Q), _MXU),
        scratch_shapes=[pltpu.VMEM((seq, HEAD_DIM), _MXU),
                        pltpu.VMEM((seq, HEAD_DIM), _MXU),
                        pltpu.VMEM((2 * R, 1, Q_BLOCK), _F32),
                        pltpu.VMEM((2 * R, 1, Q_BLOCK), _F32),
                        pltpu.VMEM((2, HEAD_DIM, W), _F32),
                        pltpu.VMEM((HEAD_DIM, W), _F32),
                        pltpu.VMEM((seq // SEL_BLOCK, Q_BLOCK), _F32)],
        compiler_params=_params("parallel", "parallel", "arbitrary"),
        name="nsa_attention",
    )(pa, kcmp, vcmp_t, pa, pvt, pa, pvt, pg, bdiag, boff, bfar, bcmp,
      q_gain.reshape(1, HEAD_DIM), k_gain, overlap_t)


def _t5_bucket(dist):
    n = jnp.maximum(dist, 0)
    exact = REL_BUCKETS // 2
    big = exact + (jnp.log(jnp.maximum(n, 1).astype(jnp.float32) / exact)
                   / math.log(REL_MAX_DIST / exact) * (REL_BUCKETS - exact)).astype(jnp.int32)
    return jnp.where(n < exact, n, jnp.minimum(big, REL_BUCKETS - 1))


def nsa_tables(rel_bias, seq):
    G, R = NSA_KV_HEADS, NSA_GROUP
    nqb = seq // Q_BLOCK
    tab_h = rel_bias.T.astype(_F32)
    s = np.arange(Q_BLOCK)[:, None]
    t = np.arange(Q_BLOCK)[None, :]

    def group_tiles(x, lead):
        nl = len(lead)
        rows = x.shape[-2]
        x = x.reshape((G, R) + lead + (rows, Q_BLOCK))
        perm = (0,) + tuple(range(2, 2 + nl)) + (2 + nl, 1, 3 + nl)
        return x.transpose(perm).reshape((G,) + lead + (rows, R * Q_BLOCK))

    bdiag = group_tiles(jnp.take(tab_h, _t5_bucket(jnp.asarray(t - s)), axis=1), ())
    boff = group_tiles(jnp.take(tab_h, _t5_bucket(jnp.asarray(Q_BLOCK + t - s)), axis=1), ())
    bfar = group_tiles(jnp.take(tab_h, _t5_bucket(jnp.full((1, Q_BLOCK), 2 * Q_BLOCK)), axis=1), ())
    lead = (nqb - 1) * Q_BLOCK // CMP_STRIDE
    rows_c = -(-(seq // CMP_STRIDE + lead) // 8) * 8
    m = np.arange(rows_c)[:, None]
    dist_c = t - (CMP_STRIDE * (m - lead) + CMP_BLOCK - 1)
    bcmp = group_tiles(jnp.take(tab_h, _t5_bucket(jnp.asarray(dist_c)), axis=1), ())
    n_cmp = (seq - CMP_BLOCK) // CMP_STRIDE + 1
    n_sel = seq // SEL_BLOCK
    c_start = np.arange(LANE) * CMP_STRIDE
    s_start = np.arange(LANE) * SEL_BLOCK
    overlap = ((c_start[:, None] < s_start[None, :] + SEL_BLOCK)
               & (c_start[:, None] + CMP_BLOCK > s_start[None, :])
               & (np.arange(LANE)[:, None] < n_cmp) & (np.arange(LANE)[None, :] < n_sel))
    return bdiag, boff, bfar, bcmp, jnp.asarray(overlap.T.astype(np.float32)).astype(_MXU)


def _sb_kernel(q_ref, k_ref, vt_ref, o_ref, acc_ref, rest_ref, *, heads):
    QB = Q_BLOCK
    i = pl.program_id(2)
    scale = HEAD_DIM ** -0.5
    row = lax.broadcasted_iota(jnp.int32, (QB, QB), 0)
    col = lax.broadcasted_iota(jnp.int32, (QB, QB), 1)
    later = jnp.where(col > row, 1.0, 0.0).astype(_MXU)
    strict = row < col

    def block(c, first):
        off = pl.multiple_of(c * QB, QB)
        hss = [slice(h * HEAD_DIM, (h + 1) * HEAD_DIM) for h in range(heads)]
        zs = [_dot_nt(k_ref[pl.ds(off, QB), hs], q_ref[:, hs]) * scale for hs in hss]
        log_bs, pieces, totals = [], [], []
        for z in zs:
            log_b, log_1mb = _log_sigmoid_pair(z)
            if first:
                log_1mb = jnp.where(strict, log_1mb, 0.0)
            log_bs.append(log_b)
            pieces.append(_split(log_1mb, 2))
            totals.append(jnp.sum(log_1mb, axis=0, keepdims=True))
        betweens = [_dot(later, p[0]) + _dot(later, p[1]) for p in pieces]
        probs = []
        for h in range(heads):
            if first:
                probs.append(jnp.where(strict, jnp.exp(log_bs[h] + betweens[h]), 0.0).astype(_MXU))
                rest_ref[h] = totals[h]
            else:
                rest = rest_ref[h]
                probs.append(jnp.exp(log_bs[h] + betweens[h] + rest).astype(_MXU))
                rest_ref[h] = rest + totals[h]
        upds = [_dot(vt_ref[hs, pl.ds(off, QB)], probs[h]) for h, hs in enumerate(hss)]
        for h in range(heads):
            acc_ref[h] = upds[h] if first else acc_ref[h] + upds[h]

    block(i, True)

    def body(n, carry):
        block(i - 1 - n, False)
        return carry
    lax.fori_loop(0, i, body, 0)
    for h in range(heads):
        o_ref[:, h * HEAD_DIM:(h + 1) * HEAD_DIM] = acc_ref[h].T.astype(o_ref.dtype)


def stick_breaking(pa, pvt, bsz, seq, heads=SB_HEADS):
    nqb = seq // Q_BLOCK
    w = heads * HEAD_DIM
    qb, kb, vb = _A_BLK[_Q_S] * LANE // w, _A_BLK[_K_S] * LANE // w, _T_BLK[_V_S] * LANE // w
    return pl.pallas_call(
        functools.partial(_sb_kernel, heads=heads),
        grid=(bsz, SB_HEADS // heads, nqb),
        in_specs=[pl.BlockSpec((Q_BLOCK, w), lambda b, h, i: (b * nqb + i, qb + h)),
                  pl.BlockSpec((seq, w), lambda b, h, i: (b, kb + h)),
                  pl.BlockSpec((w, seq), lambda b, h, i: (vb + h, b))],
        out_specs=pl.BlockSpec((Q_BLOCK, w), lambda b, h, i: (b * nqb + i, h)),
        out_shape=jax.ShapeDtypeStruct((bsz * seq, SB_W), _MXU),
        scratch_shapes=[pltpu.VMEM((heads, HEAD_DIM, Q_BLOCK), _F32),
                        pltpu.VMEM((heads, 1, Q_BLOCK), _F32)],
        compiler_params=_params("parallel", "parallel", "arbitrary"),
        name="stick_breaking",
    )(pa, pa, pvt)


def _hgrn_kernel(q_ref, f_ref, i_ref, g_ref, llb_ref, l1lb_ref, omlb_ref, ng_ref, o_ref, state_ref, *, seq, heads):
    C, SUB = HG_CHUNK, HG_SUB
    nsub = C // SUB
    hss = [slice(h * HG_DK, (h + 1) * HG_DK) for h in range(heads)]
    row = lax.broadcasted_iota(jnp.int32, (C, C), 0)
    col = lax.broadcasted_iota(jnp.int32, (C, C), 1)
    upto = jnp.where(col <= row, 1.0, 0.0).astype(_MXU)
    sub_t = lax.broadcasted_iota(jnp.int32, (SUB, 1), 0)
    sub_c = lax.broadcasted_iota(jnp.int32, (SUB, C), 1)
    state_ref[...] = jnp.zeros(state_ref.shape, _F32)

    def chunk(n, carry):
        off = pl.multiple_of(n * C, C)
        fps = [f_ref[pl.ds(off, C), hs] for hs in hss]
        pieces = []
        for fp, hs in zip(fps, hss):
            log_sig, _ = _log_sigmoid_pair(fp)
            y = l1lb_ref[:, hs] + log_sig
            a = llb_ref[:, hs]
            log_f = jnp.maximum(a, y) + jnp.log1p(jnp.exp(-jnp.abs(a - y)))
            pieces.append(_split(log_f, 3))
        bs = [_dot(upto, p[0]) + _dot(upto, p[1]) + _dot(upto, p[2]) for p in pieces]
        qs = [q_ref[pl.ds(off, C), hs].astype(_F32) for hs in hss]
        ks = [omlb_ref[:, hs] * jax.nn.sigmoid(-fp) for fp, hs in zip(fps, hss)]
        vs = [i_ref[pl.ds(off, C), hs].astype(_F32) for hs in hss]

        outs = [_dot_nt((q * jnp.exp(b)).astype(_MXU), state_ref[h].astype(_MXU))
                for h, (q, b) in enumerate(zip(qs, bs))]

        lhs, rhs = [], []
        for q, k, b in zip(qs, ks, bs):
            for s_i in range(1, nsub):
                lo = s_i * SUB
                b_ref = b[lo - 1:lo, :]
                lhs.append((q[lo:lo + SUB, :] * jnp.exp(b[lo:lo + SUB, :] - b_ref)).astype(_MXU))
                rhs.append((k * jnp.exp(jnp.minimum(b_ref - b, 0.0))).astype(_MXU))
        a_blocks = [_dot_nt(x, y) for x, y in zip(lhs, rhs)]
        intra = []
        for h in range(heads):
            rows = [jnp.zeros((SUB, C), _F32)]
            for s_i in range(1, nsub):
                rows.append(jnp.where(sub_c < s_i * SUB, a_blocks[h * (nsub - 1) + s_i - 1], 0.0))
            intra.append(jnp.concatenate(rows, axis=0).astype(_MXU))
        outs = [o + _dot(a, v.astype(_MXU)) for o, a, v in zip(outs, intra, vs)]

        for h, (q, k, v, b) in enumerate(zip(qs, ks, vs, bs)):
            diag = []
            for s_i in range(nsub):
                lo = s_i * SUB
                b_i = b[lo:lo + SUB, :]
                q_i = q[lo:lo + SUB, :]
                o_d = jnp.zeros((SUB, HG_DV), _F32)
                for s in range(SUB):
                    w = jnp.exp(b_i - b[lo + s:lo + s + 1, :])
                    a_col = jnp.sum(q_i * k[lo + s:lo + s + 1, :] * w, axis=-1, keepdims=True)
                    a_col = jnp.where(sub_t >= s, a_col, 0.0)
                    o_d = o_d + a_col * v[lo + s:lo + s + 1, :]
                diag.append(o_d)
            outs[h] = outs[h] + jnp.concatenate(diag, axis=0)

        k_decs = [(k * jnp.exp(b[C - 1:C, :] - b)).astype(_MXU) for k, b in zip(ks, bs)]
        grown = [_dot(v.T.astype(_MXU), kd) for v, kd in zip(vs, k_decs)]
        for h, (b, hs) in enumerate(zip(bs, hss)):
            state_ref[h] = state_ref[h] * jnp.exp(b[C - 1:C, :]) + grown[h]
            gate = jax.nn.sigmoid(g_ref[pl.ds(off, C), hs])
            o_ref[pl.ds(off, C), hs] = (_rms(outs[h], ng_ref[...]) * gate).astype(o_ref.dtype)
        return carry

    lax.fori_loop(0, seq // C, chunk, 0)


def hgrn2(pa, pb, lb, norm_gain, bsz, seq, heads=4):
    lb = lb.reshape(1, HG_KW).astype(_F32)
    log_lb = jnp.log(lb)
    log_1mlb = jnp.log1p(-lb)
    one_mlb = 1.0 - lb
    w = heads * HG_DK

    def seq_spec(blk):
        return pl.BlockSpec((seq, w), lambda b, h, s=blk * LANE // w: (b, s + h))

    head_vec = pl.BlockSpec((1, w), lambda b, h: (0, h))
    return pl.pallas_call(
        functools.partial(_hgrn_kernel, seq=seq, heads=heads),
        grid=(bsz, HG_HEADS // heads),
        in_specs=[seq_spec(_A_BLK[_Q_H]), seq_spec(_B_BLK[_F_H]), seq_spec(_A_BLK[_I_H]), seq_spec(_B_BLK[_G_H]),
                  head_vec, head_vec, head_vec,
                  pl.BlockSpec((1, HG_DV), lambda b, h: (0, 0))],
        out_specs=pl.BlockSpec((seq, w), lambda b, h: (b, h)),
        out_shape=jax.ShapeDtypeStruct((bsz * seq, HG_VW), _MXU),
        scratch_shapes=[pltpu.VMEM((heads, HG_DV, HG_DK), _F32)],
        compiler_params=_params("parallel", "parallel"),
        name="hgrn2",
    )(pa, pb, pa, pb, log_lb, log_1mlb, one_mlb, norm_gain.reshape(1, HG_DV))


def _merge_kernel(o0_ref, o1_ref, o2_ref, w0_ref, w1_ref, w2_ref, g0_ref, g1_ref, g2_ref, out_ref):
    acc = jax.nn.sigmoid(g0_ref[...]) * _dot(o0_ref[...], w0_ref[0])
    acc = acc + jax.nn.sigmoid(g1_ref[...]) * _dot(o1_ref[...], w1_ref[0])
    acc = acc + jax.nn.sigmoid(g2_ref[...]) * _dot(o2_ref[...], w2_ref[0])
    out_ref[...] = acc.astype(out_ref.dtype)


def merge_branches(o_nsa, o_sb, o_hg, w_branch, pb, tm=1024, tn=512):
    m = o_nsa.shape[0]
    nj = D_MODEL // tn
    gblk = _B_BLK[_GATES] * LANE // tn

    def o_spec():
        return pl.BlockSpec((tm, BRANCH_W), lambda i, j: (i, 0))

    def w_spec(n):
        return pl.BlockSpec((1, BRANCH_W, tn), lambda i, j, n=n: (n, 0, j))

    def g_spec(n):
        return pl.BlockSpec((tm, tn), lambda i, j, n=n: (i, gblk + n * nj + j))

    return pl.pallas_call(
        _merge_kernel,
        grid=(m // tm, nj),
        in_specs=[o_spec(), o_spec(), o_spec(), w_spec(0), w_spec(1), w_spec(2), g_spec(0), g_spec(1), g_spec(2)],
        out_specs=pl.BlockSpec((tm, tn), lambda i, j: (i, j)),
        out_shape=jax.ShapeDtypeStruct((m, D_MODEL), _MXU),
        compiler_params=_params("parallel", "arbitrary"),
        name="merge_branches",
    )(o_nsa, o_sb, o_hg, w_branch, w_branch, w_branch, pb, pb, pb)


def _ffn_up_kernel(x_ref, halo_ref, gain_ref, wg_ref, wv_ref, cwg_ref, cwv_ref, cbg_ref, cbv_ref, o_ref, h_ref,
                   *, tiles_per_seq, sub):
    i = pl.program_id(0)

    @pl.when(pl.program_id(1) == 0)
    def _():
        h_ref[...] = _rms(x_ref[...], gain_ref[...]).astype(_MXU)

    first = (i % tiles_per_seq) == 0
    h = h_ref[...]
    halo = _rms(halo_ref[...], gain_ref[...]).astype(_MXU)
    tm = h.shape[0]
    hrows = halo.shape[0]
    row = lax.broadcasted_iota(jnp.int32, (tm, 1), 0)
    cols = [slice(c * sub, (c + 1) * sub) for c in range(wg_ref.shape[1] // sub)]

    prods = [[(_dot(h, w_ref[:, cs]), jnp.where(first, 0.0, _dot(halo, w_ref[:, cs]))) for w_ref in (wg_ref, wv_ref)]
             for cs in cols]

    def conv(u, uh, cw, cb):
        u1 = jnp.where(row == 0, uh[hrows - 1:hrows, :], pltpu.roll(u, 1, axis=0))
        u2 = jnp.where(row == 0, uh[hrows - 2:hrows - 1, :],
                       jnp.where(row == 1, uh[hrows - 1:hrows, :], pltpu.roll(u, 2, axis=0)))
        return cb + u2 * cw[0:1, :] + u1 * cw[1:2, :] + u * cw[2:3, :]

    for cs, ((ug, uhg), (uv, uhv)) in zip(cols, prods):
        gate = conv(ug, uhg, cwg_ref[:, cs], cbg_ref[:, cs])
        val = conv(uv, uhv, cwv_ref[:, cs], cbv_ref[:, cs])
        o_ref[:, cs] = (jax.nn.silu(gate) * val).astype(o_ref.dtype)


def ffn_up(x, gain, w_up, conv_w, conv_b, seq, tm=1024, tn=512, sub=256):
    m, k = x.shape
    nj = D_FF // tn
    hrows = 8
    conv_b = conv_b.reshape(1, 2 * D_FF)
    return pl.pallas_call(
        functools.partial(_ffn_up_kernel, tiles_per_seq=seq // tm, sub=sub),
        grid=(m // tm, nj),
        in_specs=[pl.BlockSpec((tm, k), lambda i, j: (i, 0)),
                  pl.BlockSpec((hrows, k), lambda i, j: (jnp.maximum(i * (tm // hrows) - 1, 0), 0)),
                  pl.BlockSpec((1, k), lambda i, j: (0, 0)),
                  pl.BlockSpec((k, tn), lambda i, j: (0, j)),
                  pl.BlockSpec((k, tn), lambda i, j: (0, nj + j)),
                  pl.BlockSpec((CONV_W, tn), lambda i, j: (0, j)),
                  pl.BlockSpec((CONV_W, tn), lambda i, j: (0, nj + j)),
                  pl.BlockSpec((1, tn), lambda i, j: (0, j)),
                  pl.BlockSpec((1, tn), lambda i, j: (0, nj + j))],
        out_specs=pl.BlockSpec((tm, tn), lambda i, j: (i, j)),
        out_shape=jax.ShapeDtypeStruct((m, D_FF), _MXU),
        scratch_shapes=[pltpu.VMEM((tm, k), _MXU)],
        compiler_params=_params("parallel", "arbitrary"),
        name="ffn_up_conv",
    )(x, x, gain.reshape(1, k), w_up, w_up, conv_w, conv_w, conv_b, conv_b)


def _gather_cols(w, segs):
    return jnp.concatenate([w[:, _SPLIT_OFF[s]:_SPLIT_OFF[s + 1]] for s in segs], axis=1)


def _nsa_gate_cols(w):
    g = w[:, _SPLIT_OFF[_G_N]:_SPLIT_OFF[_G_N + 1]].reshape(-1, NSA_KV_HEADS, NSA_GROUP * 3)
    g = jnp.pad(g, ((0, 0), (0, 0), (0, LANE - NSA_GROUP * 3)))
    return g.reshape(-1, NSA_KV_HEADS * LANE)


def kernel(x, norm_attn, w_in, nsa_q_gain, nsa_k_gain, cmp_pos, cmp_w1, cmp_w2, rel_bias, hg_lower_bound,
           hg_norm_gain, w_branch, w_out, norm_ffn, w_up, conv_w, conv_b, w_down):
    bsz, seq, d = x.shape
    m = bsz * seq
    lb_all = jnp.cumsum(jax.nn.softmax(hg_lower_bound.astype(_F32), axis=0), axis=0)
    lb_all = jnp.maximum(lb_all - lb_all[0:1], 0.0)
    tables = nsa_tables(rel_bias, seq)
    xf = x.reshape(m, d)
    for l in range(w_in.shape[0]):
        wl = w_in[l]
        h = rmsnorm_cast(xf, norm_attn[l])
        pa = matmul(h, _gather_cols(wl, _A_ORDER).astype(_MXU), _MXU, tm=512, tn=A_COLS // 2, name="proj_a")
        pb = matmul(h, _gather_cols(wl, _B_ORDER).astype(_MXU), _F32, tn=1024, name="proj_b")
        pkc = matmul(h, _gather_cols(wl, (_KC,)).astype(_MXU), _MXU, name="proj_kc")
        pvc = matmul(h, _gather_cols(wl, (_VC,)).astype(_MXU), _MXU, name="proj_vc")
        pg = matmul(h, _nsa_gate_cols(wl).astype(_MXU), _F32, name="proj_g")
        blk_rows = seq // CMP_STRIDE
        kcmp = compress(pkc.reshape(bsz, blk_rows, CMP_STRIDE * NSA_KV), cmp_w1[l, 0], cmp_w2[l, 0], cmp_pos[l, 0],
                        nsa_k_gain[l, 0], True)
        vcmp_t = compress(pvc.reshape(bsz, blk_rows, CMP_STRIDE * NSA_KV), cmp_w1[l, 1], cmp_w2[l, 1], cmp_pos[l, 1],
                          nsa_k_gain[l, 0], False)
        pvt = matmul_nt(_gather_cols(wl, _T_ORDER).T.astype(_MXU), h, _MXU, name="proj_vt")
        o_nsa = nsa_attention(pa, pvt, kcmp, vcmp_t, pg, tables, nsa_q_gain[l], nsa_k_gain[l], bsz, seq)
        o_sb = stick_breaking(pa, pvt, bsz, seq)
        o_hg = hgrn2(pa, pb, lb_all[l], hg_norm_gain[l], bsz, seq)
        merged = merge_branches(o_nsa, o_sb, o_hg, w_branch[l].astype(_MXU), pb)
        xf = matmul(merged, w_out[l].astype(_MXU), _F32, residual=xf, tn=1024, name="out_proj")
        act = ffn_up(xf, norm_ffn[l], w_up[l].astype(_MXU), conv_w[l], conv_b[l], seq)
        xf = matmul(act, w_down[l].astype(_MXU), _F32, residual=xf, tm=512, tn=1024, name="ffn_down")
    return xf.reshape(bsz, seq, d)
```

```python
import functools
import math

import numpy as np
import jax
import jax.numpy as jnp
from jax import lax
from jax.experimental import pallas as pl
from jax.experimental.pallas import tpu as pltpu

D_MODEL = 2048
DEPTH = 4
HEAD_DIM = 128
BRANCH_W = D_MODEL // 2
N_BRANCH = 3
NSA_HEADS = BRANCH_W // HEAD_DIM
NSA_KV_HEADS = NSA_HEADS // 4
NSA_GROUP = NSA_HEADS // NSA_KV_HEADS
CMP_BLOCK = 32
CMP_STRIDE = 16
SEL_BLOCK = 64
SEL_TOPK = 16
WINDOW = 512
Q_BLOCK = 128
SB_HEADS = BRANCH_W // HEAD_DIM
HG_DK = 128
HG_DV = 128
HG_HEADS = BRANCH_W // HG_DV
HG_CHUNK = 64
HG_SUB = 8
D_FF = 256 * ((8 * D_MODEL // 3 + 255) // 256)
CONV_W = 3
REL_BUCKETS = 32
REL_MAX_DIST = 128
EPS = 1e-6
NEG = -1e30
TINY = 1e-30

NSA_Q = NSA_HEADS * HEAD_DIM
NSA_KV = NSA_KV_HEADS * HEAD_DIM
NSA_GATE = 3 * NSA_HEADS
SB_W = SB_HEADS * HEAD_DIM
HG_KW = HG_HEADS * HG_DK
HG_VW = HG_HEADS * HG_DV
SPLIT_SIZES = (NSA_Q,) + (NSA_KV,) * 6 + (NSA_GATE,) + (SB_W,) * 3 + (HG_KW, HG_KW, HG_VW, HG_VW, N_BRANCH * D_MODEL)
(_Q_N, _KC, _VC, _KS, _VS, _KW, _VW, _G_N, _Q_S, _K_S, _V_S, _Q_H, _F_H, _I_H, _G_H, _GATES) = range(16)
_SPLIT_OFF = np.concatenate([[0], np.cumsum(SPLIT_SIZES)]).tolist()

LANE = 128
_A_ORDER = (_Q_N, _Q_S, _K_S, _Q_H, _I_H, _KS, _KW)
_A_BLK = {}
_off = 0
for _s in _A_ORDER:
    _A_BLK[_s] = _off // LANE
    _off += SPLIT_SIZES[_s]
A_COLS = _off
_T_ORDER = (_V_S, _VS, _VW)
_T_BLK = {}
_off = 0
for _s in _T_ORDER:
    _T_BLK[_s] = _off // LANE
    _off += SPLIT_SIZES[_s]
T_ROWS = _off
_B_ORDER = (_F_H, _G_H)
_B_BLK = {}
_off = 0
for _s in _B_ORDER:
    _B_BLK[_s] = _off // LANE
    _off += SPLIT_SIZES[_s]
B_COLS = _off

VMEM_LIMIT = 48 * 1024 * 1024

_MXU = jnp.bfloat16
_F32 = jnp.float32


def _dot(a, b):
    return jnp.dot(a, b, preferred_element_type=_F32)


def _dot_nt(a, b):
    return lax.dot_general(a, b, (((1,), (1,)), ((), ())), preferred_element_type=_F32)


def _split(x, parts):
    out = []
    r = x
    for _ in range(parts):
        h = r.astype(_MXU)
        out.append(h)
        r = r - h.astype(_F32)
    return out


def _params(*sem):
    return pltpu.CompilerParams(dimension_semantics=sem, vmem_limit_bytes=VMEM_LIMIT)


def _rms(x, gain):
    return x * lax.rsqrt(jnp.mean(x * x, axis=-1, keepdims=True) + EPS) * gain


def _log_sigmoid_pair(z):
    lp = jnp.log(1.0 + jnp.exp(-jnp.abs(z)))
    return jnp.minimum(z, 0.0) - lp, jnp.minimum(-z, 0.0) - lp


def _rmsnorm_kernel(x_ref, g_ref, o_ref):
    o_ref[...] = _rms(x_ref[...], g_ref[...]).astype(o_ref.dtype)


def rmsnorm_cast(x2d, gain):
    m, d = x2d.shape
    tm = 512
    return pl.pallas_call(
        _rmsnorm_kernel,
        grid=(m // tm,),
        in_specs=[pl.BlockSpec((tm, d), lambda i: (i, 0)),
                  pl.BlockSpec((1, d), lambda i: (0, 0))],
        out_specs=pl.BlockSpec((tm, d), lambda i: (i, 0)),
        out_shape=jax.ShapeDtypeStruct((m, d), _MXU),
        compiler_params=_params("parallel"),
        name="rmsnorm_cast",
    )(x2d, gain.reshape(1, d))


def _mm_kernel(a_ref, w_ref, o_ref):
    o_ref[...] = _dot(a_ref[...], w_ref[...]).astype(o_ref.dtype)


def _mm_res_kernel(a_ref, w_ref, r_ref, o_ref):
    o_ref[...] = (r_ref[...] + _dot(a_ref[...], w_ref[...])).astype(o_ref.dtype)


def matmul(a, w, out_dtype, residual=None, tm=1024, tn=512, name="matmul"):
    m, k = a.shape
    n = w.shape[1]
    tm = min(tm, m)
    tn = min(tn, n)
    in_specs = [pl.BlockSpec((tm, k), lambda i, j: (i, 0)),
                pl.BlockSpec((k, tn), lambda i, j: (0, j))]
    args = [a, w]
    kern = _mm_kernel
    if residual is not None:
        in_specs.append(pl.BlockSpec((tm, tn), lambda i, j: (i, j)))
        args.append(residual)
        kern = _mm_res_kernel
    return pl.pallas_call(
        kern,
        grid=(m // tm, n // tn),
        in_specs=in_specs,
        out_specs=pl.BlockSpec((tm, tn), lambda i, j: (i, j)),
        out_shape=jax.ShapeDtypeStruct((m, n), out_dtype),
        compiler_params=_params("parallel", "arbitrary"),
        name=name,
    )(*args)


def _mm_nt_kernel(wt_ref, a_ref, o_ref):
    o_ref[...] = _dot_nt(wt_ref[...], a_ref[...]).astype(o_ref.dtype)


def matmul_nt(wt, a, out_dtype, tm=1024, tn=512, name="matmul_nt"):
    n, k = wt.shape
    m = a.shape[0]
    tm = min(tm, m)
    tn = min(tn, n)
    return pl.pallas_call(
        _mm_nt_kernel,
        grid=(m // tm, n // tn),
        in_specs=[pl.BlockSpec((tn, k), lambda i, j: (j, 0)),
                  pl.BlockSpec((tm, k), lambda i, j: (i, 0))],
        out_specs=pl.BlockSpec((tn, tm), lambda i, j: (j, i)),
        out_shape=jax.ShapeDtypeStruct((n, m), out_dtype),
        compiler_params=_params("parallel", "arbitrary"),
        name=name,
    )(wt, a)


def _compress_kernel(x_ref, wa_ref, wb_ref, pa_ref, pb_ref, w2_ref, gain_ref, o_ref, *, normalize):
    x = x_ref[0].astype(_F32)
    u = _dot((x + pa_ref[...]).astype(_MXU), wa_ref[...])
    v = _dot((x + pb_ref[...]).astype(_MXU), wb_ref[...])
    hid = jax.nn.gelu(u + pltpu.roll(v, v.shape[0] - 1, axis=0)).astype(_MXU)
    if normalize:
        out = _dot(hid, w2_ref[...])
        out = jnp.concatenate(
            [_rms(out[:, g * HEAD_DIM:(g + 1) * HEAD_DIM], gain_ref[...]) for g in range(NSA_KV_HEADS)], axis=1)
    else:
        out = _dot_nt(w2_ref[...], hid)
    o_ref[0] = out.astype(o_ref.dtype)


def compress(x, w1, w2, pos, gain, normalize):
    bsz = x.shape[0]
    nrow = x.shape[1]
    kdim = x.shape[2]
    half = CMP_BLOCK // 2
    eye = jnp.eye(NSA_KV_HEADS, dtype=w1.dtype)

    def embed(w):
        return jnp.einsum('lde,gh->lgdhe', w, eye).reshape(kdim, NSA_KV).astype(_MXU)

    def tile_pos(p):
        return jnp.broadcast_to(p[:, None, :], (half, NSA_KV_HEADS, HEAD_DIM)).reshape(1, kdim)

    w2b = jnp.einsum('de,gh->gdhe', w2, eye).reshape(NSA_KV, NSA_KV).astype(_MXU)
    out_block = (1, nrow, NSA_KV)
    if not normalize:
        w2b = w2b.T
        out_block = (1, NSA_KV, nrow)
    return pl.pallas_call(
        functools.partial(_compress_kernel, normalize=normalize),
        grid=(bsz,),
        in_specs=[pl.BlockSpec((1, nrow, kdim), lambda b: (b, 0, 0)),
                  pl.BlockSpec((kdim, NSA_KV), lambda b: (0, 0)),
                  pl.BlockSpec((kdim, NSA_KV), lambda b: (0, 0)),
                  pl.BlockSpec((1, kdim), lambda b: (0, 0)),
                  pl.BlockSpec((1, kdim), lambda b: (0, 0)),
                  pl.BlockSpec((NSA_KV, NSA_KV), lambda b: (0, 0)),
                  pl.BlockSpec((1, HEAD_DIM), lambda b: (0, 0))],
        out_specs=pl.BlockSpec(out_block, lambda b: (b, 0, 0)),
        out_shape=jax.ShapeDtypeStruct((bsz,) + out_block[1:], _MXU),
        compiler_params=_params("parallel"),
        name="nsa_compress",
    )(x, embed(w1[:half]), embed(w1[half:]), tile_pos(pos[:half]), tile_pos(pos[half:]), w2b,
      gain.reshape(1, HEAD_DIM))


def _nsa_kernel(q_ref, kc_ref, vct_ref, ks_ref, vst_ref, kw_ref, vwt_ref, gate_ref,
                bdiag_ref, boff_ref, bfar_ref, bcmp_ref, qg_ref, kg_ref, ovt_ref,
                o_ref, ksn_ref, kwn_ref, m_ref, l_ref, acc_ref, ocmp_ref, sel_ref, *, seq):
    R = NSA_GROUP
    QB = Q_BLOCK
    W = R * QB
    i = pl.program_id(2)
    scale = HEAD_DIM ** -0.5
    sel_shift = int(math.log2(SEL_BLOCK))

    @pl.when(i == 0)
    def _():
        def body(c, carry):
            off = pl.multiple_of(c * QB, QB)
            ksn_ref[pl.ds(off, QB), :] = _rms(ks_ref[pl.ds(off, QB), :].astype(_F32), kg_ref[1:2, :]).astype(_MXU)
            kwn_ref[pl.ds(off, QB), :] = _rms(kw_ref[pl.ds(off, QB), :].astype(_F32), kg_ref[2:3, :]).astype(_MXU)
            return carry
        lax.fori_loop(0, seq // QB, body, 0)

    heads = [slice(r * QB, (r + 1) * QB) for r in range(R)]
    q_all = jnp.concatenate(
        [_rms(q_ref[:, hs].astype(_F32), qg_ref[...]).astype(_MXU) for hs in heads], axis=0)

    row = lax.broadcasted_iota(jnp.int32, (QB, QB), 0)
    col = lax.broadcasted_iota(jnp.int32, (QB, QB), 1)

    m_ref[...] = jnp.full(m_ref.shape, NEG, _F32)
    l_ref[...] = jnp.zeros(l_ref.shape, _F32)
    acc_ref[...] = jnp.zeros(acc_ref.shape, _F32)

    def attend_pre(jobs):
        staged = []
        for branch, s_all, mask, vt in jobs:
            p_parts, alphas = [], []
            for r, hs in enumerate(heads):
                slot = branch * R + r
                s = s_all[:, hs]
                if mask is not None:
                    s = jnp.where(mask, s, NEG)
                m_old = m_ref[slot]
                m_new = jnp.maximum(m_old, jnp.max(s, axis=0, keepdims=True))
                alpha = jnp.exp(m_old - m_new)
                p = jnp.exp(s - m_new)
                if mask is not None:
                    p = jnp.where(mask, p, 0.0)
                l_ref[slot] = alpha * l_ref[slot] + jnp.sum(p, axis=0, keepdims=True)
                m_ref[slot] = m_new
                p_parts.append(p.astype(_MXU))
                alphas.append(alpha)
            staged.append((branch, vt, jnp.concatenate(p_parts, axis=1), alphas))
        return staged

    def attend_post(staged):
        upds = [_dot(vt, p_all) for _, vt, p_all, _ in staged]
        for (branch, _, _, alphas), upd in zip(staged, upds):
            for r, hs in enumerate(heads):
                acc_ref[branch, :, hs] = alphas[r] * acc_ref[branch, :, hs] + upd[:, hs]

    def attend(jobs):
        attend_post(attend_pre(jobs))

    valid_c = i * QB + col >= CMP_STRIDE * row + (CMP_BLOCK - 1)
    shift = QB // CMP_STRIDE
    bias_c = bcmp_ref[0, pl.ds(pl.multiple_of((seq // QB - 1 - i) * shift, shift), seq // CMP_STRIDE), :]
    s_c = _dot_nt(kc_ref[0], q_all) * scale + bias_c

    nwb = WINDOW // QB
    rows_w = (nwb - 1) * QB
    start_w = jnp.maximum(i - nwb, 0)
    off_w = pl.multiple_of(start_w * QB, QB)
    key_w = start_w * QB + lax.broadcasted_iota(jnp.int32, (rows_w, QB), 0)
    qry_w = i * QB + lax.broadcasted_iota(jnp.int32, (rows_w, QB), 1)
    mask_w = (key_w > qry_w - WINDOW) & (key_w < (i - 1) * QB)
    s_w = _dot_nt(kwn_ref[pl.ds(off_w, rows_w), :], q_all) * scale + bfar_ref[0]

    p_parts = []
    psum = jnp.zeros((QB, QB), _F32)
    for hs in heads:
        s = jnp.where(valid_c, s_c[:, hs], NEG)
        e = jnp.where(valid_c, jnp.exp(s - jnp.max(s, axis=0, keepdims=True)), 0.0)
        p = e * (1.0 / jnp.maximum(jnp.sum(e, axis=0, keepdims=True), TINY))
        psum = psum + p
        p_parts.append(p.astype(_MXU))
    staged_w = attend_pre([(1, s_w, mask_w, vwt_ref[:, pl.ds(off_w, rows_w)])])
    ocmp_ref[...] = _dot(vct_ref[0], jnp.concatenate(p_parts, axis=1))
    ovt = ovt_ref[...]
    imp = functools.reduce(lambda a, b: a + b, [_dot(ovt, h) for h in _split(psum, 2)])
    attend_post(staged_w)

    n_sel = seq // SEL_BLOCK
    jrow = lax.broadcasted_iota(jnp.int32, (n_sel, QB), 0)
    tcol = lax.broadcasted_iota(jnp.int32, (n_sel, QB), 1)
    qblk = lax.shift_right_arithmetic(i * QB + tcol, sel_shift)
    causal_b = jrow <= qblk
    forced = causal_b & ((jrow == 0) | (jrow >= qblk - 1))
    score = jnp.where(forced, jnp.inf, jnp.where(causal_b, imp[:n_sel, :], -jnp.inf))
    rank = jnp.zeros((n_sel, QB), jnp.int32)
    for jj in range(n_sel):
        sc = score[jj:jj + 1, :]
        beats = (sc > score) | ((sc == score) & (jrow > jj))
        rank = rank + beats.astype(jnp.int32)
    sel_ref[...] = jnp.where((rank < SEL_TOPK) & causal_b, 1.0, 0.0)

    def scores(k_ref, off, rows, bias):
        return _dot_nt(k_ref[pl.ds(off, rows), :], q_all) * scale + bias

    def sel_mask(c, nblk):
        rix = lax.broadcasted_iota(jnp.int32, (nblk * QB, QB), 0)
        per_key_block = QB // SEL_BLOCK
        flags = [sel_ref[pl.ds(per_key_block * c + j, 1), :] for j in range(per_key_block * nblk)]
        m = flags[-1]
        for j in reversed(range(len(flags) - 1)):
            m = jnp.where(rix < (j + 1) * SEL_BLOCK, flags[j], m)
        return m > 0.5

    lower = row <= col
    bfar, boff, bdiag = bfar_ref[0], boff_ref[0], bdiag_ref[0]

    n_far = jnp.maximum(i - 1, 0)
    wide = 4

    def far_tile(c, nblk):
        off = pl.multiple_of(c * QB, QB)
        attend([(0, scores(ksn_ref, off, nblk * QB, bfar), sel_mask(c, nblk), vst_ref[:, pl.ds(off, nblk * QB)])])

    def far_body(p, carry):
        far_tile(p * wide, wide)
        return carry
    lax.fori_loop(0, n_far // wide, far_body, 0)
    rem = n_far % wide

    @pl.when(rem >= 2)
    def _():
        far_tile(n_far - rem, 2)

    @pl.when(rem % 2 == 1)
    def _():
        far_tile(n_far - 1, 1)

    @pl.when(i >= 1)
    def _():
        off = pl.multiple_of((i - 1) * QB, QB)
        bias = jnp.concatenate([boff, bdiag], axis=0)
        after_first = lax.broadcasted_iota(jnp.int32, (2 * QB, QB), 0) - QB
        tcol = lax.broadcasted_iota(jnp.int32, (2 * QB, QB), 1)
        causal = after_first <= tcol
        s_sel = scores(ksn_ref, off, 2 * QB, bias)
        s_win = scores(kwn_ref, off, 2 * QB, bias)
        attend([(0, s_sel, sel_mask(i - 1, 2) & causal, vst_ref[:, pl.ds(off, 2 * QB)]),
                (1, s_win, causal, vwt_ref[:, pl.ds(off, 2 * QB)])])

    @pl.when(i == 0)
    def _():
        s_sel = scores(ksn_ref, 0, QB, bdiag)
        s_win = scores(kwn_ref, 0, QB, bdiag)
        attend([(0, s_sel, sel_mask(0, 1) & lower, vst_ref[:, 0:QB]),
                (1, s_win, lower, vwt_ref[:, 0:QB])])

    gate_t = jax.nn.sigmoid(gate_ref[...].T[:4 * R, :])
    for r, hs in enumerate(heads):
        g_cmp = gate_t[3 * r:3 * r + 1, :]
        g_sel = gate_t[3 * r + 1:3 * r + 2, :] * (1.0 / jnp.maximum(l_ref[r], TINY))
        g_win = gate_t[3 * r + 2:3 * r + 3, :] * (1.0 / jnp.maximum(l_ref[R + r], TINY))
        o_t = g_cmp * ocmp_ref[:, hs] + g_sel * acc_ref[0, :, hs] + g_win * acc_ref[1, :, hs]
        o_ref[:, hs] = o_t.T.astype(o_ref.dtype)


def nsa_attention(pa, pvt, kcmp, vcmp_t, pg, tables, q_gain, k_gain, bsz, seq):
    nqb = seq // Q_BLOCK
    G, R = NSA_KV_HEADS, NSA_GROUP
    W = R * Q_BLOCK
    bdiag, boff, bfar, bcmp, overlap_t = tables
    n_cmp_rows = seq // CMP_STRIDE

    def k_spec(seg):
        return pl.BlockSpec((seq, HEAD_DIM), lambda b, g, i, s=_A_BLK[seg]: (b, s + g))

    def vt_spec(seg):
        return pl.BlockSpec((HEAD_DIM, seq), lambda b, g, i, s=_T_BLK[seg]: (s + g, b))

    group_tile = pl.BlockSpec((1, Q_BLOCK, W), lambda b, g, i: (g, 0, 0))
    return pl.pallas_call(
        functools.partial(_nsa_kernel, seq=seq),
        grid=(bsz, G, nqb),
        in_specs=[pl.BlockSpec((Q_BLOCK, W), lambda b, g, i: (b * nqb + i, g)),
                  pl.BlockSpec((1, n_cmp_rows, HEAD_DIM), lambda b, g, i: (b, 0, g)),
                  pl.BlockSpec((1, HEAD_DIM, n_cmp_rows), lambda b, g, i: (b, g, 0)),
                  k_spec(_KS), vt_spec(_VS), k_spec(_KW), vt_spec(_VW),
                  pl.BlockSpec((Q_BLOCK, LANE), lambda b, g, i: (b * nqb + i, g)),
                  group_tile, group_tile,
                  pl.BlockSpec((1, 1, W), lambda b, g, i: (g, 0, 0)),
                  pl.BlockSpec((1,) + bcmp.shape[1:], lambda b, g, i: (g, 0, 0)),
                  pl.BlockSpec((1, HEAD_DIM), lambda b, g, i: (0, 0)),
                  pl.BlockSpec((3, HEAD_DIM), lambda b, g, i: (0, 0)),
                  pl.BlockSpec((LANE, LANE), lambda b, g, i: (0, 0))],
        out_specs=pl.BlockSpec((Q_BLOCK, W), lambda b, g, i: (b * nqb + i, g)),
        out_shape=jax.ShapeDtypeStruct((bsz * seq, NSA_Q), _MXU),
        scratch_shapes=[pltpu.VMEM((seq, HEAD_DIM), _MXU),
                        pltpu.VMEM((seq, HEAD_DIM), _MXU),
                        pltpu.VMEM((2 * R, 1, Q_BLOCK), _F32),
                        pltpu.VMEM((2 * R, 1, Q_BLOCK), _F32),
                        pltpu.VMEM((2, HEAD_DIM, W), _F32),
                        pltpu.VMEM((HEAD_DIM, W), _F32),
                        pltpu.VMEM((seq // SEL_BLOCK, Q_BLOCK), _F32)],
        compiler_params=_params("parallel", "parallel", "arbitrary"),
        name="nsa_attention",
    )(pa, kcmp, vcmp_t, pa, pvt, pa, pvt, pg, bdiag, boff, bfar, bcmp,
      q_gain.reshape(1, HEAD_DIM), k_gain, overlap_t)


def _t5_bucket(dist):
    n = jnp.maximum(dist, 0)
    exact = REL_BUCKETS // 2
    big = exact + (jnp.log(jnp.maximum(n, 1).astype(jnp.float32) / exact)
                   / math.log(REL_MAX_DIST / exact) * (REL_BUCKETS - exact)).astype(jnp.int32)
    return jnp.where(n < exact, n, jnp.minimum(big, REL_BUCKETS - 1))


def nsa_tables(rel_bias, seq):
    G, R = NSA_KV_HEADS, NSA_GROUP
    nqb = seq // Q_BLOCK
    tab_h = rel_bias.T.astype(_F32)
    s = np.arange(Q_BLOCK)[:, None]
    t = np.arange(Q_BLOCK)[None, :]

    def group_tiles(x, lead):
        nl = len(lead)
        rows = x.shape[-2]
        x = x.reshape((G, R) + lead + (rows, Q_BLOCK))
        perm = (0,) + tuple(range(2, 2 + nl)) + (2 + nl, 1, 3 + nl)
        return x.transpose(perm).reshape((G,) + lead + (rows, R * Q_BLOCK))

    bdiag = group_tiles(jnp.take(tab_h, _t5_bucket(jnp.asarray(t - s)), axis=1), ())
    boff = group_tiles(jnp.take(tab_h, _t5_bucket(jnp.asarray(Q_BLOCK + t - s)), axis=1), ())
    bfar = group_tiles(jnp.take(tab_h, _t5_bucket(jnp.full((1, Q_BLOCK), 2 * Q_BLOCK)), axis=1), ())
    lead = (nqb - 1) * Q_BLOCK // CMP_STRIDE
    rows_c = -(-(seq // CMP_STRIDE + lead) // 8) * 8
    m = np.arange(rows_c)[:, None]
    dist_c = t - (CMP_STRIDE * (m - lead) + CMP_BLOCK - 1)
    bcmp = group_tiles(jnp.take(tab_h, _t5_bucket(jnp.asarray(dist_c)), axis=1), ())
    n_cmp = (seq - CMP_BLOCK) // CMP_STRIDE + 1
    n_sel = seq // SEL_BLOCK
    c_start = np.arange(LANE) * CMP_STRIDE
    s_start = np.arange(LANE) * SEL_BLOCK
    overlap = ((c_start[:, None] < s_start[None, :] + SEL_BLOCK)
               & (c_start[:, None] + CMP_BLOCK > s_start[None, :])
               & (np.arange(LANE)[:, None] < n_cmp) & (np.arange(LANE)[None, :] < n_sel))
    return bdiag, boff, bfar, bcmp, jnp.asarray(overlap.T.astype(np.float32)).astype(_MXU)


def _sb_kernel(q_ref, k_ref, vt_ref, o_ref, acc_ref, rest_ref, *, heads):
    QB = Q_BLOCK
    i = pl.program_id(2)
    scale = HEAD_DIM ** -0.5
    row = lax.broadcasted_iota(jnp.int32, (QB, QB), 0)
    col = lax.broadcasted_iota(jnp.int32, (QB, QB), 1)
    later = jnp.where(col > row, 1.0, 0.0).astype(_MXU)
    strict = row < col

    def block(c, first):
        off = pl.multiple_of(c * QB, QB)
        hss = [slice(h * HEAD_DIM, (h + 1) * HEAD_DIM) for h in range(heads)]
        zs = [_dot_nt(k_ref[pl.ds(off, QB), hs], q_ref[:, hs]) * scale for hs in hss]
        log_bs, pieces, totals = [], [], []
        for z in zs:
            log_b, log_1mb = _log_sigmoid_pair(z)
            if first:
                log_1mb = jnp.where(strict, log_1mb, 0.0)
            log_bs.append(log_b)
            pieces.append(_split(log_1mb, 2))
            totals.append(jnp.sum(log_1mb, axis=0, keepdims=True))
        betweens = [_dot(later, p[0]) + _dot(later, p[1]) for p in pieces]
        probs = []
        for h in range(heads):
            if first:
                probs.append(jnp.where(strict, jnp.exp(log_bs[h] + betweens[h]), 0.0).astype(_MXU))
                rest_ref[h] = totals[h]
            else:
                rest = rest_ref[h]
                probs.append(jnp.exp(log_bs[h] + betweens[h] + rest).astype(_MXU))
                rest_ref[h] = rest + totals[h]
        upds = [_dot(vt_ref[hs, pl.ds(off, QB)], probs[h]) for h, hs in enumerate(hss)]
        for h in range(heads):
            acc_ref[h] = upds[h] if first else acc_ref[h] + upds[h]

    block(i, True)

    def body(n, carry):
        block(i - 1 - n, False)
        return carry
    lax.fori_loop(0, i, body, 0)
    for h in range(heads):
        o_ref[:, h * HEAD_DIM:(h + 1) * HEAD_DIM] = acc_ref[h].T.astype(o_ref.dtype)


def stick_breaking(pa, pvt, bsz, seq, heads=SB_HEADS):
    nqb = seq // Q_BLOCK
    w = heads * HEAD_DIM
    qb, kb, vb = _A_BLK[_Q_S] * LANE // w, _A_BLK[_K_S] * LANE // w, _T_BLK[_V_S] * LANE // w
    return pl.pallas_call(
        functools.partial(_sb_kernel, heads=heads),
        grid=(bsz, SB_HEADS // heads, nqb),
        in_specs=[pl.BlockSpec((Q_BLOCK, w), lambda b, h, i: (b * nqb + i, qb + h)),
                  pl.BlockSpec((seq, w), lambda b, h, i: (b, kb + h)),
                  pl.BlockSpec((w, seq), lambda b, h, i: (vb + h, b))],
        out_specs=pl.BlockSpec((Q_BLOCK, w), lambda b, h, i: (b * nqb + i, h)),
        out_shape=jax.ShapeDtypeStruct((bsz * seq, SB_W), _MXU),
        scratch_shapes=[pltpu.VMEM((heads, HEAD_DIM, Q_BLOCK), _F32),
                        pltpu.VMEM((heads, 1, Q_BLOCK), _F32)],
        compiler_params=_params("parallel", "parallel", "arbitrary"),
        name="stick_breaking",
    )(pa, pa, pvt)


def _hgrn_kernel(q_ref, f_ref, i_ref, g_ref, llb_ref, l1lb_ref, omlb_ref, ng_ref, o_ref, state_ref, *, seq, heads):
    C, SUB = HG_CHUNK, HG_SUB
    nsub = C // SUB
    hss = [slice(h * HG_DK, (h + 1) * HG_DK) for h in range(heads)]
    row = lax.broadcasted_iota(jnp.int32, (C, C), 0)
    col = lax.broadcasted_iota(jnp.int32, (C, C), 1)
    upto = jnp.where(col <= row, 1.0, 0.0).astype(_MXU)
    sub_t = lax.broadcasted_iota(jnp.int32, (SUB, 1), 0)
    sub_c = lax.broadcasted_iota(jnp.int32, (SUB, C), 1)
    state_ref[...] = jnp.zeros(state_ref.shape, _F32)

    def chunk(n, carry):
        off = pl.multiple_of(n * C, C)
        fps = [f_ref[pl.ds(off, C), hs] for hs in hss]
        pieces = []
        for fp, hs in zip(fps, hss):
            log_sig, _ = _log_sigmoid_pair(fp)
            y = l1lb_ref[:, hs] + log_sig
            a = llb_ref[:, hs]
            log_f = jnp.maximum(a, y) + jnp.log1p(jnp.exp(-jnp.abs(a - y)))
            pieces.append(_split(log_f, 3))
        bs = [_dot(upto, p[0]) + _dot(upto, p[1]) + _dot(upto, p[2]) for p in pieces]
        qs = [q_ref[pl.ds(off, C), hs].astype(_F32) for hs in hss]
        ks = [omlb_ref[:, hs] * jax.nn.sigmoid(-fp) for fp, hs in zip(fps, hss)]
        vs = [i_ref[pl.ds(off, C), hs].astype(_F32) for hs in hss]

        outs = [_dot_nt((q * jnp.exp(b)).astype(_MXU), state_ref[h].astype(_MXU))
                for h, (q, b) in enumerate(zip(qs, bs))]

        lhs, rhs = [], []
        for q, k, b in zip(qs, ks, bs):
            for s_i in range(1, nsub):
                lo = s_i * SUB
                b_ref = b[lo - 1:lo, :]
                lhs.append((q[lo:lo + SUB, :] * jnp.exp(b[lo:lo + SUB, :] - b_ref)).astype(_MXU))
                rhs.append((k * jnp.exp(jnp.minimum(b_ref - b, 0.0))).astype(_MXU))
        a_blocks = [_dot_nt(x, y) for x, y in zip(lhs, rhs)]
        intra = []
        for h in range(heads):
            rows = [jnp.zeros((SUB, C), _F32)]
            for s_i in range(1, nsub):
                rows.append(jnp.where(sub_c < s_i * SUB, a_blocks[h * (nsub - 1) + s_i - 1], 0.0))
            intra.append(jnp.concatenate(rows, axis=0).astype(_MXU))
        outs = [o + _dot(a, v.astype(_MXU)) for o, a, v in zip(outs, intra, vs)]

        for h, (q, k, v, b) in enumerate(zip(qs, ks, vs, bs)):
            diag = []
            for s_i in range(nsub):
                lo = s_i * SUB
                b_i = b[lo:lo + SUB, :]
                q_i = q[lo:lo + SUB, :]
                o_d = jnp.zeros((SUB, HG_DV), _F32)
                for s in range(SUB):
                    w = jnp.exp(b_i - b[lo + s:lo + s + 1, :])
                    a_col = jnp.sum(q_i * k[lo + s:lo + s + 1, :] * w, axis=-1, keepdims=True)
                    a_col = jnp.where(sub_t >= s, a_col, 0.0)
                    o_d = o_d + a_col * v[lo + s:lo + s + 1, :]
                diag.append(o_d)
            outs[h] = outs[h] + jnp.concatenate(diag, axis=0)

        k_decs = [(k * jnp.exp(b[C - 1:C, :] - b)).astype(_MXU) for k, b in zip(ks, bs)]
        grown = [_dot(v.T.astype(_MXU), kd) for v, kd in zip(vs, k_decs)]
        for h, (b, hs) in enumerate(zip(bs, hss)):
            state_ref[h] = state_ref[h] * jnp.exp(b[C - 1:C, :]) + grown[h]
            gate = jax.nn.sigmoid(g_ref[pl.ds(off, C), hs])
            o_ref[pl.ds(off, C), hs] = (_rms(outs[h], ng_ref[...]) * gate).astype(o_ref.dtype)
        return carry

    lax.fori_loop(0, seq // C, chunk, 0)


def hgrn2(pa, pb, lb, norm_gain, bsz, seq, heads=4):
    lb = lb.reshape(1, HG_KW).astype(_F32)
    log_lb = jnp.log(lb)
    log_1mlb = jnp.log1p(-lb)
    one_mlb = 1.0 - lb
    w = heads * HG_DK

    def seq_spec(blk):
        return pl.BlockSpec((seq, w), lambda b, h, s=blk * LANE // w: (b, s + h))

    head_vec = pl.BlockSpec((1, w), lambda b, h: (0, h))
    return pl.pallas_call(
        functools.partial(_hgrn_kernel, seq=seq, heads=heads),
        grid=(bsz, HG_HEADS // heads),
        in_specs=[seq_spec(_A_BLK[_Q_H]), seq_spec(_B_BLK[_F_H]), seq_spec(_A_BLK[_I_H]), seq_spec(_B_BLK[_G_H]),
                  head_vec, head_vec, head_vec,
                  pl.BlockSpec((1, HG_DV), lambda b, h: (0, 0))],
        out_specs=pl.BlockSpec((seq, w), lambda b, h: (b, h)),
        out_shape=jax.ShapeDtypeStruct((bsz * seq, HG_VW), _MXU),
        scratch_shapes=[pltpu.VMEM((heads, HG_DV, HG_DK), _F32)],
        compiler_params=_params("parallel", "parallel"),
        name="hgrn2",
    )(pa, pb, pa, pb, log_lb, log_1mlb, one_mlb, norm_gain.reshape(1, HG_DV))


def _merge_kernel(o0_ref, o1_ref, o2_ref, w0_ref, w1_ref, w2_ref, g0_ref, g1_ref, g2_ref, out_ref):
    ys = [_dot(o_ref[...], w_ref[0]) for o_ref, w_ref in ((o0_ref, w0_ref), (o1_ref, w1_ref), (o2_ref, w2_ref))]
    acc = jax.nn.sigmoid(g0_ref[...].astype(_F32)) * ys[0]
    acc = acc + jax.nn.sigmoid(g1_ref[...].astype(_F32)) * ys[1]
    acc = acc + jax.nn.sigmoid(g2_ref[...].astype(_F32)) * ys[2]
    out_ref[...] = acc.astype(out_ref.dtype)


def merge_branches(o_nsa, o_sb, o_hg, w_branch, gates, tm=1024, tn=512):
    m = o_nsa.shape[0]
    nj = D_MODEL // tn
    gblk = 0

    def o_spec():
        return pl.BlockSpec((tm, BRANCH_W), lambda i, j: (i, 0))

    def w_spec(n):
        return pl.BlockSpec((1, BRANCH_W, tn), lambda i, j, n=n: (n, 0, j))

    def g_spec(n):
        return pl.BlockSpec((tm, tn), lambda i, j, n=n: (i, gblk + n * nj + j))

    return pl.pallas_call(
        _merge_kernel,
        grid=(m // tm, nj),
        in_specs=[o_spec(), o_spec(), o_spec(), w_spec(0), w_spec(1), w_spec(2), g_spec(0), g_spec(1), g_spec(2)],
        out_specs=pl.BlockSpec((tm, tn), lambda i, j: (i, j)),
        out_shape=jax.ShapeDtypeStruct((m, D_MODEL), _MXU),
        compiler_params=_params("parallel", "arbitrary"),
        name="merge_branches",
    )(o_nsa, o_sb, o_hg, w_branch, w_branch, w_branch, gates, gates, gates)


def _ffn_up_kernel(x_ref, halo_ref, gain_ref, wg_ref, wv_ref, cwg_ref, cwv_ref, cbg_ref, cbv_ref, o_ref, h_ref,
                   *, tiles_per_seq, sub):
    i = pl.program_id(0)

    @pl.when(pl.program_id(1) == 0)
    def _():
        h_ref[...] = _rms(x_ref[...], gain_ref[...]).astype(_MXU)

    first = (i % tiles_per_seq) == 0
    h = h_ref[...]
    halo = _rms(halo_ref[...], gain_ref[...]).astype(_MXU)
    tm = h.shape[0]
    hrows = halo.shape[0]
    row = lax.broadcasted_iota(jnp.int32, (tm, 1), 0)
    cols = [slice(c * sub, (c + 1) * sub) for c in range(wg_ref.shape[1] // sub)]

    prods = [[(_dot(h, w_ref[:, cs]), jnp.where(first, 0.0, _dot(halo, w_ref[:, cs]))) for w_ref in (wg_ref, wv_ref)]
             for cs in cols]

    def conv(u, uh, cw, cb):
        u1 = jnp.where(row == 0, uh[hrows - 1:hrows, :], pltpu.roll(u, 1, axis=0))
        u2 = jnp.where(row == 0, uh[hrows - 2:hrows - 1, :],
                       jnp.where(row == 1, uh[hrows - 1:hrows, :], pltpu.roll(u, 2, axis=0)))
        return cb + u2 * cw[0:1, :] + u1 * cw[1:2, :] + u * cw[2:3, :]

    for cs, ((ug, uhg), (uv, uhv)) in zip(cols, prods):
        gate = conv(ug, uhg, cwg_ref[:, cs], cbg_ref[:, cs])
        val = conv(uv, uhv, cwv_ref[:, cs], cbv_ref[:, cs])
        o_ref[:, cs] = (jax.nn.silu(gate) * val).astype(o_ref.dtype)


def ffn_up(x, gain, w_up, conv_w, conv_b, seq, tm=1024, tn=512, sub=256):
    m, k = x.shape
    nj = D_FF // tn
    hrows = 8
    conv_b = conv_b.reshape(1, 2 * D_FF)
    return pl.pallas_call(
        functools.partial(_ffn_up_kernel, tiles_per_seq=seq // tm, sub=sub),
        grid=(m // tm, nj),
        in_specs=[pl.BlockSpec((tm, k), lambda i, j: (i, 0)),
                  pl.BlockSpec((hrows, k), lambda i, j: (jnp.maximum(i * (tm // hrows) - 1, 0), 0)),
                  pl.BlockSpec((1, k), lambda i, j: (0, 0)),
                  pl.BlockSpec((k, tn), lambda i, j: (0, j)),
                  pl.BlockSpec((k, tn), lambda i, j: (0, nj + j)),
                  pl.BlockSpec((CONV_W, tn), lambda i, j: (0, j)),
                  pl.BlockSpec((CONV_W, tn), lambda i, j: (0, nj + j)),
                  pl.BlockSpec((1, tn), lambda i, j: (0, j)),
                  pl.BlockSpec((1, tn), lambda i, j: (0, nj + j))],
        out_specs=pl.BlockSpec((tm, tn), lambda i, j: (i, j)),
        out_shape=jax.ShapeDtypeStruct((m, D_FF), _MXU),
        scratch_shapes=[pltpu.VMEM((tm, k), _MXU)],
        compiler_params=_params("parallel", "arbitrary"),
        name="ffn_up_conv",
    )(x, x, gain.reshape(1, k), w_up, w_up, conv_w, conv_w, conv_b, conv_b)


def _gather_cols(w, segs):
    return jnp.concatenate([w[:, _SPLIT_OFF[s]:_SPLIT_OFF[s + 1]] for s in segs], axis=1)


def _nsa_gate_cols(w):
    g = w[:, _SPLIT_OFF[_G_N]:_SPLIT_OFF[_G_N + 1]].reshape(-1, NSA_KV_HEADS, NSA_GROUP * 3)
    g = jnp.pad(g, ((0, 0), (0, 0), (0, LANE - NSA_GROUP * 3)))
    return g.reshape(-1, NSA_KV_HEADS * LANE)


def kernel(x, norm_attn, w_in, nsa_q_gain, nsa_k_gain, cmp_pos, cmp_w1, cmp_w2, rel_bias, hg_lower_bound,
           hg_norm_gain, w_branch, w_out, norm_ffn, w_up, conv_w, conv_b, w_down):
    bsz, seq, d = x.shape
    m = bsz * seq
    lb_all = jnp.cumsum(jax.nn.softmax(hg_lower_bound.astype(_F32), axis=0), axis=0)
    lb_all = jnp.maximum(lb_all - lb_all[0:1], 0.0)
    tables = nsa_tables(rel_bias, seq)
    xf = x.reshape(m, d)
    for l in range(w_in.shape[0]):
        wl = w_in[l]
        h = rmsnorm_cast(xf, norm_attn[l])
        pa = matmul(h, _gather_cols(wl, _A_ORDER).astype(_MXU), _MXU, tm=512, tn=A_COLS // 2, name="proj_a")
        pb = matmul(h, _gather_cols(wl, _B_ORDER).astype(_MXU), _F32, tn=1024, name="proj_b")
        gates = matmul(h, _gather_cols(wl, (_GATES,)).astype(_MXU), _MXU, tn=1024, name="proj_gates")
        pkc = matmul(h, _gather_cols(wl, (_KC,)).astype(_MXU), _MXU, name="proj_kc")
        pvc = matmul(h, _gather_cols(wl, (_VC,)).astype(_MXU), _MXU, name="proj_vc")
        pg = matmul(h, _nsa_gate_cols(wl).astype(_MXU), _F32, name="proj_g")
        blk_rows = seq // CMP_STRIDE
        kcmp = compress(pkc.reshape(bsz, blk_rows, CMP_STRIDE * NSA_KV), cmp_w1[l, 0], cmp_w2[l, 0], cmp_pos[l, 0],
                        nsa_k_gain[l, 0], True)
        vcmp_t = compress(pvc.reshape(bsz, blk_rows, CMP_STRIDE * NSA_KV), cmp_w1[l, 1], cmp_w2[l, 1], cmp_pos[l, 1],
                          nsa_k_gain[l, 0], False)
        pvt = matmul_nt(_gather_cols(wl, _T_ORDER).T.astype(_MXU), h, _MXU, name="proj_vt")
        o_nsa = nsa_attention(pa, pvt, kcmp, vcmp_t, pg, tables, nsa_q_gain[l], nsa_k_gain[l], bsz, seq)
        o_sb = stick_breaking(pa, pvt, bsz, seq)
        o_hg = hgrn2(pa, pb, lb_all[l], hg_norm_gain[l], bsz, seq)
        merged = merge_branches(o_nsa, o_sb, o_hg, w_branch[l].astype(_MXU), gates)
        xf = matmul(merged, w_out[l].astype(_MXU), _F32, residual=xf, tn=1024, name="out_proj")
        act = ffn_up(xf, norm_ffn[l], w_up[l].astype(_MXU), conv_w[l], conv_b[l], seq)
        xf = matmul(act, w_down[l].astype(_MXU), _F32, residual=xf, tm=512, tn=1024, name="ffn_down")
    return xf.reshape(bsz, seq, d)
```

```python
import functools
import math

import numpy as np
import jax
import jax.numpy as jnp
from jax import lax
from jax.experimental import pallas as pl
from jax.experimental.pallas import tpu as pltpu

D_MODEL = 2048
DEPTH = 4
HEAD_DIM = 128
BRANCH_W = D_MODEL // 2
N_BRANCH = 3
NSA_HEADS = BRANCH_W // HEAD_DIM
NSA_KV_HEADS = NSA_HEADS // 4
NSA_GROUP = NSA_HEADS // NSA_KV_HEADS
CMP_BLOCK = 32
CMP_STRIDE = 16
SEL_BLOCK = 64
SEL_TOPK = 16
WINDOW = 512
Q_BLOCK = 128
SB_HEADS = BRANCH_W // HEAD_DIM
HG_DK = 128
HG_DV = 128
HG_HEADS = BRANCH_W // HG_DV
HG_CHUNK = 64
HG_SUB = 8
D_FF = 256 * ((8 * D_MODEL // 3 + 255) // 256)
CONV_W = 3
REL_BUCKETS = 32
REL_MAX_DIST = 128
EPS = 1e-6
NEG = -1e30
TINY = 1e-30

NSA_Q = NSA_HEADS * HEAD_DIM
NSA_KV = NSA_KV_HEADS * HEAD_DIM
NSA_GATE = 3 * NSA_HEADS
SB_W = SB_HEADS * HEAD_DIM
HG_KW = HG_HEADS * HG_DK
HG_VW = HG_HEADS * HG_DV
SPLIT_SIZES = (NSA_Q,) + (NSA_KV,) * 6 + (NSA_GATE,) + (SB_W,) * 3 + (HG_KW, HG_KW, HG_VW, HG_VW, N_BRANCH * D_MODEL)
(_Q_N, _KC, _VC, _KS, _VS, _KW, _VW, _G_N, _Q_S, _K_S, _V_S, _Q_H, _F_H, _I_H, _G_H, _GATES) = range(16)
_SPLIT_OFF = np.concatenate([[0], np.cumsum(SPLIT_SIZES)]).tolist()

LANE = 128
_A_ORDER = (_K_S, _Q_H, _I_H, _KS, _KW)
_A_BLK = {}
_off = 0
for _s in _A_ORDER:
    _A_BLK[_s] = _off // LANE
    _off += SPLIT_SIZES[_s]
A_COLS = _off
_T_ORDER = (_V_S, _Q_S, _Q_N, _VS, _VW)
_T_BLK = {}
_off = 0
for _s in _T_ORDER:
    _T_BLK[_s] = _off // LANE
    _off += SPLIT_SIZES[_s]
T_ROWS = _off
_B_ORDER = (_F_H, _G_H)
_B_BLK = {}
_off = 0
for _s in _B_ORDER:
    _B_BLK[_s] = _off // LANE
    _off += SPLIT_SIZES[_s]
B_COLS = _off

VMEM_LIMIT = 48 * 1024 * 1024

_MXU = jnp.bfloat16
_F32 = jnp.float32


def _dot(a, b):
    return jnp.dot(a, b, preferred_element_type=_F32)


def _dot_nt(a, b):
    return lax.dot_general(a, b, (((1,), (1,)), ((), ())), preferred_element_type=_F32)


def _split(x, parts):
    out = []
    r = x
    for _ in range(parts):
        h = r.astype(_MXU)
        out.append(h)
        r = r - h.astype(_F32)
    return out


def _params(*sem):
    return pltpu.CompilerParams(dimension_semantics=sem, vmem_limit_bytes=VMEM_LIMIT)


def _rms(x, gain):
    return x * lax.rsqrt(jnp.mean(x * x, axis=-1, keepdims=True) + EPS) * gain


def _log_sigmoid_pair(z):
    lp = jnp.log(1.0 + jnp.exp(-jnp.abs(z)))
    return jnp.minimum(z, 0.0) - lp, jnp.minimum(-z, 0.0) - lp


def _rmsnorm_kernel(x_ref, g_ref, o_ref):
    o_ref[...] = _rms(x_ref[...], g_ref[...]).astype(o_ref.dtype)


def rmsnorm_cast(x2d, gain):
    m, d = x2d.shape
    tm = 512
    return pl.pallas_call(
        _rmsnorm_kernel,
        grid=(m // tm,),
        in_specs=[pl.BlockSpec((tm, d), lambda i: (i, 0)),
                  pl.BlockSpec((1, d), lambda i: (0, 0))],
        out_specs=pl.BlockSpec((tm, d), lambda i: (i, 0)),
        out_shape=jax.ShapeDtypeStruct((m, d), _MXU),
        compiler_params=_params("parallel"),
        name="rmsnorm_cast",
    )(x2d, gain.reshape(1, d))


def _mm_kernel(a_ref, w_ref, o_ref):
    o_ref[...] = _dot(a_ref[...], w_ref[...]).astype(o_ref.dtype)


def _mm_res_kernel(a_ref, w_ref, r_ref, o_ref):
    o_ref[...] = (r_ref[...] + _dot(a_ref[...], w_ref[...])).astype(o_ref.dtype)


def matmul(a, w, out_dtype, residual=None, tm=1024, tn=512, name="matmul"):
    m, k = a.shape
    n = w.shape[1]
    tm = min(tm, m)
    tn = min(tn, n)
    in_specs = [pl.BlockSpec((tm, k), lambda i, j: (i, 0)),
                pl.BlockSpec((k, tn), lambda i, j: (0, j))]
    args = [a, w]
    kern = _mm_kernel
    if residual is not None:
        in_specs.append(pl.BlockSpec((tm, tn), lambda i, j: (i, j)))
        args.append(residual)
        kern = _mm_res_kernel
    return pl.pallas_call(
        kern,
        grid=(m // tm, n // tn),
        in_specs=in_specs,
        out_specs=pl.BlockSpec((tm, tn), lambda i, j: (i, j)),
        out_shape=jax.ShapeDtypeStruct((m, n), out_dtype),
        compiler_params=_params("parallel", "arbitrary"),
        name=name,
    )(*args)


def _mm_nt_kernel(wt_ref, a_ref, o_ref):
    o_ref[...] = _dot_nt(wt_ref[...], a_ref[...]).astype(o_ref.dtype)


def matmul_nt(wt, a, out_dtype, tm=1024, tn=512, name="matmul_nt"):
    n, k = wt.shape
    m = a.shape[0]
    tm = min(tm, m)
    tn = min(tn, n)
    return pl.pallas_call(
        _mm_nt_kernel,
        grid=(m // tm, n // tn),
        in_specs=[pl.BlockSpec((tn, k), lambda i, j: (j, 0)),
                  pl.BlockSpec((tm, k), lambda i, j: (i, 0))],
        out_specs=pl.BlockSpec((tn, tm), lambda i, j: (j, i)),
        out_shape=jax.ShapeDtypeStruct((n, m), out_dtype),
        compiler_params=_params("parallel", "arbitrary"),
        name=name,
    )(wt, a)


def _compress_kernel(x_ref, wa_ref, wb_ref, pa_ref, pb_ref, w2_ref, gain_ref, o_ref, *, normalize):
    x = x_ref[0].astype(_F32)
    u = _dot((x + pa_ref[...]).astype(_MXU), wa_ref[...])
    v = _dot((x + pb_ref[...]).astype(_MXU), wb_ref[...])
    hid = jax.nn.gelu(u + pltpu.roll(v, v.shape[0] - 1, axis=0)).astype(_MXU)
    if normalize:
        out = _dot(hid, w2_ref[...])
        out = jnp.concatenate(
            [_rms(out[:, g * HEAD_DIM:(g + 1) * HEAD_DIM], gain_ref[...]) for g in range(NSA_KV_HEADS)], axis=1)
    else:
        out = _dot_nt(w2_ref[...], hid)
    o_ref[0] = out.astype(o_ref.dtype)


def compress(x, w1, w2, pos, gain, normalize):
    bsz = x.shape[0]
    nrow = x.shape[1]
    kdim = x.shape[2]
    half = CMP_BLOCK // 2
    eye = jnp.eye(NSA_KV_HEADS, dtype=w1.dtype)

    def embed(w):
        return jnp.einsum('lde,gh->lgdhe', w, eye).reshape(kdim, NSA_KV).astype(_MXU)

    def tile_pos(p):
        return jnp.broadcast_to(p[:, None, :], (half, NSA_KV_HEADS, HEAD_DIM)).reshape(1, kdim)

    w2b = jnp.einsum('de,gh->gdhe', w2, eye).reshape(NSA_KV, NSA_KV).astype(_MXU)
    out_block = (1, nrow, NSA_KV)
    if not normalize:
        w2b = w2b.T
        out_block = (1, NSA_KV, nrow)
    return pl.pallas_call(
        functools.partial(_compress_kernel, normalize=normalize),
        grid=(bsz,),
        in_specs=[pl.BlockSpec((1, nrow, kdim), lambda b: (b, 0, 0)),
                  pl.BlockSpec((kdim, NSA_KV), lambda b: (0, 0)),
                  pl.BlockSpec((kdim, NSA_KV), lambda b: (0, 0)),
                  pl.BlockSpec((1, kdim), lambda b: (0, 0)),
                  pl.BlockSpec((1, kdim), lambda b: (0, 0)),
                  pl.BlockSpec((NSA_KV, NSA_KV), lambda b: (0, 0)),
                  pl.BlockSpec((1, HEAD_DIM), lambda b: (0, 0))],
        out_specs=pl.BlockSpec(out_block, lambda b: (b, 0, 0)),
        out_shape=jax.ShapeDtypeStruct((bsz,) + out_block[1:], _MXU),
        compiler_params=_params("parallel"),
        name="nsa_compress",
    )(x, embed(w1[:half]), embed(w1[half:]), tile_pos(pos[:half]), tile_pos(pos[half:]), w2b,
      gain.reshape(1, HEAD_DIM))


def _nsa_kernel(q_ref, kc_ref, vct_ref, ks_ref, vst_ref, kw_ref, vwt_ref, gate_ref,
                bdiag_ref, boff_ref, bfar_ref, bcmp_ref, qg_ref, kg_ref, ovt_ref,
                o_ref, ksn_ref, kwn_ref, m_ref, l_ref, acc_ref, ocmp_ref, sel_ref, *, seq):
    R = NSA_GROUP
    QB = Q_BLOCK
    W = R * QB
    i = pl.program_id(2)
    scale = HEAD_DIM ** -0.5
    sel_shift = int(math.log2(SEL_BLOCK))

    @pl.when(i == 0)
    def _():
        def body(c, carry):
            off = pl.multiple_of(c * QB, QB)
            ksn_ref[pl.ds(off, QB), :] = _rms(ks_ref[pl.ds(off, QB), :].astype(_F32), kg_ref[1:2, :]).astype(_MXU)
            kwn_ref[pl.ds(off, QB), :] = _rms(kw_ref[pl.ds(off, QB), :].astype(_F32), kg_ref[2:3, :]).astype(_MXU)
            return carry
        lax.fori_loop(0, seq // QB, body, 0)

    heads = [slice(r * QB, (r + 1) * QB) for r in range(R)]

    def norm_t(x):
        return x * lax.rsqrt(jnp.mean(x * x, axis=0, keepdims=True) + EPS) * qg_ref[...]

    q_all = jnp.concatenate(
        [norm_t(q_ref[r * HEAD_DIM:(r + 1) * HEAD_DIM, :].astype(_F32)).astype(_MXU) for r in range(R)], axis=1)

    row = lax.broadcasted_iota(jnp.int32, (QB, QB), 0)
    col = lax.broadcasted_iota(jnp.int32, (QB, QB), 1)

    m_ref[...] = jnp.full(m_ref.shape, NEG, _F32)
    l_ref[...] = jnp.zeros(l_ref.shape, _F32)
    acc_ref[...] = jnp.zeros(acc_ref.shape, _F32)

    def attend_pre(jobs):
        staged = []
        for branch, s_all, mask, vt in jobs:
            p_parts, alphas = [], []
            for r, hs in enumerate(heads):
                slot = branch * R + r
                s = s_all[:, hs]
                if mask is not None:
                    s = jnp.where(mask, s, NEG)
                m_old = m_ref[slot]
                m_new = jnp.maximum(m_old, jnp.max(s, axis=0, keepdims=True))
                alpha = jnp.exp(m_old - m_new)
                p = jnp.exp(s - m_new)
                if mask is not None:
                    p = jnp.where(mask, p, 0.0)
                l_ref[slot] = alpha * l_ref[slot] + jnp.sum(p, axis=0, keepdims=True)
                m_ref[slot] = m_new
                p_parts.append(p.astype(_MXU))
                alphas.append(alpha)
            staged.append((branch, vt, jnp.concatenate(p_parts, axis=1), alphas))
        return staged

    def attend_post(staged):
        upds = [_dot(vt, p_all) for _, vt, p_all, _ in staged]
        for (branch, _, _, alphas), upd in zip(staged, upds):
            for r, hs in enumerate(heads):
                acc_ref[branch, :, hs] = alphas[r] * acc_ref[branch, :, hs] + upd[:, hs]

    def attend(jobs):
        attend_post(attend_pre(jobs))

    valid_c = i * QB + col >= CMP_STRIDE * row + (CMP_BLOCK - 1)
    shift = QB // CMP_STRIDE
    bias_c = bcmp_ref[0, pl.ds(pl.multiple_of((seq // QB - 1 - i) * shift, shift), seq // CMP_STRIDE), :]
    s_c = _dot(kc_ref[0], q_all) * scale + bias_c

    nwb = WINDOW // QB
    rows_w = (nwb - 1) * QB
    start_w = jnp.maximum(i - nwb, 0)
    off_w = pl.multiple_of(start_w * QB, QB)
    key_w = start_w * QB + lax.broadcasted_iota(jnp.int32, (rows_w, QB), 0)
    qry_w = i * QB + lax.broadcasted_iota(jnp.int32, (rows_w, QB), 1)
    mask_w = (key_w > qry_w - WINDOW) & (key_w < (i - 1) * QB)
    s_w = _dot(kwn_ref[pl.ds(off_w, rows_w), :], q_all) * scale + bfar_ref[0]

    p_parts = []
    psum = jnp.zeros((QB, QB), _F32)
    for hs in heads:
        s = jnp.where(valid_c, s_c[:, hs], NEG)
        e = jnp.where(valid_c, jnp.exp(s - jnp.max(s, axis=0, keepdims=True)), 0.0)
        p = e * (1.0 / jnp.maximum(jnp.sum(e, axis=0, keepdims=True), TINY))
        psum = psum + p
        p_parts.append(p.astype(_MXU))
    staged_w = attend_pre([(1, s_w, mask_w, vwt_ref[:, pl.ds(off_w, rows_w)])])
    ocmp_ref[...] = _dot(vct_ref[0], jnp.concatenate(p_parts, axis=1))
    ovt = ovt_ref[...]
    imp = functools.reduce(lambda a, b: a + b, [_dot(ovt, h) for h in _split(psum, 2)])
    attend_post(staged_w)

    n_sel = seq // SEL_BLOCK
    jrow = lax.broadcasted_iota(jnp.int32, (n_sel, QB), 0)
    tcol = lax.broadcasted_iota(jnp.int32, (n_sel, QB), 1)
    qblk = lax.shift_right_arithmetic(i * QB + tcol, sel_shift)
    causal_b = jrow <= qblk
    forced = causal_b & ((jrow == 0) | (jrow >= qblk - 1))
    score = jnp.where(forced, jnp.inf, jnp.where(causal_b, imp[:n_sel, :], -jnp.inf))
    rank = jnp.zeros((n_sel, QB), jnp.int32)
    for jj in range(n_sel):
        sc = score[jj:jj + 1, :]
        beats = (sc > score) | ((sc == score) & (jrow > jj))
        rank = rank + beats.astype(jnp.int32)
    sel_ref[...] = jnp.where((rank < SEL_TOPK) & causal_b, 1.0, 0.0)

    def scores(k_ref, off, rows, bias):
        return _dot(k_ref[pl.ds(off, rows), :], q_all) * scale + bias

    def sel_mask(c, nblk):
        rix = lax.broadcasted_iota(jnp.int32, (nblk * QB, QB), 0)
        per_key_block = QB // SEL_BLOCK
        flags = [sel_ref[pl.ds(per_key_block * c + j, 1), :] for j in range(per_key_block * nblk)]
        m = flags[-1]
        for j in reversed(range(len(flags) - 1)):
            m = jnp.where(rix < (j + 1) * SEL_BLOCK, flags[j], m)
        return m > 0.5

    lower = row <= col
    bfar, boff, bdiag = bfar_ref[0], boff_ref[0], bdiag_ref[0]

    n_far = jnp.maximum(i - 1, 0)
    wide = 4

    def far_tile(c, nblk):
        off = pl.multiple_of(c * QB, QB)
        attend([(0, scores(ksn_ref, off, nblk * QB, bfar), sel_mask(c, nblk), vst_ref[:, pl.ds(off, nblk * QB)])])

    def far_body(p, carry):
        far_tile(p * wide, wide)
        return carry
    lax.fori_loop(0, n_far // wide, far_body, 0)
    rem = n_far % wide

    @pl.when(rem >= 2)
    def _():
        far_tile(n_far - rem, 2)

    @pl.when(rem % 2 == 1)
    def _():
        far_tile(n_far - 1, 1)

    @pl.when(i >= 1)
    def _():
        off = pl.multiple_of((i - 1) * QB, QB)
        bias = jnp.concatenate([boff, bdiag], axis=0)
        after_first = lax.broadcasted_iota(jnp.int32, (2 * QB, QB), 0) - QB
        tcol = lax.broadcasted_iota(jnp.int32, (2 * QB, QB), 1)
        causal = after_first <= tcol
        s_sel = scores(ksn_ref, off, 2 * QB, bias)
        s_win = scores(kwn_ref, off, 2 * QB, bias)
        attend([(0, s_sel, sel_mask(i - 1, 2) & causal, vst_ref[:, pl.ds(off, 2 * QB)]),
                (1, s_win, causal, vwt_ref[:, pl.ds(off, 2 * QB)])])

    @pl.when(i == 0)
    def _():
        s_sel = scores(ksn_ref, 0, QB, bdiag)
        s_win = scores(kwn_ref, 0, QB, bdiag)
        attend([(0, s_sel, sel_mask(0, 1) & lower, vst_ref[:, 0:QB]),
                (1, s_win, lower, vwt_ref[:, 0:QB])])

    gate_t = jax.nn.sigmoid(gate_ref[...].T[:4 * R, :])
    for r, hs in enumerate(heads):
        g_cmp = gate_t[3 * r:3 * r + 1, :]
        g_sel = gate_t[3 * r + 1:3 * r + 2, :] * (1.0 / jnp.maximum(l_ref[r], TINY))
        g_win = gate_t[3 * r + 2:3 * r + 3, :] * (1.0 / jnp.maximum(l_ref[R + r], TINY))
        o_t = g_cmp * ocmp_ref[:, hs] + g_sel * acc_ref[0, :, hs] + g_win * acc_ref[1, :, hs]
        o_ref[:, hs] = o_t.T.astype(o_ref.dtype)


def nsa_attention(pa, pvt, kcmp, vcmp_t, pg, tables, q_gain, k_gain, bsz, seq):
    nqb = seq // Q_BLOCK
    G, R = NSA_KV_HEADS, NSA_GROUP
    W = R * Q_BLOCK
    bdiag, boff, bfar, bcmp, overlap_t = tables
    n_cmp_rows = seq // CMP_STRIDE

    def k_spec(seg):
        return pl.BlockSpec((seq, HEAD_DIM), lambda b, g, i, s=_A_BLK[seg]: (b, s + g))

    def vt_spec(seg):
        return pl.BlockSpec((HEAD_DIM, seq), lambda b, g, i, s=_T_BLK[seg]: (s + g, b))

    group_tile = pl.BlockSpec((1, Q_BLOCK, W), lambda b, g, i: (g, 0, 0))
    return pl.pallas_call(
        functools.partial(_nsa_kernel, seq=seq),
        grid=(bsz, G, nqb),
        in_specs=[pl.BlockSpec((R * HEAD_DIM, Q_BLOCK),
                               lambda b, g, i, s=_T_BLK[_Q_N] // R: (s + g, b * nqb + i)),
                  pl.BlockSpec((1, n_cmp_rows, HEAD_DIM), lambda b, g, i: (b, 0, g)),
                  pl.BlockSpec((1, HEAD_DIM, n_cmp_rows), lambda b, g, i: (b, g, 0)),
                  k_spec(_KS), vt_spec(_VS), k_spec(_KW), vt_spec(_VW),
                  pl.BlockSpec((Q_BLOCK, LANE), lambda b, g, i: (b * nqb + i, g)),
                  group_tile, group_tile,
                  pl.BlockSpec((1, 1, W), lambda b, g, i: (g, 0, 0)),
                  pl.BlockSpec((1,) + bcmp.shape[1:], lambda b, g, i: (g, 0, 0)),
                  pl.BlockSpec((HEAD_DIM, Q_BLOCK), lambda b, g, i: (0, 0)),
                  pl.BlockSpec((3, HEAD_DIM), lambda b, g, i: (0, 0)),
                  pl.BlockSpec((LANE, LANE), lambda b, g, i: (0, 0))],
        out_specs=pl.BlockSpec((Q_BLOCK, W), lambda b, g, i: (b * nqb + i, g)),
        out_shape=jax.ShapeDtypeStruct((bsz * seq, NSA_Q), _MXU),
        scratch_shapes=[pltpu.VMEM((seq, HEAD_DIM), _MXU),
                        pltpu.VMEM((seq, HEAD_DIM), _MXU),
                        pltpu.VMEM((2 * R, 1, Q_BLOCK), _F32),
                        pltpu.VMEM((2 * R, 1, Q_BLOCK), _F32),
                        pltpu.VMEM((2, HEAD_DIM, W), _F32),
                        pltpu.VMEM((HEAD_DIM, W), _F32),
                        pltpu.VMEM((seq // SEL_BLOCK, Q_BLOCK), _F32)],
        compiler_params=_params("parallel", "parallel", "arbitrary"),
        name="nsa_attention",
    )(pvt, kcmp, vcmp_t, pa, pvt, pa, pvt, pg, bdiag, boff, bfar, bcmp,
      jnp.broadcast_to(q_gain.reshape(HEAD_DIM, 1), (HEAD_DIM, Q_BLOCK)), k_gain, overlap_t)


def _t5_bucket(dist):
    n = jnp.maximum(dist, 0)
    exact = REL_BUCKETS // 2
    big = exact + (jnp.log(jnp.maximum(n, 1).astype(jnp.float32) / exact)
                   / math.log(REL_MAX_DIST / exact) * (REL_BUCKETS - exact)).astype(jnp.int32)
    return jnp.where(n < exact, n, jnp.minimum(big, REL_BUCKETS - 1))


def nsa_tables(rel_bias, seq):
    G, R = NSA_KV_HEADS, NSA_GROUP
    nqb = seq // Q_BLOCK
    tab_h = rel_bias.T.astype(_F32)
    s = np.arange(Q_BLOCK)[:, None]
    t = np.arange(Q_BLOCK)[None, :]

    def group_tiles(x, lead):
        nl = len(lead)
        rows = x.shape[-2]
        x = x.reshape((G, R) + lead + (rows, Q_BLOCK))
        perm = (0,) + tuple(range(2, 2 + nl)) + (2 + nl, 1, 3 + nl)
        return x.transpose(perm).reshape((G,) + lead + (rows, R * Q_BLOCK))

    bdiag = group_tiles(jnp.take(tab_h, _t5_bucket(jnp.asarray(t - s)), axis=1), ())
    boff = group_tiles(jnp.take(tab_h, _t5_bucket(jnp.asarray(Q_BLOCK + t - s)), axis=1), ())
    bfar = group_tiles(jnp.take(tab_h, _t5_bucket(jnp.full((1, Q_BLOCK), 2 * Q_BLOCK)), axis=1), ())
    lead = (nqb - 1) * Q_BLOCK // CMP_STRIDE
    rows_c = -(-(seq // CMP_STRIDE + lead) // 8) * 8
    m = np.arange(rows_c)[:, None]
    dist_c = t - (CMP_STRIDE * (m - lead) + CMP_BLOCK - 1)
    bcmp = group_tiles(jnp.take(tab_h, _t5_bucket(jnp.asarray(dist_c)), axis=1), ())
    n_cmp = (seq - CMP_BLOCK) // CMP_STRIDE + 1
    n_sel = seq // SEL_BLOCK
    c_start = np.arange(LANE) * CMP_STRIDE
    s_start = np.arange(LANE) * SEL_BLOCK
    overlap = ((c_start[:, None] < s_start[None, :] + SEL_BLOCK)
               & (c_start[:, None] + CMP_BLOCK > s_start[None, :])
               & (np.arange(LANE)[:, None] < n_cmp) & (np.arange(LANE)[None, :] < n_sel))
    return bdiag, boff, bfar, bcmp, jnp.asarray(overlap.T.astype(np.float32)).astype(_MXU)


def _sb_kernel(q_ref, k_ref, vt_ref, o_ref, acc_ref, rest_ref, *, heads):
    QB = Q_BLOCK
    i = pl.program_id(2)
    scale = HEAD_DIM ** -0.5
    row = lax.broadcasted_iota(jnp.int32, (QB, QB), 0)
    col = lax.broadcasted_iota(jnp.int32, (QB, QB), 1)
    later = jnp.where(col > row, 1.0, 0.0).astype(_MXU)
    strict = row < col

    def block(c, first):
        off = pl.multiple_of(c * QB, QB)
        hss = [slice(h * HEAD_DIM, (h + 1) * HEAD_DIM) for h in range(heads)]
        zs = [_dot(k_ref[pl.ds(off, QB), hs], q_ref[hs, :]) * scale for hs in hss]
        log_bs, pieces, totals = [], [], []
        for z in zs:
            log_b, log_1mb = _log_sigmoid_pair(z)
            if first:
                log_1mb = jnp.where(strict, log_1mb, 0.0)
            log_bs.append(log_b)
            pieces.append(_split(log_1mb, 2))
            totals.append(jnp.sum(log_1mb, axis=0, keepdims=True))
        betweens = [_dot(later, p[0]) + _dot(later, p[1]) for p in pieces]
        probs = []
        for h in range(heads):
            if first:
                probs.append(jnp.where(strict, jnp.exp(log_bs[h] + betweens[h]), 0.0).astype(_MXU))
                rest_ref[h] = totals[h]
            else:
                rest = rest_ref[h]
                probs.append(jnp.exp(log_bs[h] + betweens[h] + rest).astype(_MXU))
                rest_ref[h] = rest + totals[h]
        upds = [_dot(vt_ref[hs, pl.ds(off, QB)], probs[h]) for h, hs in enumerate(hss)]
        for h in range(heads):
            acc_ref[h] = upds[h] if first else acc_ref[h] + upds[h]

    block(i, True)

    def body(n, carry):
        block(i - 1 - n, False)
        return carry
    lax.fori_loop(0, i, body, 0)
    for h in range(heads):
        o_ref[:, h * HEAD_DIM:(h + 1) * HEAD_DIM] = acc_ref[h].T.astype(o_ref.dtype)


def stick_breaking(pa, pvt, bsz, seq, heads=SB_HEADS):
    nqb = seq // Q_BLOCK
    w = heads * HEAD_DIM
    qb, kb, vb = _T_BLK[_Q_S] * LANE // w, _A_BLK[_K_S] * LANE // w, _T_BLK[_V_S] * LANE // w
    return pl.pallas_call(
        functools.partial(_sb_kernel, heads=heads),
        grid=(bsz, SB_HEADS // heads, nqb),
        in_specs=[pl.BlockSpec((w, Q_BLOCK), lambda b, h, i: (qb + h, b * nqb + i)),
                  pl.BlockSpec((seq, w), lambda b, h, i: (b, kb + h)),
                  pl.BlockSpec((w, seq), lambda b, h, i: (vb + h, b))],
        out_specs=pl.BlockSpec((Q_BLOCK, w), lambda b, h, i: (b * nqb + i, h)),
        out_shape=jax.ShapeDtypeStruct((bsz * seq, SB_W), _MXU),
        scratch_shapes=[pltpu.VMEM((heads, HEAD_DIM, Q_BLOCK), _F32),
                        pltpu.VMEM((heads, 1, Q_BLOCK), _F32)],
        compiler_params=_params("parallel", "parallel", "arbitrary"),
        name="stick_breaking",
    )(pvt, pa, pvt)


def _hgrn_kernel(q_ref, f_ref, i_ref, g_ref, llb_ref, l1lb_ref, omlb_ref, ng_ref, o_ref, state_ref, *, seq, heads):
    C, SUB = HG_CHUNK, HG_SUB
    nsub = C // SUB
    hss = [slice(h * HG_DK, (h + 1) * HG_DK) for h in range(heads)]
    row = lax.broadcasted_iota(jnp.int32, (C, C), 0)
    col = lax.broadcasted_iota(jnp.int32, (C, C), 1)
    upto = jnp.where(col <= row, 1.0, 0.0).astype(_MXU)
    sub_t = lax.broadcasted_iota(jnp.int32, (SUB, 1), 0)
    sub_c = lax.broadcasted_iota(jnp.int32, (SUB, C), 1)
    state_ref[...] = jnp.zeros(state_ref.shape, _F32)

    def chunk(n, carry):
        off = pl.multiple_of(n * C, C)
        fps = [f_ref[pl.ds(off, C), hs] for hs in hss]
        pieces = []
        for fp, hs in zip(fps, hss):
            log_sig, _ = _log_sigmoid_pair(fp)
            y = l1lb_ref[:, hs] + log_sig
            a = llb_ref[:, hs]
            log_f = jnp.maximum(a, y) + jnp.log1p(jnp.exp(-jnp.abs(a - y)))
            pieces.append(_split(log_f, 3))
        bs = [_dot(upto, p[0]) + _dot(upto, p[1]) + _dot(upto, p[2]) for p in pieces]
        qs = [q_ref[pl.ds(off, C), hs].astype(_F32) for hs in hss]
        ks = [omlb_ref[:, hs] * jax.nn.sigmoid(-fp) for fp, hs in zip(fps, hss)]
        vs = [i_ref[pl.ds(off, C), hs].astype(_F32) for hs in hss]

        outs = [_dot_nt((q * jnp.exp(b)).astype(_MXU), state_ref[h].astype(_MXU))
                for h, (q, b) in enumerate(zip(qs, bs))]

        lhs, rhs = [], []
        for q, k, b in zip(qs, ks, bs):
            for s_i in range(1, nsub):
                lo = s_i * SUB
                b_ref = b[lo - 1:lo, :]
                lhs.append((q[lo:lo + SUB, :] * jnp.exp(b[lo:lo + SUB, :] - b_ref)).astype(_MXU))
                rhs.append((k * jnp.exp(jnp.minimum(b_ref - b, 0.0))).astype(_MXU))
        a_blocks = [_dot_nt(x, y) for x, y in zip(lhs, rhs)]
        intra = []
        for h in range(heads):
            rows = [jnp.zeros((SUB, C), _F32)]
            for s_i in range(1, nsub):
                rows.append(jnp.where(sub_c < s_i * SUB, a_blocks[h * (nsub - 1) + s_i - 1], 0.0))
            intra.append(jnp.concatenate(rows, axis=0).astype(_MXU))
        outs = [o + _dot(a, v.astype(_MXU)) for o, a, v in zip(outs, intra, vs)]

        for h, (q, k, v, b) in enumerate(zip(qs, ks, vs, bs)):
            diag = []
            for s_i in range(nsub):
                lo = s_i * SUB
                b_i = b[lo:lo + SUB, :]
                q_i = q[lo:lo + SUB, :]
                o_d = jnp.zeros((SUB, HG_DV), _F32)
                for s in range(SUB):
                    w = jnp.exp(b_i - b[lo + s:lo + s + 1, :])
                    a_col = jnp.sum(q_i * k[lo + s:lo + s + 1, :] * w, axis=-1, keepdims=True)
                    a_col = jnp.where(sub_t >= s, a_col, 0.0)
                    o_d = o_d + a_col * v[lo + s:lo + s + 1, :]
                diag.append(o_d)
            outs[h] = outs[h] + jnp.concatenate(diag, axis=0)

        k_decs = [(k * jnp.exp(b[C - 1:C, :] - b)).astype(_MXU) for k, b in zip(ks, bs)]
        grown = [_dot(v.T.astype(_MXU), kd) for v, kd in zip(vs, k_decs)]
        for h, (b, hs) in enumerate(zip(bs, hss)):
            state_ref[h] = state_ref[h] * jnp.exp(b[C - 1:C, :]) + grown[h]
            gate = jax.nn.sigmoid(g_ref[pl.ds(off, C), hs])
            o_ref[pl.ds(off, C), hs] = (_rms(outs[h], ng_ref[...]) * gate).astype(o_ref.dtype)
        return carry

    lax.fori_loop(0, seq // C, chunk, 0)


def hgrn2(pa, pb, lb, norm_gain, bsz, seq, heads=4):
    lb = lb.reshape(1, HG_KW).astype(_F32)
    log_lb = jnp.log(lb)
    log_1mlb = jnp.log1p(-lb)
    one_mlb = 1.0 - lb
    w = heads * HG_DK

    def seq_spec(blk):
        return pl.BlockSpec((seq, w), lambda b, h, s=blk * LANE // w: (b, s + h))

    head_vec = pl.BlockSpec((1, w), lambda b, h: (0, h))
    return pl.pallas_call(
        functools.partial(_hgrn_kernel, seq=seq, heads=heads),
        grid=(bsz, HG_HEADS // heads),
        in_specs=[seq_spec(_A_BLK[_Q_H]), seq_spec(_B_BLK[_F_H]), seq_spec(_A_BLK[_I_H]), seq_spec(_B_BLK[_G_H]),
                  head_vec, head_vec, head_vec,
                  pl.BlockSpec((1, HG_DV), lambda b, h: (0, 0))],
        out_specs=pl.BlockSpec((seq, w), lambda b, h: (b, h)),
        out_shape=jax.ShapeDtypeStruct((bsz * seq, HG_VW), _MXU),
        scratch_shapes=[pltpu.VMEM((heads, HG_DV, HG_DK), _F32)],
        compiler_params=_params("parallel", "parallel"),
        name="hgrn2",
    )(pa, pb, pa, pb, log_lb, log_1mlb, one_mlb, norm_gain.reshape(1, HG_DV))


def _merge_kernel(o0_ref, o1_ref, o2_ref, w0_ref, w1_ref, w2_ref, g0_ref, g1_ref, g2_ref, out_ref):
    ys = [_dot(o_ref[...], w_ref[0]) for o_ref, w_ref in ((o0_ref, w0_ref), (o1_ref, w1_ref), (o2_ref, w2_ref))]
    acc = jax.nn.sigmoid(g0_ref[...].astype(_F32)) * ys[0]
    acc = acc + jax.nn.sigmoid(g1_ref[...].astype(_F32)) * ys[1]
    acc = acc + jax.nn.sigmoid(g2_ref[...].astype(_F32)) * ys[2]
    out_ref[...] = acc.astype(out_ref.dtype)


def merge_branches(o_nsa, o_sb, o_hg, w_branch, gates, tm=1024, tn=512):
    m = o_nsa.shape[0]
    nj = D_MODEL // tn
    gblk = 0

    def o_spec():
        return pl.BlockSpec((tm, BRANCH_W), lambda i, j: (i, 0))

    def w_spec(n):
        return pl.BlockSpec((1, BRANCH_W, tn), lambda i, j, n=n: (n, 0, j))

    def g_spec(n):
        return pl.BlockSpec((tm, tn), lambda i, j, n=n: (i, gblk + n * nj + j))

    return pl.pallas_call(
        _merge_kernel,
        grid=(m // tm, nj),
        in_specs=[o_spec(), o_spec(), o_spec(), w_spec(0), w_spec(1), w_spec(2), g_spec(0), g_spec(1), g_spec(2)],
        out_specs=pl.BlockSpec((tm, tn), lambda i, j: (i, j)),
        out_shape=jax.ShapeDtypeStruct((m, D_MODEL), _MXU),
        compiler_params=_params("parallel", "arbitrary"),
        name="merge_branches",
    )(o_nsa, o_sb, o_hg, w_branch, w_branch, w_branch, gates, gates, gates)


def _ffn_up_kernel(x_ref, halo_ref, gain_ref, wg_ref, wv_ref, cwg_ref, cwv_ref, cbg_ref, cbv_ref, o_ref, h_ref,
                   *, tiles_per_seq, sub):
    i = pl.program_id(0)

    @pl.when(pl.program_id(1) == 0)
    def _():
        h_ref[...] = _rms(x_ref[...], gain_ref[...]).astype(_MXU)

    first = (i % tiles_per_seq) == 0
    h = h_ref[...]
    halo = _rms(halo_ref[...], gain_ref[...]).astype(_MXU)
    tm = h.shape[0]
    hrows = halo.shape[0]
    row = lax.broadcasted_iota(jnp.int32, (tm, 1), 0)
    cols = [slice(c * sub, (c + 1) * sub) for c in range(wg_ref.shape[1] // sub)]

    prods = [[(_dot(h, w_ref[:, cs]), jnp.where(first, 0.0, _dot(halo, w_ref[:, cs]))) for w_ref in (wg_ref, wv_ref)]
             for cs in cols]

    def conv(u, uh, cw, cb):
        u1 = jnp.where(row == 0, uh[hrows - 1:hrows, :], pltpu.roll(u, 1, axis=0))
        u2 = jnp.where(row == 0, uh[hrows - 2:hrows - 1, :],
                       jnp.where(row == 1, uh[hrows - 1:hrows, :], pltpu.roll(u, 2, axis=0)))
        return cb + u2 * cw[0:1, :] + u1 * cw[1:2, :] + u * cw[2:3, :]

    for cs, ((ug, uhg), (uv, uhv)) in zip(cols, prods):
        gate = conv(ug, uhg, cwg_ref[:, cs], cbg_ref[:, cs])
        val = conv(uv, uhv, cwv_ref[:, cs], cbv_ref[:, cs])
        o_ref[:, cs] = (jax.nn.silu(gate) * val).astype(o_ref.dtype)


def ffn_up(x, gain, w_up, conv_w, conv_b, seq, tm=1024, tn=512, sub=256):
    m, k = x.shape
    nj = D_FF // tn
    hrows = 8
    conv_b = conv_b.reshape(1, 2 * D_FF)
    return pl.pallas_call(
        functools.partial(_ffn_up_kernel, tiles_per_seq=seq // tm, sub=sub),
        grid=(m // tm, nj),
        in_specs=[pl.BlockSpec((tm, k), lambda i, j: (i, 0)),
                  pl.BlockSpec((hrows, k), lambda i, j: (jnp.maximum(i * (tm // hrows) - 1, 0), 0)),
                  pl.BlockSpec((1, k), lambda i, j: (0, 0)),
                  pl.BlockSpec((k, tn), lambda i, j: (0, j)),
                  pl.BlockSpec((k, tn), lambda i, j: (0, nj + j)),
                  pl.BlockSpec((CONV_W, tn), lambda i, j: (0, j)),
                  pl.BlockSpec((CONV_W, tn), lambda i, j: (0, nj + j)),
                  pl.BlockSpec((1, tn), lambda i, j: (0, j)),
                  pl.BlockSpec((1, tn), lambda i, j: (0, nj + j))],
        out_specs=pl.BlockSpec((tm, tn), lambda i, j: (i, j)),
        out_shape=jax.ShapeDtypeStruct((m, D_FF), _MXU),
        scratch_shapes=[pltpu.VMEM((tm, k), _MXU)],
        compiler_params=_params("parallel", "arbitrary"),
        name="ffn_up_conv",
    )(x, x, gain.reshape(1, k), w_up, w_up, conv_w, conv_w, conv_b, conv_b)


def _gather_cols(w, segs):
    return jnp.concatenate([w[:, _SPLIT_OFF[s]:_SPLIT_OFF[s + 1]] for s in segs], axis=1)


def _nsa_gate_cols(w):
    g = w[:, _SPLIT_OFF[_G_N]:_SPLIT_OFF[_G_N + 1]].reshape(-1, NSA_KV_HEADS, NSA_GROUP * 3)
    g = jnp.pad(g, ((0, 0), (0, 0), (0, LANE - NSA_GROUP * 3)))
    return g.reshape(-1, NSA_KV_HEADS * LANE)


def kernel(x, norm_attn, w_in, nsa_q_gain, nsa_k_gain, cmp_pos, cmp_w1, cmp_w2, rel_bias, hg_lower_bound,
           hg_norm_gain, w_branch, w_out, norm_ffn, w_up, conv_w, conv_b, w_down):
    bsz, seq, d = x.shape
    m = bsz * seq
    lb_all = jnp.cumsum(jax.nn.softmax(hg_lower_bound.astype(_F32), axis=0), axis=0)
    lb_all = jnp.maximum(lb_all - lb_all[0:1], 0.0)
    tables = nsa_tables(rel_bias, seq)
    xf = x.reshape(m, d)
    for l in range(w_in.shape[0]):
        wl = w_in[l]
        h = rmsnorm_cast(xf, norm_attn[l])
        pa = matmul(h, _gather_cols(wl, _A_ORDER).astype(_MXU), _MXU, tm=512, tn=A_COLS // 2, name="proj_a")
        pb = matmul(h, _gather_cols(wl, _B_ORDER).astype(_MXU), _F32, tn=1024, name="proj_b")
        gates = matmul(h, _gather_cols(wl, (_GATES,)).astype(_MXU), _MXU, tn=1024, name="proj_gates")
        pkc = matmul(h, _gather_cols(wl, (_KC,)).astype(_MXU), _MXU, name="proj_kc")
        pvc = matmul(h, _gather_cols(wl, (_VC,)).astype(_MXU), _MXU, name="proj_vc")
        pg = matmul(h, _nsa_gate_cols(wl).astype(_MXU), _F32, name="proj_g")
        blk_rows = seq // CMP_STRIDE
        kcmp = compress(pkc.reshape(bsz, blk_rows, CMP_STRIDE * NSA_KV), cmp_w1[l, 0], cmp_w2[l, 0], cmp_pos[l, 0],
                        nsa_k_gain[l, 0], True)
        vcmp_t = compress(pvc.reshape(bsz, blk_rows, CMP_STRIDE * NSA_KV), cmp_w1[l, 1], cmp_w2[l, 1], cmp_pos[l, 1],
                          nsa_k_gain[l, 0], False)
        pvt = matmul_nt(_gather_cols(wl, _T_ORDER).T.astype(_MXU), h, _MXU, name="proj_vt")
        o_nsa = nsa_attention(pa, pvt, kcmp, vcmp_t, pg, tables, nsa_q_gain[l], nsa_k_gain[l], bsz, seq)
        o_sb = stick_breaking(pa, pvt, bsz, seq)
        o_hg = hgrn2(pa, pb, lb_all[l], hg_norm_gain[l], bsz, seq)
        merged = merge_branches(o_nsa, o_sb, o_hg, w_branch[l].astype(_MXU), gates)
        xf = matmul(merged, w_out[l].astype(_MXU), _F32, residual=xf, tn=1024, name="out_proj")
        act = ffn_up(xf, norm_ffn[l], w_up[l].astype(_MXU), conv_w[l], conv_b[l], seq)
        xf = matmul(act, w_down[l].astype(_MXU), _F32, residual=xf, tm=512, tn=1024, name="ffn_down")
    return xf.reshape(bsz, seq, d)
```

```python
import functools
import math

import numpy as np
import jax
import jax.numpy as jnp
from jax import lax
from jax.experimental import pallas as pl
from jax.experimental.pallas import tpu as pltpu

D_MODEL = 2048
DEPTH = 4
HEAD_DIM = 128
BRANCH_W = D_MODEL // 2
N_BRANCH = 3
NSA_HEADS = BRANCH_W // HEAD_DIM
NSA_KV_HEADS = NSA_HEADS // 4
NSA_GROUP = NSA_HEADS // NSA_KV_HEADS
CMP_BLOCK = 32
CMP_STRIDE = 16
SEL_BLOCK = 64
SEL_TOPK = 16
WINDOW = 512
Q_BLOCK = 128
SB_HEADS = BRANCH_W // HEAD_DIM
HG_DK = 128
HG_DV = 128
HG_HEADS = BRANCH_W // HG_DV
HG_CHUNK = 64
HG_SUB = 8
D_FF = 256 * ((8 * D_MODEL // 3 + 255) // 256)
CONV_W = 3
REL_BUCKETS = 32
REL_MAX_DIST = 128
EPS = 1e-6
NEG = -1e30
TINY = 1e-30

NSA_Q = NSA_HEADS * HEAD_DIM
NSA_KV = NSA_KV_HEADS * HEAD_DIM
NSA_GATE = 3 * NSA_HEADS
SB_W = SB_HEADS * HEAD_DIM
HG_KW = HG_HEADS * HG_DK
HG_VW = HG_HEADS * HG_DV
SPLIT_SIZES = (NSA_Q,) + (NSA_KV,) * 6 + (NSA_GATE,) + (SB_W,) * 3 + (HG_KW, HG_KW, HG_VW, HG_VW, N_BRANCH * D_MODEL)
(_Q_N, _KC, _VC, _KS, _VS, _KW, _VW, _G_N, _Q_S, _K_S, _V_S, _Q_H, _F_H, _I_H, _G_H, _GATES) = range(16)
_SPLIT_OFF = np.concatenate([[0], np.cumsum(SPLIT_SIZES)]).tolist()

LANE = 128
_A_ORDER = (_K_S, _Q_H, _I_H, _KS, _KW)
_A_BLK = {}
_off = 0
for _s in _A_ORDER:
    _A_BLK[_s] = _off // LANE
    _off += SPLIT_SIZES[_s]
A_COLS = _off
_T_ORDER = (_V_S, _Q_S, _Q_N, _VS, _VW)
_T_BLK = {}
_off = 0
for _s in _T_ORDER:
    _T_BLK[_s] = _off // LANE
    _off += SPLIT_SIZES[_s]
T_ROWS = _off
_B_ORDER = (_F_H, _G_H)
_B_BLK = {}
_off = 0
for _s in _B_ORDER:
    _B_BLK[_s] = _off // LANE
    _off += SPLIT_SIZES[_s]
B_COLS = _off

VMEM_LIMIT = 48 * 1024 * 1024

_MXU = jnp.bfloat16
_F32 = jnp.float32


def _dot(a, b):
    return jnp.dot(a, b, preferred_element_type=_F32)


def _dot_nt(a, b):
    return lax.dot_general(a, b, (((1,), (1,)), ((), ())), preferred_element_type=_F32)


def _split(x, parts):
    out = []
    r = x
    for _ in range(parts):
        h = r.astype(_MXU)
        out.append(h)
        r = r - h.astype(_F32)
    return out


def _params(*sem):
    return pltpu.CompilerParams(dimension_semantics=sem, vmem_limit_bytes=VMEM_LIMIT)


def _rms(x, gain):
    return x * lax.rsqrt(jnp.mean(x * x, axis=-1, keepdims=True) + EPS) * gain


def _log_sigmoid_pair(z):
    lp = jnp.log(1.0 + jnp.exp(-jnp.abs(z)))
    return jnp.minimum(z, 0.0) - lp, jnp.minimum(-z, 0.0) - lp


def _rmsnorm_kernel(x_ref, g_ref, o_ref):
    o_ref[...] = _rms(x_ref[...], g_ref[...]).astype(o_ref.dtype)


def rmsnorm_cast(x2d, gain):
    m, d = x2d.shape
    tm = 512
    return pl.pallas_call(
        _rmsnorm_kernel,
        grid=(m // tm,),
        in_specs=[pl.BlockSpec((tm, d), lambda i: (i, 0)),
                  pl.BlockSpec((1, d), lambda i: (0, 0))],
        out_specs=pl.BlockSpec((tm, d), lambda i: (i, 0)),
        out_shape=jax.ShapeDtypeStruct((m, d), _MXU),
        compiler_params=_params("parallel"),
        name="rmsnorm_cast",
    )(x2d, gain.reshape(1, d))


def _mm_kernel(a_ref, w_ref, o_ref):
    o_ref[...] = _dot(a_ref[...], w_ref[...]).astype(o_ref.dtype)


def _mm_res_kernel(a_ref, w_ref, r_ref, o_ref):
    o_ref[...] = (r_ref[...] + _dot(a_ref[...], w_ref[...])).astype(o_ref.dtype)


def matmul(a, w, out_dtype, residual=None, tm=1024, tn=512, name="matmul"):
    m, k = a.shape
    n = w.shape[1]
    tm = min(tm, m)
    tn = min(tn, n)
    in_specs = [pl.BlockSpec((tm, k), lambda i, j: (i, 0)),
                pl.BlockSpec((k, tn), lambda i, j: (0, j))]
    args = [a, w]
    kern = _mm_kernel
    if residual is not None:
        in_specs.append(pl.BlockSpec((tm, tn), lambda i, j: (i, j)))
        args.append(residual)
        kern = _mm_res_kernel
    return pl.pallas_call(
        kern,
        grid=(m // tm, n // tn),
        in_specs=in_specs,
        out_specs=pl.BlockSpec((tm, tn), lambda i, j: (i, j)),
        out_shape=jax.ShapeDtypeStruct((m, n), out_dtype),
        compiler_params=_params("parallel", "arbitrary"),
        name=name,
    )(*args)


def _mm_nt_kernel(wt_ref, a_ref, o_ref):
    o_ref[...] = _dot_nt(wt_ref[...], a_ref[...]).astype(o_ref.dtype)


def matmul_nt(wt, a, out_dtype, tm=1024, tn=512, name="matmul_nt"):
    n, k = wt.shape
    m = a.shape[0]
    tm = min(tm, m)
    tn = min(tn, n)
    return pl.pallas_call(
        _mm_nt_kernel,
        grid=(m // tm, n // tn),
        in_specs=[pl.BlockSpec((tn, k), lambda i, j: (j, 0)),
                  pl.BlockSpec((tm, k), lambda i, j: (i, 0))],
        out_specs=pl.BlockSpec((tn, tm), lambda i, j: (j, i)),
        out_shape=jax.ShapeDtypeStruct((n, m), out_dtype),
        compiler_params=_params("parallel", "arbitrary"),
        name=name,
    )(wt, a)


def _compress_kernel(x_ref, wa_ref, wb_ref, pa_ref, pb_ref, w2_ref, gain_ref, o_ref, *, normalize):
    x = x_ref[0].astype(_F32)
    u = _dot((x + pa_ref[...]).astype(_MXU), wa_ref[...])
    v = _dot((x + pb_ref[...]).astype(_MXU), wb_ref[...])
    hid = jax.nn.gelu(u + pltpu.roll(v, v.shape[0] - 1, axis=0)).astype(_MXU)
    if normalize:
        out = _dot(hid, w2_ref[...])
        out = jnp.concatenate(
            [_rms(out[:, g * HEAD_DIM:(g + 1) * HEAD_DIM], gain_ref[...]) for g in range(NSA_KV_HEADS)], axis=1)
    else:
        out = _dot_nt(w2_ref[...], hid)
    o_ref[0] = out.astype(o_ref.dtype)


def compress(x, w1, w2, pos, gain, normalize):
    bsz = x.shape[0]
    nrow = x.shape[1]
    kdim = x.shape[2]
    half = CMP_BLOCK // 2
    eye = jnp.eye(NSA_KV_HEADS, dtype=w1.dtype)

    def embed(w):
        return jnp.einsum('lde,gh->lgdhe', w, eye).reshape(kdim, NSA_KV).astype(_MXU)

    def tile_pos(p):
        return jnp.broadcast_to(p[:, None, :], (half, NSA_KV_HEADS, HEAD_DIM)).reshape(1, kdim)

    w2b = jnp.einsum('de,gh->gdhe', w2, eye).reshape(NSA_KV, NSA_KV).astype(_MXU)
    out_block = (1, nrow, NSA_KV)
    if not normalize:
        w2b = w2b.T
        out_block = (1, NSA_KV, nrow)
    return pl.pallas_call(
        functools.partial(_compress_kernel, normalize=normalize),
        grid=(bsz,),
        in_specs=[pl.BlockSpec((1, nrow, kdim), lambda b: (b, 0, 0)),
                  pl.BlockSpec((kdim, NSA_KV), lambda b: (0, 0)),
                  pl.BlockSpec((kdim, NSA_KV), lambda b: (0, 0)),
                  pl.BlockSpec((1, kdim), lambda b: (0, 0)),
                  pl.BlockSpec((1, kdim), lambda b: (0, 0)),
                  pl.BlockSpec((NSA_KV, NSA_KV), lambda b: (0, 0)),
                  pl.BlockSpec((1, HEAD_DIM), lambda b: (0, 0))],
        out_specs=pl.BlockSpec(out_block, lambda b: (b, 0, 0)),
        out_shape=jax.ShapeDtypeStruct((bsz,) + out_block[1:], _MXU),
        compiler_params=_params("parallel"),
        name="nsa_compress",
    )(x, embed(w1[:half]), embed(w1[half:]), tile_pos(pos[:half]), tile_pos(pos[half:]), w2b,
      gain.reshape(1, HEAD_DIM))


def _nsa_kernel(q_ref, kc_ref, vct_ref, ks_ref, vst_ref, kw_ref, vwt_ref, gate_ref,
                bdiag_ref, boff_ref, bfar_ref, bcmp_ref, qg_ref, kg_ref, ovt_ref,
                o_ref, ksn_ref, kwn_ref, m_ref, l_ref, acc_ref, ocmp_ref, sel_ref, *, seq):
    R = NSA_GROUP
    QB = Q_BLOCK
    W = R * QB
    i = pl.program_id(2)
    scale = HEAD_DIM ** -0.5
    sel_shift = int(math.log2(SEL_BLOCK))

    @pl.when(i == 0)
    def _():
        def body(c, carry):
            off = pl.multiple_of(c * QB, QB)
            ksn_ref[pl.ds(off, QB), :] = _rms(ks_ref[pl.ds(off, QB), :].astype(_F32), kg_ref[1:2, :]).astype(_MXU)
            kwn_ref[pl.ds(off, QB), :] = _rms(kw_ref[pl.ds(off, QB), :].astype(_F32), kg_ref[2:3, :]).astype(_MXU)
            return carry
        lax.fori_loop(0, seq // QB, body, 0)

    heads = [slice(r * QB, (r + 1) * QB) for r in range(R)]

    def norm_t(x):
        return x * lax.rsqrt(jnp.mean(x * x, axis=0, keepdims=True) + EPS) * qg_ref[...]

    q_all = jnp.concatenate(
        [norm_t(q_ref[r * HEAD_DIM:(r + 1) * HEAD_DIM, :].astype(_F32)).astype(_MXU) for r in range(R)], axis=1)

    row = lax.broadcasted_iota(jnp.int32, (QB, QB), 0)
    col = lax.broadcasted_iota(jnp.int32, (QB, QB), 1)

    m_ref[...] = jnp.full(m_ref.shape, NEG, _F32)
    l_ref[...] = jnp.zeros(l_ref.shape, _F32)
    acc_ref[...] = jnp.zeros(acc_ref.shape, _F32)

    def attend_pre(jobs):
        staged = []
        for branch, s_all, mask, vt in jobs:
            p_parts, alphas = [], []
            for r, hs in enumerate(heads):
                slot = branch * R + r
                s = s_all[:, hs]
                if mask is not None:
                    s = jnp.where(mask, s, NEG)
                m_old = m_ref[slot]
                m_new = jnp.maximum(m_old, jnp.max(s, axis=0, keepdims=True))
                alpha = jnp.exp(m_old - m_new)
                p = jnp.exp(s - m_new)
                if mask is not None:
                    p = jnp.where(mask, p, 0.0)
                l_ref[slot] = alpha * l_ref[slot] + jnp.sum(p, axis=0, keepdims=True)
                m_ref[slot] = m_new
                p_parts.append(p.astype(_MXU))
                alphas.append(alpha)
            staged.append((branch, vt, jnp.concatenate(p_parts, axis=1), alphas))
        return staged

    def attend_post(staged):
        upds = [_dot(vt, p_all) for _, vt, p_all, _ in staged]
        for (branch, _, _, alphas), upd in zip(staged, upds):
            for r, hs in enumerate(heads):
                acc_ref[branch, :, hs] = alphas[r] * acc_ref[branch, :, hs] + upd[:, hs]

    def attend(jobs):
        attend_post(attend_pre(jobs))

    valid_c = i * QB + col >= CMP_STRIDE * row + (CMP_BLOCK - 1)
    shift = QB // CMP_STRIDE
    bias_c = bcmp_ref[0, pl.ds(pl.multiple_of((seq // QB - 1 - i) * shift, shift), seq // CMP_STRIDE), :]
    s_c = _dot(kc_ref[0], q_all) * scale + bias_c

    nwb = WINDOW // QB
    rows_w = (nwb - 1) * QB
    start_w = jnp.maximum(i - nwb, 0)
    off_w = pl.multiple_of(start_w * QB, QB)
    key_w = start_w * QB + lax.broadcasted_iota(jnp.int32, (rows_w, QB), 0)
    qry_w = i * QB + lax.broadcasted_iota(jnp.int32, (rows_w, QB), 1)
    mask_w = (key_w > qry_w - WINDOW) & (key_w < (i - 1) * QB)
    s_w = _dot(kwn_ref[pl.ds(off_w, rows_w), :], q_all) * scale + bfar_ref[0]

    p_parts = []
    psum = jnp.zeros((QB, QB), _F32)
    for hs in heads:
        s = jnp.where(valid_c, s_c[:, hs], NEG)
        e = jnp.where(valid_c, jnp.exp(s - jnp.max(s, axis=0, keepdims=True)), 0.0)
        p = e * (1.0 / jnp.maximum(jnp.sum(e, axis=0, keepdims=True), TINY))
        psum = psum + p
        p_parts.append(p.astype(_MXU))
    staged_w = attend_pre([(1, s_w, mask_w, vwt_ref[:, pl.ds(off_w, rows_w)])])
    ocmp_ref[...] = _dot(vct_ref[0], jnp.concatenate(p_parts, axis=1))
    ovt = ovt_ref[...]
    imp = functools.reduce(lambda a, b: a + b, [_dot(ovt, h) for h in _split(psum, 2)])
    attend_post(staged_w)

    n_sel = seq // SEL_BLOCK
    jrow = lax.broadcasted_iota(jnp.int32, (n_sel, QB), 0)
    tcol = lax.broadcasted_iota(jnp.int32, (n_sel, QB), 1)
    qblk = lax.shift_right_arithmetic(i * QB + tcol, sel_shift)
    causal_b = jrow <= qblk
    forced = causal_b & ((jrow == 0) | (jrow >= qblk - 1))
    score = jnp.where(forced, jnp.inf, jnp.where(causal_b, imp[:n_sel, :], -jnp.inf))
    rank = jnp.zeros((n_sel, QB), jnp.int32)
    for jj in range(n_sel):
        sc = score[jj:jj + 1, :]
        beats = (sc > score) | ((sc == score) & (jrow > jj))
        rank = rank + beats.astype(jnp.int32)
    sel_ref[...] = jnp.where((rank < SEL_TOPK) & causal_b, 1.0, 0.0)

    def scores(k_ref, off, rows, bias):
        return _dot(k_ref[pl.ds(off, rows), :], q_all) * scale + bias

    def sel_mask(c, nblk):
        rix = lax.broadcasted_iota(jnp.int32, (nblk * QB, QB), 0)
        per_key_block = QB // SEL_BLOCK
        flags = [sel_ref[pl.ds(per_key_block * c + j, 1), :] for j in range(per_key_block * nblk)]
        m = flags[-1]
        for j in reversed(range(len(flags) - 1)):
            m = jnp.where(rix < (j + 1) * SEL_BLOCK, flags[j], m)
        return m > 0.5

    lower = row <= col
    bfar, boff, bdiag = bfar_ref[0], boff_ref[0], bdiag_ref[0]

    n_far = jnp.maximum(i - 1, 0)
    wide = 4

    def far_tile(c, nblk):
        off = pl.multiple_of(c * QB, QB)
        attend([(0, scores(ksn_ref, off, nblk * QB, bfar), sel_mask(c, nblk), vst_ref[:, pl.ds(off, nblk * QB)])])

    def far_body(p, carry):
        far_tile(p * wide, wide)
        return carry
    lax.fori_loop(0, n_far // wide, far_body, 0)
    rem = n_far % wide

    @pl.when(rem >= 2)
    def _():
        far_tile(n_far - rem, 2)

    @pl.when(rem % 2 == 1)
    def _():
        far_tile(n_far - 1, 1)

    @pl.when(i >= 1)
    def _():
        off = pl.multiple_of((i - 1) * QB, QB)
        bias = jnp.concatenate([boff, bdiag], axis=0)
        after_first = lax.broadcasted_iota(jnp.int32, (2 * QB, QB), 0) - QB
        tcol = lax.broadcasted_iota(jnp.int32, (2 * QB, QB), 1)
        causal = after_first <= tcol
        s_sel = scores(ksn_ref, off, 2 * QB, bias)
        s_win = scores(kwn_ref, off, 2 * QB, bias)
        attend([(0, s_sel, sel_mask(i - 1, 2) & causal, vst_ref[:, pl.ds(off, 2 * QB)]),
                (1, s_win, causal, vwt_ref[:, pl.ds(off, 2 * QB)])])

    @pl.when(i == 0)
    def _():
        s_sel = scores(ksn_ref, 0, QB, bdiag)
        s_win = scores(kwn_ref, 0, QB, bdiag)
        attend([(0, s_sel, sel_mask(0, 1) & lower, vst_ref[:, 0:QB]),
                (1, s_win, lower, vwt_ref[:, 0:QB])])

    gate_t = jax.nn.sigmoid(gate_ref[...].T[:4 * R, :])
    for r, hs in enumerate(heads):
        g_cmp = gate_t[3 * r:3 * r + 1, :]
        g_sel = gate_t[3 * r + 1:3 * r + 2, :] * (1.0 / jnp.maximum(l_ref[r], TINY))
        g_win = gate_t[3 * r + 2:3 * r + 3, :] * (1.0 / jnp.maximum(l_ref[R + r], TINY))
        o_t = g_cmp * ocmp_ref[:, hs] + g_sel * acc_ref[0, :, hs] + g_win * acc_ref[1, :, hs]
        o_ref[:, hs] = o_t.T.astype(o_ref.dtype)


def nsa_attention(pa, pvt, kcmp, vcmp_t, pg, tables, q_gain, k_gain, bsz, seq):
    nqb = seq // Q_BLOCK
    G, R = NSA_KV_HEADS, NSA_GROUP
    W = R * Q_BLOCK
    bdiag, boff, bfar, bcmp, overlap_t = tables
    n_cmp_rows = seq // CMP_STRIDE

    def k_spec(seg):
        return pl.BlockSpec((seq, HEAD_DIM), lambda b, g, i, s=_A_BLK[seg]: (b, s + g))

    def vt_spec(seg):
        return pl.BlockSpec((HEAD_DIM, seq), lambda b, g, i, s=_T_BLK[seg]: (s + g, b))

    group_tile = pl.BlockSpec((1, Q_BLOCK, W), lambda b, g, i: (g, 0, 0))
    return pl.pallas_call(
        functools.partial(_nsa_kernel, seq=seq),
        grid=(bsz, G, nqb),
        in_specs=[pl.BlockSpec((R * HEAD_DIM, Q_BLOCK),
                               lambda b, g, i, s=_T_BLK[_Q_N] // R: (s + g, b * nqb + i)),
                  pl.BlockSpec((1, n_cmp_rows, HEAD_DIM), lambda b, g, i: (b, 0, g)),
                  pl.BlockSpec((1, HEAD_DIM, n_cmp_rows), lambda b, g, i: (b, g, 0)),
                  k_spec(_KS), vt_spec(_VS), k_spec(_KW), vt_spec(_VW),
                  pl.BlockSpec((Q_BLOCK, LANE), lambda b, g, i, s=B_COLS // LANE: (b * nqb + i, s + g)),
                  group_tile, group_tile,
                  pl.BlockSpec((1, 1, W), lambda b, g, i: (g, 0, 0)),
                  pl.BlockSpec((1,) + bcmp.shape[1:], lambda b, g, i: (g, 0, 0)),
                  pl.BlockSpec((HEAD_DIM, Q_BLOCK), lambda b, g, i: (0, 0)),
                  pl.BlockSpec((3, HEAD_DIM), lambda b, g, i: (0, 0)),
                  pl.BlockSpec((LANE, LANE), lambda b, g, i: (0, 0))],
        out_specs=pl.BlockSpec((Q_BLOCK, W), lambda b, g, i: (b * nqb + i, g)),
        out_shape=jax.ShapeDtypeStruct((bsz * seq, NSA_Q), _MXU),
        scratch_shapes=[pltpu.VMEM((seq, HEAD_DIM), _MXU),
                        pltpu.VMEM((seq, HEAD_DIM), _MXU),
                        pltpu.VMEM((2 * R, 1, Q_BLOCK), _F32),
                        pltpu.VMEM((2 * R, 1, Q_BLOCK), _F32),
                        pltpu.VMEM((2, HEAD_DIM, W), _F32),
                        pltpu.VMEM((HEAD_DIM, W), _F32),
                        pltpu.VMEM((seq // SEL_BLOCK, Q_BLOCK), _F32)],
        compiler_params=_params("parallel", "parallel", "arbitrary"),
        name="nsa_attention",
    )(pvt, kcmp, vcmp_t, pa, pvt, pa, pvt, pg, bdiag, boff, bfar, bcmp,
      jnp.broadcast_to(q_gain.reshape(HEAD_DIM, 1), (HEAD_DIM, Q_BLOCK)), k_gain, overlap_t)


def _t5_bucket(dist):
    n = jnp.maximum(dist, 0)
    exact = REL_BUCKETS // 2
    big = exact + (jnp.log(jnp.maximum(n, 1).astype(jnp.float32) / exact)
                   / math.log(REL_MAX_DIST / exact) * (REL_BUCKETS - exact)).astype(jnp.int32)
    return jnp.where(n < exact, n, jnp.minimum(big, REL_BUCKETS - 1))


def nsa_tables(rel_bias, seq):
    G, R = NSA_KV_HEADS, NSA_GROUP
    nqb = seq // Q_BLOCK
    tab_h = rel_bias.T.astype(_F32)
    s = np.arange(Q_BLOCK)[:, None]
    t = np.arange(Q_BLOCK)[None, :]

    def group_tiles(x, lead):
        nl = len(lead)
        rows = x.shape[-2]
        x = x.reshape((G, R) + lead + (rows, Q_BLOCK))
        perm = (0,) + tuple(range(2, 2 + nl)) + (2 + nl, 1, 3 + nl)
        return x.transpose(perm).reshape((G,) + lead + (rows, R * Q_BLOCK))

    bdiag = group_tiles(jnp.take(tab_h, _t5_bucket(jnp.asarray(t - s)), axis=1), ())
    boff = group_tiles(jnp.take(tab_h, _t5_bucket(jnp.asarray(Q_BLOCK + t - s)), axis=1), ())
    bfar = group_tiles(jnp.take(tab_h, _t5_bucket(jnp.full((1, Q_BLOCK), 2 * Q_BLOCK)), axis=1), ())
    lead = (nqb - 1) * Q_BLOCK // CMP_STRIDE
    rows_c = -(-(seq // CMP_STRIDE + lead) // 8) * 8
    m = np.arange(rows_c)[:, None]
    dist_c = t - (CMP_STRIDE * (m - lead) + CMP_BLOCK - 1)
    bcmp = group_tiles(jnp.take(tab_h, _t5_bucket(jnp.asarray(dist_c)), axis=1), ())
    n_cmp = (seq - CMP_BLOCK) // CMP_STRIDE + 1
    n_sel = seq // SEL_BLOCK
    c_start = np.arange(LANE) * CMP_STRIDE
    s_start = np.arange(LANE) * SEL_BLOCK
    overlap = ((c_start[:, None] < s_start[None, :] + SEL_BLOCK)
               & (c_start[:, None] + CMP_BLOCK > s_start[None, :])
               & (np.arange(LANE)[:, None] < n_cmp) & (np.arange(LANE)[None, :] < n_sel))
    return bdiag, boff, bfar, bcmp, jnp.asarray(overlap.T.astype(np.float32)).astype(_MXU)


def _sb_kernel(q_ref, k_ref, vt_ref, o_ref, acc_ref, rest_ref, *, heads):
    QB = Q_BLOCK
    i = pl.program_id(2)
    scale = HEAD_DIM ** -0.5
    row = lax.broadcasted_iota(jnp.int32, (QB, QB), 0)
    col = lax.broadcasted_iota(jnp.int32, (QB, QB), 1)
    strict = row < col
    wide = 2

    def later_matrix(rows):
        r = lax.broadcasted_iota(jnp.int32, (rows, rows), 0)
        c = lax.broadcasted_iota(jnp.int32, (rows, rows), 1)
        return jnp.where(c > r, 1.0, 0.0).astype(_MXU)

    def block(c, first, nblk=1):
        rows = nblk * QB
        later = later_matrix(rows)
        off = pl.multiple_of(c * QB, QB)
        hss = [slice(h * HEAD_DIM, (h + 1) * HEAD_DIM) for h in range(heads)]
        zs = [_dot(k_ref[pl.ds(off, rows), hs], q_ref[hs, :]) * scale for hs in hss]
        log_bs, pieces, totals = [], [], []
        for z in zs:
            log_b, log_1mb = _log_sigmoid_pair(z)
            if first:
                log_1mb = jnp.where(strict, log_1mb, 0.0)
            log_bs.append(log_b)
            pieces.append(_split(log_1mb, 2))
            totals.append(jnp.sum(log_1mb, axis=0, keepdims=True))
        betweens = [_dot(later, p[0]) + _dot(later, p[1]) for p in pieces]
        probs = []
        for h in range(heads):
            if first:
                probs.append(jnp.where(strict, jnp.exp(log_bs[h] + betweens[h]), 0.0).astype(_MXU))
                rest_ref[h] = totals[h]
            else:
                rest = rest_ref[h]
                probs.append(jnp.exp(log_bs[h] + betweens[h] + rest).astype(_MXU))
                rest_ref[h] = rest + totals[h]
        upds = [_dot(vt_ref[hs, pl.ds(off, rows)], probs[h]) for h, hs in enumerate(hss)]
        for h in range(heads):
            acc_ref[h] = upds[h] if first else acc_ref[h] + upds[h]

    block(i, True)

    def body(n, carry):
        block(i - wide * (n + 1), False, wide)
        return carry
    lax.fori_loop(0, i // wide, body, 0)

    def tail(n, carry):
        block(i % wide - 1 - n, False)
        return carry
    lax.fori_loop(0, i % wide, tail, 0)
    for h in range(heads):
        o_ref[:, h * HEAD_DIM:(h + 1) * HEAD_DIM] = acc_ref[h].T.astype(o_ref.dtype)


def stick_breaking(pa, pvt, bsz, seq, heads=SB_HEADS):
    nqb = seq // Q_BLOCK
    w = heads * HEAD_DIM
    qb, kb, vb = _T_BLK[_Q_S] * LANE // w, _A_BLK[_K_S] * LANE // w, _T_BLK[_V_S] * LANE // w
    return pl.pallas_call(
        functools.partial(_sb_kernel, heads=heads),
        grid=(bsz, SB_HEADS // heads, nqb),
        in_specs=[pl.BlockSpec((w, Q_BLOCK), lambda b, h, i: (qb + h, b * nqb + i)),
                  pl.BlockSpec((seq, w), lambda b, h, i: (b, kb + h)),
                  pl.BlockSpec((w, seq), lambda b, h, i: (vb + h, b))],
        out_specs=pl.BlockSpec((Q_BLOCK, w), lambda b, h, i: (b * nqb + i, h)),
        out_shape=jax.ShapeDtypeStruct((bsz * seq, SB_W), _MXU),
        scratch_shapes=[pltpu.VMEM((heads, HEAD_DIM, Q_BLOCK), _F32),
                        pltpu.VMEM((heads, 1, Q_BLOCK), _F32)],
        compiler_params=_params("parallel", "parallel", "arbitrary"),
        name="stick_breaking",
    )(pvt, pa, pvt)


def _hgrn_kernel(q_ref, f_ref, i_ref, g_ref, llb_ref, l1lb_ref, omlb_ref, ng_ref, o_ref, state_ref, *, seq, heads):
    C, SUB = HG_CHUNK, HG_SUB
    nsub = C // SUB
    hss = [slice(h * HG_DK, (h + 1) * HG_DK) for h in range(heads)]
    row = lax.broadcasted_iota(jnp.int32, (C, C), 0)
    col = lax.broadcasted_iota(jnp.int32, (C, C), 1)
    upto = jnp.where(col <= row, 1.0, 0.0).astype(_MXU)
    sub_t = lax.broadcasted_iota(jnp.int32, (SUB, 1), 0)
    sub_c = lax.broadcasted_iota(jnp.int32, (SUB, C), 1)
    state_ref[...] = jnp.zeros(state_ref.shape, _F32)

    def chunk(n, carry):
        off = pl.multiple_of(n * C, C)
        fps = [f_ref[pl.ds(off, C), hs] for hs in hss]
        pieces = []
        for fp, hs in zip(fps, hss):
            log_sig, _ = _log_sigmoid_pair(fp)
            y = l1lb_ref[:, hs] + log_sig
            a = llb_ref[:, hs]
            log_f = jnp.maximum(a, y) + jnp.log1p(jnp.exp(-jnp.abs(a - y)))
            pieces.append(_split(log_f, 3))
        bs = [_dot(upto, p[0]) + _dot(upto, p[1]) + _dot(upto, p[2]) for p in pieces]
        qs = [q_ref[pl.ds(off, C), hs].astype(_F32) for hs in hss]
        ks = [omlb_ref[:, hs] * jax.nn.sigmoid(-fp) for fp, hs in zip(fps, hss)]
        vs = [i_ref[pl.ds(off, C), hs].astype(_F32) for hs in hss]

        outs = [_dot_nt((q * jnp.exp(b)).astype(_MXU), state_ref[h].astype(_MXU))
                for h, (q, b) in enumerate(zip(qs, bs))]

        lhs, rhs = [], []
        for q, k, b in zip(qs, ks, bs):
            for s_i in range(1, nsub):
                lo = s_i * SUB
                b_ref = b[lo - 1:lo, :]
                lhs.append((q[lo:lo + SUB, :] * jnp.exp(b[lo:lo + SUB, :] - b_ref)).astype(_MXU))
                rhs.append((k * jnp.exp(jnp.minimum(b_ref - b, 0.0))).astype(_MXU))
        a_blocks = [_dot_nt(x, y) for x, y in zip(lhs, rhs)]
        intra = []
        for h in range(heads):
            rows = [jnp.zeros((SUB, C), _F32)]
            for s_i in range(1, nsub):
                rows.append(jnp.where(sub_c < s_i * SUB, a_blocks[h * (nsub - 1) + s_i - 1], 0.0))
            intra.append(jnp.concatenate(rows, axis=0).astype(_MXU))
        outs = [o + _dot(a, v.astype(_MXU)) for o, a, v in zip(outs, intra, vs)]

        for h, (q, k, v, b) in enumerate(zip(qs, ks, vs, bs)):
            diag = []
            for s_i in range(nsub):
                lo = s_i * SUB
                b_i = b[lo:lo + SUB, :]
                q_i = q[lo:lo + SUB, :]
                o_d = jnp.zeros((SUB, HG_DV), _F32)
                for s in range(SUB):
                    w = jnp.exp(b_i - b[lo + s:lo + s + 1, :])
                    a_col = jnp.sum(q_i * k[lo + s:lo + s + 1, :] * w, axis=-1, keepdims=True)
                    a_col = jnp.where(sub_t >= s, a_col, 0.0)
                    o_d = o_d + a_col * v[lo + s:lo + s + 1, :]
                diag.append(o_d)
            outs[h] = outs[h] + jnp.concatenate(diag, axis=0)

        k_decs = [(k * jnp.exp(b[C - 1:C, :] - b)).astype(_MXU) for k, b in zip(ks, bs)]
        grown = [_dot(v.T.astype(_MXU), kd) for v, kd in zip(vs, k_decs)]
        for h, (b, hs) in enumerate(zip(bs, hss)):
            state_ref[h] = state_ref[h] * jnp.exp(b[C - 1:C, :]) + grown[h]
            gate = jax.nn.sigmoid(g_ref[pl.ds(off, C), hs])
            o_ref[pl.ds(off, C), hs] = (_rms(outs[h], ng_ref[...]) * gate).astype(o_ref.dtype)
        return carry

    lax.fori_loop(0, seq // C, chunk, 0)


def hgrn2(pa, pb, lb, norm_gain, bsz, seq, heads=4):
    lb = lb.reshape(1, HG_KW).astype(_F32)
    log_lb = jnp.log(lb)
    log_1mlb = jnp.log1p(-lb)
    one_mlb = 1.0 - lb
    w = heads * HG_DK

    def seq_spec(blk):
        return pl.BlockSpec((seq, w), lambda b, h, s=blk * LANE // w: (b, s + h))

    head_vec = pl.BlockSpec((1, w), lambda b, h: (0, h))
    return pl.pallas_call(
        functools.partial(_hgrn_kernel, seq=seq, heads=heads),
        grid=(bsz, HG_HEADS // heads),
        in_specs=[seq_spec(_A_BLK[_Q_H]), seq_spec(_B_BLK[_F_H]), seq_spec(_A_BLK[_I_H]), seq_spec(_B_BLK[_G_H]),
                  head_vec, head_vec, head_vec,
                  pl.BlockSpec((1, HG_DV), lambda b, h: (0, 0))],
        out_specs=pl.BlockSpec((seq, w), lambda b, h: (b, h)),
        out_shape=jax.ShapeDtypeStruct((bsz * seq, HG_VW), _MXU),
        scratch_shapes=[pltpu.VMEM((heads, HG_DV, HG_DK), _F32)],
        compiler_params=_params("parallel", "parallel"),
        name="hgrn2",
    )(pa, pb, pa, pb, log_lb, log_1mlb, one_mlb, norm_gain.reshape(1, HG_DV))


def _merge_kernel(o0_ref, o1_ref, o2_ref, w0_ref, w1_ref, w2_ref, g0_ref, g1_ref, g2_ref, out_ref):
    ys = [_dot(o_ref[...], w_ref[0]) for o_ref, w_ref in ((o0_ref, w0_ref), (o1_ref, w1_ref), (o2_ref, w2_ref))]
    acc = jax.nn.sigmoid(g0_ref[...].astype(_F32)) * ys[0]
    acc = acc + jax.nn.sigmoid(g1_ref[...].astype(_F32)) * ys[1]
    acc = acc + jax.nn.sigmoid(g2_ref[...].astype(_F32)) * ys[2]
    out_ref[...] = acc.astype(out_ref.dtype)


def merge_branches(o_nsa, o_sb, o_hg, w_branch, gates, tm=1024, tn=512):
    m = o_nsa.shape[0]
    nj = D_MODEL // tn
    gblk = 0

    def o_spec():
        return pl.BlockSpec((tm, BRANCH_W), lambda i, j: (i, 0))

    def w_spec(n):
        return pl.BlockSpec((1, BRANCH_W, tn), lambda i, j, n=n: (n, 0, j))

    def g_spec(n):
        return pl.BlockSpec((tm, tn), lambda i, j, n=n: (i, gblk + n * nj + j))

    return pl.pallas_call(
        _merge_kernel,
        grid=(m // tm, nj),
        in_specs=[o_spec(), o_spec(), o_spec(), w_spec(0), w_spec(1), w_spec(2), g_spec(0), g_spec(1), g_spec(2)],
        out_specs=pl.BlockSpec((tm, tn), lambda i, j: (i, j)),
        out_shape=jax.ShapeDtypeStruct((m, D_MODEL), _MXU),
        compiler_params=_params("parallel", "arbitrary"),
        name="merge_branches",
    )(o_nsa, o_sb, o_hg, w_branch, w_branch, w_branch, gates, gates, gates)


def _ffn_up_kernel(x_ref, halo_ref, gain_ref, wg_ref, wv_ref, cwg_ref, cwv_ref, cbg_ref, cbv_ref, o_ref, h_ref,
                   *, tiles_per_seq, sub):
    i = pl.program_id(0)

    @pl.when(pl.program_id(1) == 0)
    def _():
        h_ref[...] = _rms(x_ref[...], gain_ref[...]).astype(_MXU)

    first = (i % tiles_per_seq) == 0
    h = h_ref[...]
    halo = _rms(halo_ref[...], gain_ref[...]).astype(_MXU)
    tm = h.shape[0]
    hrows = halo.shape[0]
    row = lax.broadcasted_iota(jnp.int32, (tm, 1), 0)
    cols = [slice(c * sub, (c + 1) * sub) for c in range(wg_ref.shape[1] // sub)]

    prods = [[(_dot(h, w_ref[:, cs]), jnp.where(first, 0.0, _dot(halo, w_ref[:, cs]))) for w_ref in (wg_ref, wv_ref)]
             for cs in cols]

    def conv(u, uh, cw, cb):
        u1 = jnp.where(row == 0, uh[hrows - 1:hrows, :], pltpu.roll(u, 1, axis=0))
        u2 = jnp.where(row == 0, uh[hrows - 2:hrows - 1, :],
                       jnp.where(row == 1, uh[hrows - 1:hrows, :], pltpu.roll(u, 2, axis=0)))
        return cb + u2 * cw[0:1, :] + u1 * cw[1:2, :] + u * cw[2:3, :]

    for cs, ((ug, uhg), (uv, uhv)) in zip(cols, prods):
        gate = conv(ug, uhg, cwg_ref[:, cs], cbg_ref[:, cs])
        val = conv(uv, uhv, cwv_ref[:, cs], cbv_ref[:, cs])
        o_ref[:, cs] = (jax.nn.silu(gate) * val).astype(o_ref.dtype)


def ffn_up(x, gain, w_up, conv_w, conv_b, seq, tm=1024, tn=512, sub=256):
    m, k = x.shape
    nj = D_FF // tn
    hrows = 8
    conv_b = conv_b.reshape(1, 2 * D_FF)
    return pl.pallas_call(
        functools.partial(_ffn_up_kernel, tiles_per_seq=seq // tm, sub=sub),
        grid=(m // tm, nj),
        in_specs=[pl.BlockSpec((tm, k), lambda i, j: (i, 0)),
                  pl.BlockSpec((hrows, k), lambda i, j: (jnp.maximum(i * (tm // hrows) - 1, 0), 0)),
                  pl.BlockSpec((1, k), lambda i, j: (0, 0)),
                  pl.BlockSpec((k, tn), lambda i, j: (0, j)),
                  pl.BlockSpec((k, tn), lambda i, j: (0, nj + j)),
                  pl.BlockSpec((CONV_W, tn), lambda i, j: (0, j)),
                  pl.BlockSpec((CONV_W, tn), lambda i, j: (0, nj + j)),
                  pl.BlockSpec((1, tn), lambda i, j: (0, j)),
                  pl.BlockSpec((1, tn), lambda i, j: (0, nj + j))],
        out_specs=pl.BlockSpec((tm, tn), lambda i, j: (i, j)),
        out_shape=jax.ShapeDtypeStruct((m, D_FF), _MXU),
        scratch_shapes=[pltpu.VMEM((tm, k), _MXU)],
        compiler_params=_params("parallel", "arbitrary"),
        name="ffn_up_conv",
    )(x, x, gain.reshape(1, k), w_up, w_up, conv_w, conv_w, conv_b, conv_b)


def _gather_cols(w, segs):
    return jnp.concatenate([w[:, _SPLIT_OFF[s]:_SPLIT_OFF[s + 1]] for s in segs], axis=1)


def _nsa_gate_cols(w):
    g = w[:, _SPLIT_OFF[_G_N]:_SPLIT_OFF[_G_N + 1]].reshape(-1, NSA_KV_HEADS, NSA_GROUP * 3)
    g = jnp.pad(g, ((0, 0), (0, 0), (0, LANE - NSA_GROUP * 3)))
    return g.reshape(-1, NSA_KV_HEADS * LANE)


def kernel(x, norm_attn, w_in, nsa_q_gain, nsa_k_gain, cmp_pos, cmp_w1, cmp_w2, rel_bias, hg_lower_bound,
           hg_norm_gain, w_branch, w_out, norm_ffn, w_up, conv_w, conv_b, w_down):
    bsz, seq, d = x.shape
    m = bsz * seq
    lb_all = jnp.cumsum(jax.nn.softmax(hg_lower_bound.astype(_F32), axis=0), axis=0)
    lb_all = jnp.maximum(lb_all - lb_all[0:1], 0.0)
    tables = nsa_tables(rel_bias, seq)
    xf = x.reshape(m, d)
    for l in range(w_in.shape[0]):
        wl = w_in[l]
        h = rmsnorm_cast(xf, norm_attn[l])
        pa = matmul(h, _gather_cols(wl, _A_ORDER).astype(_MXU), _MXU, tm=512, tn=A_COLS // 2, name="proj_a")
        w_b = jnp.concatenate([_gather_cols(wl, _B_ORDER), _nsa_gate_cols(wl)], axis=1).astype(_MXU)
        pb = matmul(h, w_b, _F32, tn=w_b.shape[1] // 3, name="proj_b")
        gates = matmul(h, _gather_cols(wl, (_GATES,)).astype(_MXU), _MXU, tn=1024, name="proj_gates")
        pkc = matmul(h, _gather_cols(wl, (_KC,)).astype(_MXU), _MXU, name="proj_kc")
        pvc = matmul(h, _gather_cols(wl, (_VC,)).astype(_MXU), _MXU, name="proj_vc")
        blk_rows = seq // CMP_STRIDE
        kcmp = compress(pkc.reshape(bsz, blk_rows, CMP_STRIDE * NSA_KV), cmp_w1[l, 0], cmp_w2[l, 0], cmp_pos[l, 0],
                        nsa_k_gain[l, 0], True)
        vcmp_t = compress(pvc.reshape(bsz, blk_rows, CMP_STRIDE * NSA_KV), cmp_w1[l, 1], cmp_w2[l, 1], cmp_pos[l, 1],
                          nsa_k_gain[l, 0], False)
        pvt = matmul_nt(_gather_cols(wl, _T_ORDER).T.astype(_MXU), h, _MXU, name="proj_vt")
        o_nsa = nsa_attention(pa, pvt, kcmp, vcmp_t, pb, tables, nsa_q_gain[l], nsa_k_gain[l], bsz, seq)
        o_sb = stick_breaking(pa, pvt, bsz, seq)
        o_hg = hgrn2(pa, pb, lb_all[l], hg_norm_gain[l], bsz, seq)
        merged = merge_branches(o_nsa, o_sb, o_hg, w_branch[l].astype(_MXU), gates)
        xf = matmul(merged, w_out[l].astype(_MXU), _F32, residual=xf, tn=1024, name="out_proj")
        act = ffn_up(xf, norm_ffn[l], w_up[l].astype(_MXU), conv_w[l], conv_b[l], seq)
        xf = matmul(act, w_down[l].astype(_MXU), _F32, residual=xf, tm=512, tn=1024, name="ffn_down")
    return xf.reshape(bsz, seq, d)
```

```python
import functools
import math

import numpy as np
import jax
import jax.numpy as jnp
from jax import lax
from jax.experimental import pallas as pl
from jax.experimental.pallas import tpu as pltpu

D_MODEL = 2048
DEPTH = 4
HEAD_DIM = 128
BRANCH_W = D_MODEL // 2
N_BRANCH = 3
NSA_HEADS = BRANCH_W // HEAD_DIM
NSA_KV_HEADS = NSA_HEADS // 4
NSA_GROUP = NSA_HEADS // NSA_KV_HEADS
CMP_BLOCK = 32
CMP_STRIDE = 16
SEL_BLOCK = 64
SEL_TOPK = 16
WINDOW = 512
Q_BLOCK = 128
SB_HEADS = BRANCH_W // HEAD_DIM
HG_DK = 128
HG_DV = 128
HG_HEADS = BRANCH_W // HG_DV
HG_CHUNK = 64
HG_SUB = 8
D_FF = 256 * ((8 * D_MODEL // 3 + 255) // 256)
CONV_W = 3
REL_BUCKETS = 32
REL_MAX_DIST = 128
EPS = 1e-6
NEG = -1e30
TINY = 1e-30

NSA_Q = NSA_HEADS * HEAD_DIM
NSA_KV = NSA_KV_HEADS * HEAD_DIM
NSA_GATE = 3 * NSA_HEADS
SB_W = SB_HEADS * HEAD_DIM
HG_KW = HG_HEADS * HG_DK
HG_VW = HG_HEADS * HG_DV
SPLIT_SIZES = (NSA_Q,) + (NSA_KV,) * 6 + (NSA_GATE,) + (SB_W,) * 3 + (HG_KW, HG_KW, HG_VW, HG_VW, N_BRANCH * D_MODEL)
(_Q_N, _KC, _VC, _KS, _VS, _KW, _VW, _G_N, _Q_S, _K_S, _V_S, _Q_H, _F_H, _I_H, _G_H, _GATES) = range(16)
_SPLIT_OFF = np.concatenate([[0], np.cumsum(SPLIT_SIZES)]).tolist()

LANE = 128
_A_ORDER = (_K_S, _Q_H, _I_H, _KS, _KW)
_A_BLK = {}
_off = 0
for _s in _A_ORDER:
    _A_BLK[_s] = _off // LANE
    _off += SPLIT_SIZES[_s]
A_COLS = _off
_T_ORDER = (_V_S, _Q_S, _Q_N, _VS, _VW)
_T_BLK = {}
_off = 0
for _s in _T_ORDER:
    _T_BLK[_s] = _off // LANE
    _off += SPLIT_SIZES[_s]
T_ROWS = _off
_B_ORDER = (_F_H, _G_H)
_B_BLK = {}
_off = 0
for _s in _B_ORDER:
    _B_BLK[_s] = _off // LANE
    _off += SPLIT_SIZES[_s]
B_COLS = _off

VMEM_LIMIT = 48 * 1024 * 1024

_MXU = jnp.bfloat16
_F32 = jnp.float32


def _dot(a, b):
    return jnp.dot(a, b, preferred_element_type=_F32)


def _dot_nt(a, b):
    return lax.dot_general(a, b, (((1,), (1,)), ((), ())), preferred_element_type=_F32)


def _split(x, parts):
    out = []
    r = x
    for _ in range(parts):
        h = r.astype(_MXU)
        out.append(h)
        r = r - h.astype(_F32)
    return out


def _params(*sem):
    return pltpu.CompilerParams(dimension_semantics=sem, vmem_limit_bytes=VMEM_LIMIT)


def _rms(x, gain):
    return x * lax.rsqrt(jnp.mean(x * x, axis=-1, keepdims=True) + EPS) * gain


def _log_sigmoid_pair(z):
    lp = jnp.log(1.0 + jnp.exp(-jnp.abs(z)))
    return jnp.minimum(z, 0.0) - lp, jnp.minimum(-z, 0.0) - lp


def _rmsnorm_kernel(x_ref, g_ref, o_ref):
    o_ref[...] = _rms(x_ref[...], g_ref[...]).astype(o_ref.dtype)


def rmsnorm_cast(x2d, gain):
    m, d = x2d.shape
    tm = 512
    return pl.pallas_call(
        _rmsnorm_kernel,
        grid=(m // tm,),
        in_specs=[pl.BlockSpec((tm, d), lambda i: (i, 0)),
                  pl.BlockSpec((1, d), lambda i: (0, 0))],
        out_specs=pl.BlockSpec((tm, d), lambda i: (i, 0)),
        out_shape=jax.ShapeDtypeStruct((m, d), _MXU),
        compiler_params=_params("parallel"),
        name="rmsnorm_cast",
    )(x2d, gain.reshape(1, d))


def _mm_kernel(a_ref, w_ref, o_ref):
    o_ref[...] = _dot(a_ref[...], w_ref[...]).astype(o_ref.dtype)


def _mm_res_kernel(a_ref, w_ref, r_ref, o_ref):
    o_ref[...] = (r_ref[...] + _dot(a_ref[...], w_ref[...])).astype(o_ref.dtype)


def matmul(a, w, out_dtype, residual=None, tm=1024, tn=512, name="matmul"):
    m, k = a.shape
    n = w.shape[1]
    tm = min(tm, m)
    tn = min(tn, n)
    in_specs = [pl.BlockSpec((tm, k), lambda i, j: (i, 0)),
                pl.BlockSpec((k, tn), lambda i, j: (0, j))]
    args = [a, w]
    kern = _mm_kernel
    if residual is not None:
        in_specs.append(pl.BlockSpec((tm, tn), lambda i, j: (i, j)))
        args.append(residual)
        kern = _mm_res_kernel
    return pl.pallas_call(
        kern,
        grid=(m // tm, n // tn),
        in_specs=in_specs,
        out_specs=pl.BlockSpec((tm, tn), lambda i, j: (i, j)),
        out_shape=jax.ShapeDtypeStruct((m, n), out_dtype),
        compiler_params=_params("parallel", "arbitrary"),
        name=name,
    )(*args)


def _mm_nt_kernel(wt_ref, a_ref, o_ref):
    o_ref[...] = _dot_nt(wt_ref[...], a_ref[...]).astype(o_ref.dtype)


def matmul_nt(wt, a, out_dtype, tm=1024, tn=512, name="matmul_nt"):
    n, k = wt.shape
    m = a.shape[0]
    tm = min(tm, m)
    tn = min(tn, n)
    return pl.pallas_call(
        _mm_nt_kernel,
        grid=(m // tm, n // tn),
        in_specs=[pl.BlockSpec((tn, k), lambda i, j: (j, 0)),
                  pl.BlockSpec((tm, k), lambda i, j: (i, 0))],
        out_specs=pl.BlockSpec((tn, tm), lambda i, j: (j, i)),
        out_shape=jax.ShapeDtypeStruct((n, m), out_dtype),
        compiler_params=_params("parallel", "arbitrary"),
        name=name,
    )(wt, a)


def _compress_kernel(x_ref, wa_ref, wb_ref, pa_ref, pb_ref, w2_ref, gain_ref, o_ref, *, normalize):
    x = x_ref[0].astype(_F32)
    u = _dot((x + pa_ref[...]).astype(_MXU), wa_ref[...])
    v = _dot((x + pb_ref[...]).astype(_MXU), wb_ref[...])
    hid = jax.nn.gelu(u + pltpu.roll(v, v.shape[0] - 1, axis=0)).astype(_MXU)
    if normalize:
        out = _dot(hid, w2_ref[...])
        out = jnp.concatenate(
            [_rms(out[:, g * HEAD_DIM:(g + 1) * HEAD_DIM], gain_ref[...]) for g in range(NSA_KV_HEADS)], axis=1)
    else:
        out = _dot_nt(w2_ref[...], hid)
    o_ref[0] = out.astype(o_ref.dtype)


def compress(x, w1, w2, pos, gain, normalize):
    bsz = x.shape[0]
    nrow = x.shape[1]
    kdim = x.shape[2]
    half = CMP_BLOCK // 2
    eye = jnp.eye(NSA_KV_HEADS, dtype=w1.dtype)

    def embed(w):
        return jnp.einsum('lde,gh->lgdhe', w, eye).reshape(kdim, NSA_KV).astype(_MXU)

    def tile_pos(p):
        return jnp.broadcast_to(p[:, None, :], (half, NSA_KV_HEADS, HEAD_DIM)).reshape(1, kdim)

    w2b = jnp.einsum('de,gh->gdhe', w2, eye).reshape(NSA_KV, NSA_KV).astype(_MXU)
    out_block = (1, nrow, NSA_KV)
    if not normalize:
        w2b = w2b.T
        out_block = (1, NSA_KV, nrow)
    return pl.pallas_call(
        functools.partial(_compress_kernel, normalize=normalize),
        grid=(bsz,),
        in_specs=[pl.BlockSpec((1, nrow, kdim), lambda b: (b, 0, 0)),
                  pl.BlockSpec((kdim, NSA_KV), lambda b: (0, 0)),
                  pl.BlockSpec((kdim, NSA_KV), lambda b: (0, 0)),
                  pl.BlockSpec((1, kdim), lambda b: (0, 0)),
                  pl.BlockSpec((1, kdim), lambda b: (0, 0)),
                  pl.BlockSpec((NSA_KV, NSA_KV), lambda b: (0, 0)),
                  pl.BlockSpec((1, HEAD_DIM), lambda b: (0, 0))],
        out_specs=pl.BlockSpec(out_block, lambda b: (b, 0, 0)),
        out_shape=jax.ShapeDtypeStruct((bsz,) + out_block[1:], _MXU),
        compiler_params=_params("parallel"),
        name="nsa_compress",
    )(x, embed(w1[:half]), embed(w1[half:]), tile_pos(pos[:half]), tile_pos(pos[half:]), w2b,
      gain.reshape(1, HEAD_DIM))


def _nsa_kernel(q_ref, kc_ref, vct_ref, ks_ref, vst_ref, kw_ref, vwt_ref, gate_ref,
                bdiag_ref, boff_ref, bfar_ref, bcmp_ref, qg_ref, kg_ref, ovt_ref,
                o_ref, ksn_ref, kwn_ref, m_ref, l_ref, acc_ref, ocmp_ref, sel_ref, *, seq):
    G, R, QB, D = NSA_KV_HEADS, NSA_GROUP, Q_BLOCK, HEAD_DIM
    W = R * QB
    i = pl.program_id(1)

    @pl.when(i == 0)
    def _():
        def body(c, carry):
            off = pl.multiple_of(c * QB, QB)
            for g in range(G):
                gs = slice(g * D, (g + 1) * D)
                ksn_ref[pl.ds(off, QB), gs] = _rms(ks_ref[pl.ds(off, QB), gs].astype(_F32), kg_ref[1:2, :]).astype(_MXU)
                kwn_ref[pl.ds(off, QB), gs] = _rms(kw_ref[pl.ds(off, QB), gs].astype(_F32), kg_ref[2:3, :]).astype(_MXU)
            return carry
        lax.fori_loop(0, seq // QB, body, 0)

    for g in range(G):
        gs = slice(g * D, (g + 1) * D)
        one = slice(g, g + 1)
        _nsa_group(i, q_ref.at[g * R * D:(g + 1) * R * D, :], kc_ref.at[:, :, gs], vct_ref.at[:, gs, :],
                   vst_ref.at[gs, :], vwt_ref.at[gs, :], gate_ref.at[:, gs],
                   bdiag_ref.at[one], boff_ref.at[one], bfar_ref.at[one], bcmp_ref.at[one], qg_ref, ovt_ref,
                   o_ref.at[:, g * W:(g + 1) * W], ksn_ref.at[:, gs], kwn_ref.at[:, gs],
                   m_ref.at[g], l_ref.at[g], acc_ref.at[g], ocmp_ref.at[g], sel_ref.at[g], seq=seq)


def _nsa_group(i, q_ref, kc_ref, vct_ref, vst_ref, vwt_ref, gate_ref,
               bdiag_ref, boff_ref, bfar_ref, bcmp_ref, qg_ref, ovt_ref,
               o_ref, ksn_ref, kwn_ref, m_ref, l_ref, acc_ref, ocmp_ref, sel_ref, *, seq):
    R = NSA_GROUP
    QB = Q_BLOCK
    scale = HEAD_DIM ** -0.5
    sel_shift = int(math.log2(SEL_BLOCK))

    heads = [slice(r * QB, (r + 1) * QB) for r in range(R)]

    def norm_t(x):
        return x * lax.rsqrt(jnp.mean(x * x, axis=0, keepdims=True) + EPS) * qg_ref[...]

    q_all = jnp.concatenate(
        [norm_t(q_ref[r * HEAD_DIM:(r + 1) * HEAD_DIM, :].astype(_F32)).astype(_MXU) for r in range(R)], axis=1)

    row = lax.broadcasted_iota(jnp.int32, (QB, QB), 0)
    col = lax.broadcasted_iota(jnp.int32, (QB, QB), 1)

    m_ref[...] = jnp.full(m_ref.shape, NEG, _F32)
    l_ref[...] = jnp.zeros(l_ref.shape, _F32)
    acc_ref[...] = jnp.zeros(acc_ref.shape, _F32)

    def attend_pre(jobs):
        staged = []
        for branch, s_all, mask, vt in jobs:
            p_parts, alphas = [], []
            for r, hs in enumerate(heads):
                slot = branch * R + r
                s = s_all[:, hs]
                if mask is not None:
                    s = jnp.where(mask, s, NEG)
                m_old = m_ref[slot]
                m_new = jnp.maximum(m_old, jnp.max(s, axis=0, keepdims=True))
                alpha = jnp.exp(m_old - m_new)
                p = jnp.exp(s - m_new)
                if mask is not None:
                    p = jnp.where(mask, p, 0.0)
                l_ref[slot] = alpha * l_ref[slot] + jnp.sum(p, axis=0, keepdims=True)
                m_ref[slot] = m_new
                p_parts.append(p.astype(_MXU))
                alphas.append(alpha)
            staged.append((branch, vt, jnp.concatenate(p_parts, axis=1), alphas))
        return staged

    def attend_post(staged):
        upds = [_dot(vt, p_all) for _, vt, p_all, _ in staged]
        for (branch, _, _, alphas), upd in zip(staged, upds):
            for r, hs in enumerate(heads):
                acc_ref[branch, :, hs] = alphas[r] * acc_ref[branch, :, hs] + upd[:, hs]

    def attend(jobs):
        attend_post(attend_pre(jobs))

    valid_c = i * QB + col >= CMP_STRIDE * row + (CMP_BLOCK - 1)
    shift = QB // CMP_STRIDE
    bias_c = bcmp_ref[0, pl.ds(pl.multiple_of((seq // QB - 1 - i) * shift, shift), seq // CMP_STRIDE), :]
    s_c = _dot(kc_ref[0], q_all) * scale + bias_c

    nwb = WINDOW // QB
    rows_w = (nwb - 1) * QB
    start_w = jnp.maximum(i - nwb, 0)
    off_w = pl.multiple_of(start_w * QB, QB)
    key_w = start_w * QB + lax.broadcasted_iota(jnp.int32, (rows_w, QB), 0)
    qry_w = i * QB + lax.broadcasted_iota(jnp.int32, (rows_w, QB), 1)
    mask_w = (key_w > qry_w - WINDOW) & (key_w < (i - 1) * QB)
    s_w = _dot(kwn_ref[pl.ds(off_w, rows_w), :], q_all) * scale + bfar_ref[0]

    p_parts = []
    psum = jnp.zeros((QB, QB), _F32)
    for hs in heads:
        s = jnp.where(valid_c, s_c[:, hs], NEG)
        e = jnp.where(valid_c, jnp.exp(s - jnp.max(s, axis=0, keepdims=True)), 0.0)
        p = e * (1.0 / jnp.maximum(jnp.sum(e, axis=0, keepdims=True), TINY))
        psum = psum + p
        p_parts.append(p.astype(_MXU))
    staged_w = attend_pre([(1, s_w, mask_w, vwt_ref[:, pl.ds(off_w, rows_w)])])
    ocmp_ref[...] = _dot(vct_ref[0], jnp.concatenate(p_parts, axis=1))
    ovt = ovt_ref[...]
    imp = functools.reduce(lambda a, b: a + b, [_dot(ovt, h) for h in _split(psum, 2)])
    attend_post(staged_w)

    n_sel = seq // SEL_BLOCK
    jrow = lax.broadcasted_iota(jnp.int32, (n_sel, QB), 0)
    tcol = lax.broadcasted_iota(jnp.int32, (n_sel, QB), 1)
    qblk = lax.shift_right_arithmetic(i * QB + tcol, sel_shift)
    causal_b = jrow <= qblk
    forced = causal_b & ((jrow == 0) | (jrow >= qblk - 1))
    score = jnp.where(forced, jnp.inf, jnp.where(causal_b, imp[:n_sel, :], -jnp.inf))
    rank = jnp.zeros((n_sel, QB), jnp.int32)
    for jj in range(n_sel):
        sc = score[jj:jj + 1, :]
        beats = (sc > score) | ((sc == score) & (jrow > jj))
        rank = rank + beats.astype(jnp.int32)
    sel_ref[...] = jnp.where((rank < SEL_TOPK) & causal_b, 1.0, 0.0)

    def scores(k_ref, off, rows, bias):
        return _dot(k_ref[pl.ds(off, rows), :], q_all) * scale + bias

    def sel_mask(c, nblk):
        rix = lax.broadcasted_iota(jnp.int32, (nblk * QB, QB), 0)
        per_key_block = QB // SEL_BLOCK
        flags = [sel_ref[pl.ds(per_key_block * c + j, 1), :] for j in range(per_key_block * nblk)]
        m = flags[-1]
        for j in reversed(range(len(flags) - 1)):
            m = jnp.where(rix < (j + 1) * SEL_BLOCK, flags[j], m)
        return m > 0.5

    lower = row <= col
    bfar, boff, bdiag = bfar_ref[0], boff_ref[0], bdiag_ref[0]

    n_far = jnp.maximum(i - 1, 0)
    wide = 4

    def far_tile(c, nblk):
        off = pl.multiple_of(c * QB, QB)
        attend([(0, scores(ksn_ref, off, nblk * QB, bfar), sel_mask(c, nblk), vst_ref[:, pl.ds(off, nblk * QB)])])

    def far_body(p, carry):
        far_tile(p * wide, wide)
        return carry
    lax.fori_loop(0, n_far // wide, far_body, 0)
    rem = n_far % wide

    @pl.when(rem >= 2)
    def _():
        far_tile(n_far - rem, 2)

    @pl.when(rem % 2 == 1)
    def _():
        far_tile(n_far - 1, 1)

    @pl.when(i >= 1)
    def _():
        off = pl.multiple_of((i - 1) * QB, QB)
        bias = jnp.concatenate([boff, bdiag], axis=0)
        after_first = lax.broadcasted_iota(jnp.int32, (2 * QB, QB), 0) - QB
        tcol = lax.broadcasted_iota(jnp.int32, (2 * QB, QB), 1)
        causal = after_first <= tcol
        s_sel = scores(ksn_ref, off, 2 * QB, bias)
        s_win = scores(kwn_ref, off, 2 * QB, bias)
        attend([(0, s_sel, sel_mask(i - 1, 2) & causal, vst_ref[:, pl.ds(off, 2 * QB)]),
                (1, s_win, causal, vwt_ref[:, pl.ds(off, 2 * QB)])])

    @pl.when(i == 0)
    def _():
        s_sel = scores(ksn_ref, 0, QB, bdiag)
        s_win = scores(kwn_ref, 0, QB, bdiag)
        attend([(0, s_sel, sel_mask(0, 1) & lower, vst_ref[:, 0:QB]),
                (1, s_win, lower, vwt_ref[:, 0:QB])])

    gate_t = jax.nn.sigmoid(gate_ref[...].T[:4 * R, :])
    for r, hs in enumerate(heads):
        g_cmp = gate_t[3 * r:3 * r + 1, :]
        g_sel = gate_t[3 * r + 1:3 * r + 2, :] * (1.0 / jnp.maximum(l_ref[r], TINY))
        g_win = gate_t[3 * r + 2:3 * r + 3, :] * (1.0 / jnp.maximum(l_ref[R + r], TINY))
        o_t = g_cmp * ocmp_ref[:, hs] + g_sel * acc_ref[0, :, hs] + g_win * acc_ref[1, :, hs]
        o_ref[:, hs] = o_t.T.astype(o_ref.dtype)


def nsa_attention(pa, pvt, kcmp, vcmp_t, pg, tables, q_gain, k_gain, bsz, seq):
    nqb = seq // Q_BLOCK
    G, R = NSA_KV_HEADS, NSA_GROUP
    W = R * Q_BLOCK
    bdiag, boff, bfar, bcmp, overlap_t = tables
    n_cmp_rows = seq // CMP_STRIDE

    kv_w = G * HEAD_DIM

    def k_spec(seg):
        return pl.BlockSpec((seq, kv_w), lambda b, i, s=_A_BLK[seg] // G: (b, s))

    def vt_spec(seg):
        return pl.BlockSpec((kv_w, seq), lambda b, i, s=_T_BLK[seg] // G: (s, b))

    def whole(x):
        return pl.BlockSpec(x.shape, lambda b, i: (0,) * x.ndim)

    return pl.pallas_call(
        functools.partial(_nsa_kernel, seq=seq),
        grid=(bsz, nqb),
        in_specs=[pl.BlockSpec((NSA_Q, Q_BLOCK), lambda b, i, s=_T_BLK[_Q_N] * LANE // NSA_Q: (s, b * nqb + i)),
                  pl.BlockSpec((1, n_cmp_rows, kv_w), lambda b, i: (b, 0, 0)),
                  pl.BlockSpec((1, kv_w, n_cmp_rows), lambda b, i: (b, 0, 0)),
                  k_spec(_KS), vt_spec(_VS), k_spec(_KW), vt_spec(_VW),
                  pl.BlockSpec((Q_BLOCK, G * LANE), lambda b, i, s=B_COLS // (G * LANE): (b * nqb + i, s)),
                  whole(bdiag), whole(boff), whole(bfar), whole(bcmp),
                  pl.BlockSpec((HEAD_DIM, Q_BLOCK), lambda b, i: (0, 0)),
                  pl.BlockSpec((3, HEAD_DIM), lambda b, i: (0, 0)),
                  pl.BlockSpec((LANE, LANE), lambda b, i: (0, 0))],
        out_specs=pl.BlockSpec((Q_BLOCK, NSA_Q), lambda b, i: (b * nqb + i, 0)),
        out_shape=jax.ShapeDtypeStruct((bsz * seq, NSA_Q), _MXU),
        scratch_shapes=[pltpu.VMEM((seq, kv_w), _MXU),
                        pltpu.VMEM((seq, kv_w), _MXU),
                        pltpu.VMEM((G, 2 * R, 1, Q_BLOCK), _F32),
                        pltpu.VMEM((G, 2 * R, 1, Q_BLOCK), _F32),
                        pltpu.VMEM((G, 2, HEAD_DIM, W), _F32),
                        pltpu.VMEM((G, HEAD_DIM, W), _F32),
                        pltpu.VMEM((G, seq // SEL_BLOCK, Q_BLOCK), _F32)],
        compiler_params=_params("parallel", "arbitrary"),
        name="nsa_attention",
    )(pvt, kcmp, vcmp_t, pa, pvt, pa, pvt, pg, bdiag, boff, bfar, bcmp,
      jnp.broadcast_to(q_gain.reshape(HEAD_DIM, 1), (HEAD_DIM, Q_BLOCK)), k_gain, overlap_t)


def _t5_bucket(dist):
    n = jnp.maximum(dist, 0)
    exact = REL_BUCKETS // 2
    big = exact + (jnp.log(jnp.maximum(n, 1).astype(jnp.float32) / exact)
                   / math.log(REL_MAX_DIST / exact) * (REL_BUCKETS - exact)).astype(jnp.int32)
    return jnp.where(n < exact, n, jnp.minimum(big, REL_BUCKETS - 1))


def nsa_tables(rel_bias, seq):
    G, R = NSA_KV_HEADS, NSA_GROUP
    nqb = seq // Q_BLOCK
    tab_h = rel_bias.T.astype(_F32)
    s = np.arange(Q_BLOCK)[:, None]
    t = np.arange(Q_BLOCK)[None, :]

    def group_tiles(x, lead):
        nl = len(lead)
        rows = x.shape[-2]
        x = x.reshape((G, R) + lead + (rows, Q_BLOCK))
        perm = (0,) + tuple(range(2, 2 + nl)) + (2 + nl, 1, 3 + nl)
        return x.transpose(perm).reshape((G,) + lead + (rows, R * Q_BLOCK))

    bdiag = group_tiles(jnp.take(tab_h, _t5_bucket(jnp.asarray(t - s)), axis=1), ())
    boff = group_tiles(jnp.take(tab_h, _t5_bucket(jnp.asarray(Q_BLOCK + t - s)), axis=1), ())
    bfar = group_tiles(jnp.take(tab_h, _t5_bucket(jnp.full((1, Q_BLOCK), 2 * Q_BLOCK)), axis=1), ())
    lead = (nqb - 1) * Q_BLOCK // CMP_STRIDE
    rows_c = -(-(seq // CMP_STRIDE + lead) // 8) * 8
    m = np.arange(rows_c)[:, None]
    dist_c = t - (CMP_STRIDE * (m - lead) + CMP_BLOCK - 1)
    bcmp = group_tiles(jnp.take(tab_h, _t5_bucket(jnp.asarray(dist_c)), axis=1), ())
    n_cmp = (seq - CMP_BLOCK) // CMP_STRIDE + 1
    n_sel = seq // SEL_BLOCK
    c_start = np.arange(LANE) * CMP_STRIDE
    s_start = np.arange(LANE) * SEL_BLOCK
    overlap = ((c_start[:, None] < s_start[None, :] + SEL_BLOCK)
               & (c_start[:, None] + CMP_BLOCK > s_start[None, :])
               & (np.arange(LANE)[:, None] < n_cmp) & (np.arange(LANE)[None, :] < n_sel))
    return bdiag, boff, bfar, bcmp, jnp.asarray(overlap.T.astype(np.float32)).astype(_MXU)


def _sb_kernel(q_ref, k_ref, vt_ref, o_ref, acc_ref, rest_ref, *, heads):
    QB = Q_BLOCK
    i = pl.program_id(2)
    scale = HEAD_DIM ** -0.5
    row = lax.broadcasted_iota(jnp.int32, (QB, QB), 0)
    col = lax.broadcasted_iota(jnp.int32, (QB, QB), 1)
    strict = row < col
    wide = 2

    def later_matrix(rows):
        r = lax.broadcasted_iota(jnp.int32, (rows, rows), 0)
        c = lax.broadcasted_iota(jnp.int32, (rows, rows), 1)
        return jnp.where(c > r, 1.0, 0.0).astype(_MXU)

    def block(c, first, nblk=1):
        rows = nblk * QB
        later = later_matrix(rows)
        off = pl.multiple_of(c * QB, QB)
        hss = [slice(h * HEAD_DIM, (h + 1) * HEAD_DIM) for h in range(heads)]
        zs = [_dot(k_ref[pl.ds(off, rows), hs], q_ref[hs, :]) * scale for hs in hss]
        log_bs, pieces, totals = [], [], []
        for z in zs:
            log_b, log_1mb = _log_sigmoid_pair(z)
            if first:
                log_1mb = jnp.where(strict, log_1mb, 0.0)
            log_bs.append(log_b)
            pieces.append(_split(log_1mb, 2))
            totals.append(jnp.sum(log_1mb, axis=0, keepdims=True))
        betweens = [_dot(later, p[0]) + _dot(later, p[1]) for p in pieces]
        probs = []
        for h in range(heads):
            if first:
                probs.append(jnp.where(strict, jnp.exp(log_bs[h] + betweens[h]), 0.0).astype(_MXU))
                rest_ref[h] = totals[h]
            else:
                rest = rest_ref[h]
                probs.append(jnp.exp(log_bs[h] + betweens[h] + rest).astype(_MXU))
                rest_ref[h] = rest + totals[h]
        upds = [_dot(vt_ref[hs, pl.ds(off, rows)], probs[h]) for h, hs in enumerate(hss)]
        for h in range(heads):
            acc_ref[h] = upds[h] if first else acc_ref[h] + upds[h]

    block(i, True)

    def body(n, carry):
        block(i - wide * (n + 1), False, wide)
        return carry
    lax.fori_loop(0, i // wide, body, 0)

    def tail(n, carry):
        block(i % wide - 1 - n, False)
        return carry
    lax.fori_loop(0, i % wide, tail, 0)
    for h in range(heads):
        o_ref[:, h * HEAD_DIM:(h + 1) * HEAD_DIM] = acc_ref[h].T.astype(o_ref.dtype)


def stick_breaking(pa, pvt, bsz, seq, heads=SB_HEADS):
    nqb = seq // Q_BLOCK
    w = heads * HEAD_DIM
    qb, kb, vb = _T_BLK[_Q_S] * LANE // w, _A_BLK[_K_S] * LANE // w, _T_BLK[_V_S] * LANE // w
    return pl.pallas_call(
        functools.partial(_sb_kernel, heads=heads),
        grid=(bsz, SB_HEADS // heads, nqb),
        in_specs=[pl.BlockSpec((w, Q_BLOCK), lambda b, h, i: (qb + h, b * nqb + i)),
                  pl.BlockSpec((seq, w), lambda b, h, i: (b, kb + h)),
                  pl.BlockSpec((w, seq), lambda b, h, i: (vb + h, b))],
        out_specs=pl.BlockSpec((Q_BLOCK, w), lambda b, h, i: (b * nqb + i, h)),
        out_shape=jax.ShapeDtypeStruct((bsz * seq, SB_W), _MXU),
        scratch_shapes=[pltpu.VMEM((heads, HEAD_DIM, Q_BLOCK), _F32),
                        pltpu.VMEM((heads, 1, Q_BLOCK), _F32)],
        compiler_params=_params("parallel", "parallel", "arbitrary"),
        name="stick_breaking",
    )(pvt, pa, pvt)


def _hgrn_kernel(q_ref, f_ref, i_ref, g_ref, llb_ref, l1lb_ref, omlb_ref, ng_ref, o_ref, state_ref, *, seq, heads):
    C, SUB = HG_CHUNK, HG_SUB
    nsub = C // SUB
    hss = [slice(h * HG_DK, (h + 1) * HG_DK) for h in range(heads)]
    row = lax.broadcasted_iota(jnp.int32, (C, C), 0)
    col = lax.broadcasted_iota(jnp.int32, (C, C), 1)
    upto = jnp.where(col <= row, 1.0, 0.0).astype(_MXU)
    sub_t = lax.broadcasted_iota(jnp.int32, (SUB, 1), 0)
    sub_c = lax.broadcasted_iota(jnp.int32, (SUB, C), 1)
    state_ref[...] = jnp.zeros(state_ref.shape, _F32)

    def chunk(n, carry):
        off = pl.multiple_of(n * C, C)
        fps = [f_ref[pl.ds(off, C), hs] for hs in hss]
        pieces = []
        for fp, hs in zip(fps, hss):
            log_sig, _ = _log_sigmoid_pair(fp)
            y = l1lb_ref[:, hs] + log_sig
            a = llb_ref[:, hs]
            log_f = jnp.maximum(a, y) + jnp.log1p(jnp.exp(-jnp.abs(a - y)))
            pieces.append(_split(log_f, 3))
        bs = [_dot(upto, p[0]) + _dot(upto, p[1]) + _dot(upto, p[2]) for p in pieces]
        qs = [q_ref[pl.ds(off, C), hs].astype(_F32) for hs in hss]
        ks = [omlb_ref[:, hs] * jax.nn.sigmoid(-fp) for fp, hs in zip(fps, hss)]
        vs = [i_ref[pl.ds(off, C), hs].astype(_F32) for hs in hss]

        outs = [_dot_nt((q * jnp.exp(b)).astype(_MXU), state_ref[h].astype(_MXU))
                for h, (q, b) in enumerate(zip(qs, bs))]

        lhs, rhs = [], []
        for q, k, b in zip(qs, ks, bs):
            for s_i in range(1, nsub):
                lo = s_i * SUB
                b_ref = b[lo - 1:lo, :]
                lhs.append((q[lo:lo + SUB, :] * jnp.exp(b[lo:lo + SUB, :] - b_ref)).astype(_MXU))
                rhs.append((k * jnp.exp(jnp.minimum(b_ref - b, 0.0))).astype(_MXU))
        a_blocks = [_dot_nt(x, y) for x, y in zip(lhs, rhs)]
        intra = []
        for h in range(heads):
            rows = [jnp.zeros((SUB, C), _F32)]
            for s_i in range(1, nsub):
                rows.append(jnp.where(sub_c < s_i * SUB, a_blocks[h * (nsub - 1) + s_i - 1], 0.0))
            intra.append(jnp.concatenate(rows, axis=0).astype(_MXU))
        outs = [o + _dot(a, v.astype(_MXU)) for o, a, v in zip(outs, intra, vs)]

        for h, (q, k, v, b) in enumerate(zip(qs, ks, vs, bs)):
            diag = []
            for s_i in range(nsub):
                lo = s_i * SUB
                b_i = b[lo:lo + SUB, :]
                q_i = q[lo:lo + SUB, :]
                o_d = jnp.zeros((SUB, HG_DV), _F32)
                for s in range(SUB):
                    w = jnp.exp(b_i - b[lo + s:lo + s + 1, :])
                    a_col = jnp.sum(q_i * k[lo + s:lo + s + 1, :] * w, axis=-1, keepdims=True)
                    a_col = jnp.where(sub_t >= s, a_col, 0.0)
                    o_d = o_d + a_col * v[lo + s:lo + s + 1, :]
                diag.append(o_d)
            outs[h] = outs[h] + jnp.concatenate(diag, axis=0)

        k_decs = [(k * jnp.exp(b[C - 1:C, :] - b)).astype(_MXU) for k, b in zip(ks, bs)]
        grown = [_dot(v.T.astype(_MXU), kd) for v, kd in zip(vs, k_decs)]
        for h, (b, hs) in enumerate(zip(bs, hss)):
            state_ref[h] = state_ref[h] * jnp.exp(b[C - 1:C, :]) + grown[h]
            gate = jax.nn.sigmoid(g_ref[pl.ds(off, C), hs])
            o_ref[pl.ds(off, C), hs] = (_rms(outs[h], ng_ref[...]) * gate).astype(o_ref.dtype)
        return carry

    lax.fori_loop(0, seq // C, chunk, 0)


def hgrn2(pa, pb, lb, norm_gain, bsz, seq, heads=4):
    lb = lb.reshape(1, HG_KW).astype(_F32)
    log_lb = jnp.log(lb)
    log_1mlb = jnp.log1p(-lb)
    one_mlb = 1.0 - lb
    w = heads * HG_DK

    def seq_spec(blk):
        return pl.BlockSpec((seq, w), lambda b, h, s=blk * LANE // w: (b, s + h))

    head_vec = pl.BlockSpec((1, w), lambda b, h: (0, h))
    return pl.pallas_call(
        functools.partial(_hgrn_kernel, seq=seq, heads=heads),
        grid=(bsz, HG_HEADS // heads),
        in_specs=[seq_spec(_A_BLK[_Q_H]), seq_spec(_B_BLK[_F_H]), seq_spec(_A_BLK[_I_H]), seq_spec(_B_BLK[_G_H]),
                  head_vec, head_vec, head_vec,
                  pl.BlockSpec((1, HG_DV), lambda b, h: (0, 0))],
        out_specs=pl.BlockSpec((seq, w), lambda b, h: (b, h)),
        out_shape=jax.ShapeDtypeStruct((bsz * seq, HG_VW), _MXU),
        scratch_shapes=[pltpu.VMEM((heads, HG_DV, HG_DK), _F32)],
        compiler_params=_params("parallel", "parallel"),
        name="hgrn2",
    )(pa, pb, pa, pb, log_lb, log_1mlb, one_mlb, norm_gain.reshape(1, HG_DV))


def _merge_kernel(o0_ref, o1_ref, o2_ref, w0_ref, w1_ref, w2_ref, g0_ref, g1_ref, g2_ref, out_ref):
    ys = [_dot(o_ref[...], w_ref[0]) for o_ref, w_ref in ((o0_ref, w0_ref), (o1_ref, w1_ref), (o2_ref, w2_ref))]
    acc = jax.nn.sigmoid(g0_ref[...].astype(_F32)) * ys[0]
    acc = acc + jax.nn.sigmoid(g1_ref[...].astype(_F32)) * ys[1]
    acc = acc + jax.nn.sigmoid(g2_ref[...].astype(_F32)) * ys[2]
    out_ref[...] = acc.astype(out_ref.dtype)


def merge_branches(o_nsa, o_sb, o_hg, w_branch, gates, tm=1024, tn=512):
    m = o_nsa.shape[0]
    nj = D_MODEL // tn
    gblk = 0

    def o_spec():
        return pl.BlockSpec((tm, BRANCH_W), lambda i, j: (i, 0))

    def w_spec(n):
        return pl.BlockSpec((1, BRANCH_W, tn), lambda i, j, n=n: (n, 0, j))

    def g_spec(n):
        return pl.BlockSpec((tm, tn), lambda i, j, n=n: (i, gblk + n * nj + j))

    return pl.pallas_call(
        _merge_kernel,
        grid=(m // tm, nj),
        in_specs=[o_spec(), o_spec(), o_spec(), w_spec(0), w_spec(1), w_spec(2), g_spec(0), g_spec(1), g_spec(2)],
        out_specs=pl.BlockSpec((tm, tn), lambda i, j: (i, j)),
        out_shape=jax.ShapeDtypeStruct((m, D_MODEL), _MXU),
        compiler_params=_params("parallel", "arbitrary"),
        name="merge_branches",
    )(o_nsa, o_sb, o_hg, w_branch, w_branch, w_branch, gates, gates, gates)


def _ffn_up_kernel(x_ref, halo_ref, gain_ref, wg_ref, wv_ref, cwg_ref, cwv_ref, cbg_ref, cbv_ref, o_ref, h_ref,
                   *, tiles_per_seq, sub):
    i = pl.program_id(0)

    @pl.when(pl.program_id(1) == 0)
    def _():
        h_ref[...] = _rms(x_ref[...], gain_ref[...]).astype(_MXU)

    first = (i % tiles_per_seq) == 0
    h = h_ref[...]
    halo = _rms(halo_ref[...], gain_ref[...]).astype(_MXU)
    tm = h.shape[0]
    hrows = halo.shape[0]
    row = lax.broadcasted_iota(jnp.int32, (tm, 1), 0)
    cols = [slice(c * sub, (c + 1) * sub) for c in range(wg_ref.shape[1] // sub)]

    prods = [[(_dot(h, w_ref[:, cs]), jnp.where(first, 0.0, _dot(halo, w_ref[:, cs]))) for w_ref in (wg_ref, wv_ref)]
             for cs in cols]

    def conv(u, uh, cw, cb):
        u1 = jnp.where(row == 0, uh[hrows - 1:hrows, :], pltpu.roll(u, 1, axis=0))
        u2 = jnp.where(row == 0, uh[hrows - 2:hrows - 1, :],
                       jnp.where(row == 1, uh[hrows - 1:hrows, :], pltpu.roll(u, 2, axis=0)))
        return cb + u2 * cw[0:1, :] + u1 * cw[1:2, :] + u * cw[2:3, :]

    for cs, ((ug, uhg), (uv, uhv)) in zip(cols, prods):
        gate = conv(ug, uhg, cwg_ref[:, cs], cbg_ref[:, cs])
        val = conv(uv, uhv, cwv_ref[:, cs], cbv_ref[:, cs])
        o_ref[:, cs] = (jax.nn.silu(gate) * val).astype(o_ref.dtype)


def ffn_up(x, gain, w_up, conv_w, conv_b, seq, tm=1024, tn=512, sub=256):
    m, k = x.shape
    nj = D_FF // tn
    hrows = 8
    conv_b = conv_b.reshape(1, 2 * D_FF)
    return pl.pallas_call(
        functools.partial(_ffn_up_kernel, tiles_per_seq=seq // tm, sub=sub),
        grid=(m // tm, nj),
        in_specs=[pl.BlockSpec((tm, k), lambda i, j: (i, 0)),
                  pl.BlockSpec((hrows, k), lambda i, j: (jnp.maximum(i * (tm // hrows) - 1, 0), 0)),
                  pl.BlockSpec((1, k), lambda i, j: (0, 0)),
                  pl.BlockSpec((k, tn), lambda i, j: (0, j)),
                  pl.BlockSpec((k, tn), lambda i, j: (0, nj + j)),
                  pl.BlockSpec((CONV_W, tn), lambda i, j: (0, j)),
                  pl.BlockSpec((CONV_W, tn), lambda i, j: (0, nj + j)),
                  pl.BlockSpec((1, tn), lambda i, j: (0, j)),
                  pl.BlockSpec((1, tn), lambda i, j: (0, nj + j))],
        out_specs=pl.BlockSpec((tm, tn), lambda i, j: (i, j)),
        out_shape=jax.ShapeDtypeStruct((m, D_FF), _MXU),
        scratch_shapes=[pltpu.VMEM((tm, k), _MXU)],
        compiler_params=_params("parallel", "arbitrary"),
        name="ffn_up_conv",
    )(x, x, gain.reshape(1, k), w_up, w_up, conv_w, conv_w, conv_b, conv_b)


def _gather_cols(w, segs):
    return jnp.concatenate([w[:, _SPLIT_OFF[s]:_SPLIT_OFF[s + 1]] for s in segs], axis=1)


def _nsa_gate_cols(w):
    g = w[:, _SPLIT_OFF[_G_N]:_SPLIT_OFF[_G_N + 1]].reshape(-1, NSA_KV_HEADS, NSA_GROUP * 3)
    g = jnp.pad(g, ((0, 0), (0, 0), (0, LANE - NSA_GROUP * 3)))
    return g.reshape(-1, NSA_KV_HEADS * LANE)


def kernel(x, norm_attn, w_in, nsa_q_gain, nsa_k_gain, cmp_pos, cmp_w1, cmp_w2, rel_bias, hg_lower_bound,
           hg_norm_gain, w_branch, w_out, norm_ffn, w_up, conv_w, conv_b, w_down):
    bsz, seq, d = x.shape
    m = bsz * seq
    lb_all = jnp.cumsum(jax.nn.softmax(hg_lower_bound.astype(_F32), axis=0), axis=0)
    lb_all = jnp.maximum(lb_all - lb_all[0:1], 0.0)
    tables = nsa_tables(rel_bias, seq)
    xf = x.reshape(m, d)
    for l in range(w_in.shape[0]):
        wl = w_in[l]
        h = rmsnorm_cast(xf, norm_attn[l])
        pa = matmul(h, _gather_cols(wl, _A_ORDER).astype(_MXU), _MXU, tm=512, tn=A_COLS // 2, name="proj_a")
        w_b = jnp.concatenate([_gather_cols(wl, _B_ORDER), _nsa_gate_cols(wl)], axis=1).astype(_MXU)
        pb = matmul(h, w_b, _F32, tn=w_b.shape[1] // 3, name="proj_b")
        gates = matmul(h, _gather_cols(wl, (_GATES,)).astype(_MXU), _MXU, tn=1024, name="proj_gates")
        pkc = matmul(h, _gather_cols(wl, (_KC,)).astype(_MXU), _MXU, name="proj_kc")
        pvc = matmul(h, _gather_cols(wl, (_VC,)).astype(_MXU), _MXU, name="proj_vc")
        blk_rows = seq // CMP_STRIDE
        kcmp = compress(pkc.reshape(bsz, blk_rows, CMP_STRIDE * NSA_KV), cmp_w1[l, 0], cmp_w2[l, 0], cmp_pos[l, 0],
                        nsa_k_gain[l, 0], True)
        vcmp_t = compress(pvc.reshape(bsz, blk_rows, CMP_STRIDE * NSA_KV), cmp_w1[l, 1], cmp_w2[l, 1], cmp_pos[l, 1],
                          nsa_k_gain[l, 0], False)
        pvt = matmul_nt(_gather_cols(wl, _T_ORDER).T.astype(_MXU), h, _MXU, name="proj_vt")
        o_nsa = nsa_attention(pa, pvt, kcmp, vcmp_t, pb, tables, nsa_q_gain[l], nsa_k_gain[l], bsz, seq)
        o_sb = stick_breaking(pa, pvt, bsz, seq)
        o_hg = hgrn2(pa, pb, lb_all[l], hg_norm_gain[l], bsz, seq)
        merged = merge_branches(o_nsa, o_sb, o_hg, w_branch[l].astype(_MXU), gates)
        xf = matmul(merged, w_out[l].astype(_MXU), _F32, residual=xf, tn=1024, name="out_proj")
        act = ffn_up(xf, norm_ffn[l], w_up[l].astype(_MXU), conv_w[l], conv_b[l], seq)
        xf = matmul(act, w_down[l].astype(_MXU), _F32, residual=xf, tm=512, tn=1024, name="ffn_down")
    return xf.reshape(bsz, seq, d)
```

```python
import functools
import math
import types

import numpy as np
import jax
import jax.numpy as jnp
from jax import lax
from jax.experimental import pallas as pl
from jax.experimental.pallas import tpu as pltpu

D_MODEL = 2048
DEPTH = 4
HEAD_DIM = 128
BRANCH_W = D_MODEL // 2
N_BRANCH = 3
NSA_HEADS = BRANCH_W // HEAD_DIM
NSA_KV_HEADS = NSA_HEADS // 4
NSA_GROUP = NSA_HEADS // NSA_KV_HEADS
CMP_BLOCK = 32
CMP_STRIDE = 16
SEL_BLOCK = 64
SEL_TOPK = 16
WINDOW = 512
Q_BLOCK = 128
SB_HEADS = BRANCH_W // HEAD_DIM
HG_DK = 128
HG_DV = 128
HG_HEADS = BRANCH_W // HG_DV
HG_CHUNK = 64
HG_SUB = 8
D_FF = 256 * ((8 * D_MODEL // 3 + 255) // 256)
CONV_W = 3
REL_BUCKETS = 32
REL_MAX_DIST = 128
EPS = 1e-6
NEG = -1e30
TINY = 1e-30

NSA_Q = NSA_HEADS * HEAD_DIM
NSA_KV = NSA_KV_HEADS * HEAD_DIM
NSA_GATE = 3 * NSA_HEADS
SB_W = SB_HEADS * HEAD_DIM
HG_KW = HG_HEADS * HG_DK
HG_VW = HG_HEADS * HG_DV
SPLIT_SIZES = (NSA_Q,) + (NSA_KV,) * 6 + (NSA_GATE,) + (SB_W,) * 3 + (HG_KW, HG_KW, HG_VW, HG_VW, N_BRANCH * D_MODEL)
(_Q_N, _KC, _VC, _KS, _VS, _KW, _VW, _G_N, _Q_S, _K_S, _V_S, _Q_H, _F_H, _I_H, _G_H, _GATES) = range(16)
_SPLIT_OFF = np.concatenate([[0], np.cumsum(SPLIT_SIZES)]).tolist()

LANE = 128
_A_ORDER = (_K_S, _Q_H, _I_H, _KS, _KW)
_A_BLK = {}
_off = 0
for _s in _A_ORDER:
    _A_BLK[_s] = _off // LANE
    _off += SPLIT_SIZES[_s]
A_COLS = _off
_T_ORDER = (_V_S, _Q_S, _Q_N, _VS, _VW)
_T_BLK = {}
_off = 0
for _s in _T_ORDER:
    _T_BLK[_s] = _off // LANE
    _off += SPLIT_SIZES[_s]
T_ROWS = _off
_B_ORDER = (_F_H, _G_H)
_B_BLK = {}
_off = 0
for _s in _B_ORDER:
    _B_BLK[_s] = _off // LANE
    _off += SPLIT_SIZES[_s]
B_COLS = _off

VMEM_LIMIT = 48 * 1024 * 1024

_MXU = jnp.bfloat16
_F32 = jnp.float32


def _dot(a, b):
    return jnp.dot(a, b, preferred_element_type=_F32)


def _dot_nt(a, b):
    return lax.dot_general(a, b, (((1,), (1,)), ((), ())), preferred_element_type=_F32)


def _split(x, parts):
    out = []
    r = x
    for _ in range(parts):
        h = r.astype(_MXU)
        out.append(h)
        r = r - h.astype(_F32)
    return out


def _params(*sem):
    return pltpu.CompilerParams(dimension_semantics=sem, vmem_limit_bytes=VMEM_LIMIT)


def _rms(x, gain):
    return x * lax.rsqrt(jnp.mean(x * x, axis=-1, keepdims=True) + EPS) * gain


def _log_sigmoid_pair(z):
    lp = jnp.log(1.0 + jnp.exp(-jnp.abs(z)))
    return jnp.minimum(z, 0.0) - lp, jnp.minimum(-z, 0.0) - lp


def _rmsnorm_kernel(x_ref, g_ref, o_ref):
    o_ref[...] = _rms(x_ref[...], g_ref[...]).astype(o_ref.dtype)


def rmsnorm_cast(x2d, gain):
    m, d = x2d.shape
    tm = 512
    return pl.pallas_call(
        _rmsnorm_kernel,
        grid=(m // tm,),
        in_specs=[pl.BlockSpec((tm, d), lambda i: (i, 0)),
                  pl.BlockSpec((1, d), lambda i: (0, 0))],
        out_specs=pl.BlockSpec((tm, d), lambda i: (i, 0)),
        out_shape=jax.ShapeDtypeStruct((m, d), _MXU),
        compiler_params=_params("parallel"),
        name="rmsnorm_cast",
    )(x2d, gain.reshape(1, d))


def _mm_kernel(a_ref, w_ref, o_ref):
    o_ref[...] = _dot(a_ref[...], w_ref[...]).astype(o_ref.dtype)


def _mm_res_kernel(a_ref, w_ref, r_ref, o_ref):
    o_ref[...] = (r_ref[...] + _dot(a_ref[...], w_ref[...])).astype(o_ref.dtype)


def matmul(a, w, out_dtype, residual=None, tm=1024, tn=512, name="matmul"):
    m, k = a.shape
    n = w.shape[1]
    tm = min(tm, m)
    tn = min(tn, n)
    in_specs = [pl.BlockSpec((tm, k), lambda i, j: (i, 0)),
                pl.BlockSpec((k, tn), lambda i, j: (0, j))]
    args = [a, w]
    kern = _mm_kernel
    if residual is not None:
        in_specs.append(pl.BlockSpec((tm, tn), lambda i, j: (i, j)))
        args.append(residual)
        kern = _mm_res_kernel
    return pl.pallas_call(
        kern,
        grid=(m // tm, n // tn),
        in_specs=in_specs,
        out_specs=pl.BlockSpec((tm, tn), lambda i, j: (i, j)),
        out_shape=jax.ShapeDtypeStruct((m, n), out_dtype),
        compiler_params=_params("parallel", "arbitrary"),
        name=name,
    )(*args)


def _mm_nt_kernel(wt_ref, a_ref, o_ref):
    o_ref[...] = _dot_nt(wt_ref[...], a_ref[...]).astype(o_ref.dtype)


def matmul_nt(wt, a, out_dtype, tm=1024, tn=512, name="matmul_nt"):
    n, k = wt.shape
    m = a.shape[0]
    tm = min(tm, m)
    tn = min(tn, n)
    return pl.pallas_call(
        _mm_nt_kernel,
        grid=(m // tm, n // tn),
        in_specs=[pl.BlockSpec((tn, k), lambda i, j: (j, 0)),
                  pl.BlockSpec((tm, k), lambda i, j: (i, 0))],
        out_specs=pl.BlockSpec((tn, tm), lambda i, j: (j, i)),
        out_shape=jax.ShapeDtypeStruct((n, m), out_dtype),
        compiler_params=_params("parallel", "arbitrary"),
        name=name,
    )(wt, a)


def _compress_kernel(x_ref, wa_ref, wb_ref, pa_ref, pb_ref, w2_ref, gain_ref, o_ref, *, normalize):
    x = x_ref[0].astype(_F32)
    u = _dot((x + pa_ref[...]).astype(_MXU), wa_ref[...])
    v = _dot((x + pb_ref[...]).astype(_MXU), wb_ref[...])
    hid = jax.nn.gelu(u + pltpu.roll(v, v.shape[0] - 1, axis=0)).astype(_MXU)
    if normalize:
        out = _dot(hid, w2_ref[...])
        out = jnp.concatenate(
            [_rms(out[:, g * HEAD_DIM:(g + 1) * HEAD_DIM], gain_ref[...]) for g in range(NSA_KV_HEADS)], axis=1)
    else:
        out = _dot_nt(w2_ref[...], hid)
    o_ref[0] = out.astype(o_ref.dtype)


def compress(x, w1, w2, pos, gain, normalize):
    bsz = x.shape[0]
    nrow = x.shape[1]
    kdim = x.shape[2]
    half = CMP_BLOCK // 2
    eye = jnp.eye(NSA_KV_HEADS, dtype=w1.dtype)

    def embed(w):
        return jnp.einsum('lde,gh->lgdhe', w, eye).reshape(kdim, NSA_KV).astype(_MXU)

    def tile_pos(p):
        return jnp.broadcast_to(p[:, None, :], (half, NSA_KV_HEADS, HEAD_DIM)).reshape(1, kdim)

    w2b = jnp.einsum('de,gh->gdhe', w2, eye).reshape(NSA_KV, NSA_KV).astype(_MXU)
    out_block = (1, nrow, NSA_KV)
    if not normalize:
        w2b = w2b.T
        out_block = (1, NSA_KV, nrow)
    return pl.pallas_call(
        functools.partial(_compress_kernel, normalize=normalize),
        grid=(bsz,),
        in_specs=[pl.BlockSpec((1, nrow, kdim), lambda b: (b, 0, 0)),
                  pl.BlockSpec((kdim, NSA_KV), lambda b: (0, 0)),
                  pl.BlockSpec((kdim, NSA_KV), lambda b: (0, 0)),
                  pl.BlockSpec((1, kdim), lambda b: (0, 0)),
                  pl.BlockSpec((1, kdim), lambda b: (0, 0)),
                  pl.BlockSpec((NSA_KV, NSA_KV), lambda b: (0, 0)),
                  pl.BlockSpec((1, HEAD_DIM), lambda b: (0, 0))],
        out_specs=pl.BlockSpec(out_block, lambda b: (b, 0, 0)),
        out_shape=jax.ShapeDtypeStruct((bsz,) + out_block[1:], _MXU),
        compiler_params=_params("parallel"),
        name="nsa_compress",
    )(x, embed(w1[:half]), embed(w1[half:]), tile_pos(pos[:half]), tile_pos(pos[half:]), w2b,
      gain.reshape(1, HEAD_DIM))


def _nsa_kernel(q_ref, kc_ref, vct_ref, ks_ref, vst_ref, kw_ref, vwt_ref, gate_ref,
                bdiag_ref, boff_ref, bfar_ref, bcmp_ref, qg_ref, kg_ref, ovt_ref,
                o_ref, ksn_ref, kwn_ref, m_ref, l_ref, acc_ref, ocmp_ref, sel_ref, *, seq):
    G, R, QB, D = NSA_KV_HEADS, NSA_GROUP, Q_BLOCK, HEAD_DIM
    W = R * QB
    i = pl.program_id(1)

    @pl.when(i == 0)
    def _():
        def body(c, carry):
            off = pl.multiple_of(c * QB, QB)
            for g in range(G):
                gs = slice(g * D, (g + 1) * D)
                ksn_ref[pl.ds(off, QB), gs] = _rms(ks_ref[pl.ds(off, QB), gs].astype(_F32), kg_ref[1:2, :]).astype(_MXU)
                kwn_ref[pl.ds(off, QB), gs] = _rms(kw_ref[pl.ds(off, QB), gs].astype(_F32), kg_ref[2:3, :]).astype(_MXU)
            return carry
        lax.fori_loop(0, seq // QB, body, 0)

    groups = []
    for g in range(G):
        gs = slice(g * D, (g + 1) * D)
        one = slice(g, g + 1)
        groups.append(types.SimpleNamespace(
            q=q_ref.at[g * R * D:(g + 1) * R * D, :], kc=kc_ref.at[:, :, gs], vct=vct_ref.at[:, gs, :],
            vst=vst_ref.at[gs, :], vwt=vwt_ref.at[gs, :], gate=gate_ref.at[:, gs],
            bdiag=bdiag_ref.at[one], boff=boff_ref.at[one], bfar=bfar_ref.at[one], bcmp=bcmp_ref.at[one],
            o=o_ref.at[:, g * W:(g + 1) * W], ksn=ksn_ref.at[:, gs], kwn=kwn_ref.at[:, gs],
            m=m_ref.at[g], l=l_ref.at[g], acc=acc_ref.at[g], ocmp=ocmp_ref.at[g], sel=sel_ref.at[g]))
    _nsa_groups(i, groups, qg_ref, ovt_ref, seq=seq)


def _nsa_groups(i, grp, qg_ref, ovt_ref, *, seq):
    R = NSA_GROUP
    QB = Q_BLOCK
    scale = HEAD_DIM ** -0.5
    sel_shift = int(math.log2(SEL_BLOCK))

    heads = [slice(r * QB, (r + 1) * QB) for r in range(R)]

    def norm_t(x):
        return x * lax.rsqrt(jnp.mean(x * x, axis=0, keepdims=True) + EPS) * qg_ref[...]

    for g in grp:
        g.q_all = jnp.concatenate(
            [norm_t(g.q[r * HEAD_DIM:(r + 1) * HEAD_DIM, :].astype(_F32)).astype(_MXU) for r in range(R)], axis=1)

    row = lax.broadcasted_iota(jnp.int32, (QB, QB), 0)
    col = lax.broadcasted_iota(jnp.int32, (QB, QB), 1)

    for g in grp:
        g.m[...] = jnp.full(g.m.shape, NEG, _F32)
        g.l[...] = jnp.zeros(g.l.shape, _F32)
        g.acc[...] = jnp.zeros(g.acc.shape, _F32)

    def attend_pre(jobs):
        staged = []
        for g, branch, s_all, mask, vt in jobs:
            p_parts, alphas = [], []
            for r, hs in enumerate(heads):
                slot = branch * R + r
                s = s_all[:, hs]
                if mask is not None:
                    s = jnp.where(mask, s, NEG)
                m_old = g.m[slot]
                m_new = jnp.maximum(m_old, jnp.max(s, axis=0, keepdims=True))
                alpha = jnp.exp(m_old - m_new)
                p = jnp.exp(s - m_new)
                if mask is not None:
                    p = jnp.where(mask, p, 0.0)
                g.l[slot] = alpha * g.l[slot] + jnp.sum(p, axis=0, keepdims=True)
                g.m[slot] = m_new
                p_parts.append(p.astype(_MXU))
                alphas.append(alpha)
            staged.append((g, branch, vt, jnp.concatenate(p_parts, axis=1), alphas))
        return staged

    def attend_post(staged):
        upds = [_dot(vt, p_all) for _, _, vt, p_all, _ in staged]
        for (g, branch, _, _, alphas), upd in zip(staged, upds):
            for r, hs in enumerate(heads):
                g.acc[branch, :, hs] = alphas[r] * g.acc[branch, :, hs] + upd[:, hs]

    def attend(jobs):
        attend_post(attend_pre(jobs))

    valid_c = i * QB + col >= CMP_STRIDE * row + (CMP_BLOCK - 1)
    shift = QB // CMP_STRIDE
    off_c = pl.multiple_of((seq // QB - 1 - i) * shift, shift)
    s_cs = [_dot(g.kc[0], g.q_all) * scale + g.bcmp[0, pl.ds(off_c, seq // CMP_STRIDE), :] for g in grp]

    nwb = WINDOW // QB
    rows_w = (nwb - 1) * QB
    start_w = jnp.maximum(i - nwb, 0)
    off_w = pl.multiple_of(start_w * QB, QB)
    key_w = start_w * QB + lax.broadcasted_iota(jnp.int32, (rows_w, QB), 0)
    qry_w = i * QB + lax.broadcasted_iota(jnp.int32, (rows_w, QB), 1)
    mask_w = (key_w > qry_w - WINDOW) & (key_w < (i - 1) * QB)
    s_ws = [_dot(g.kwn[pl.ds(off_w, rows_w), :], g.q_all) * scale + g.bfar[0] for g in grp]

    p_cmp, psums = [], []
    for s_c in s_cs:
        p_parts = []
        psum = jnp.zeros((QB, QB), _F32)
        for hs in heads:
            s = jnp.where(valid_c, s_c[:, hs], NEG)
            e = jnp.where(valid_c, jnp.exp(s - jnp.max(s, axis=0, keepdims=True)), 0.0)
            p = e * (1.0 / jnp.maximum(jnp.sum(e, axis=0, keepdims=True), TINY))
            psum = psum + p
            p_parts.append(p.astype(_MXU))
        p_cmp.append(jnp.concatenate(p_parts, axis=1))
        psums.append(_split(psum, 2))
    staged_w = attend_pre([(g, 1, s_w, mask_w, g.vwt[:, pl.ds(off_w, rows_w)]) for g, s_w in zip(grp, s_ws)])
    for g, p_all in zip(grp, p_cmp):
        g.ocmp[...] = _dot(g.vct[0], p_all)
    ovt = ovt_ref[...]
    imps = [_dot(ovt, hi) + _dot(ovt, lo) for hi, lo in psums]
    attend_post(staged_w)

    n_sel = seq // SEL_BLOCK
    jrow = lax.broadcasted_iota(jnp.int32, (n_sel, QB), 0)
    tcol = lax.broadcasted_iota(jnp.int32, (n_sel, QB), 1)
    qblk = lax.shift_right_arithmetic(i * QB + tcol, sel_shift)
    causal_b = jrow <= qblk
    forced = causal_b & ((jrow == 0) | (jrow >= qblk - 1))
    for g, imp in zip(grp, imps):
        score = jnp.where(forced, jnp.inf, jnp.where(causal_b, imp[:n_sel, :], -jnp.inf))
        rank = jnp.zeros((n_sel, QB), jnp.int32)
        for jj in range(n_sel):
            sc = score[jj:jj + 1, :]
            beats = (sc > score) | ((sc == score) & (jrow > jj))
            rank = rank + beats.astype(jnp.int32)
        g.sel[...] = jnp.where((rank < SEL_TOPK) & causal_b, 1.0, 0.0)

    def scores(g, k_ref, off, rows, bias):
        return _dot(k_ref[pl.ds(off, rows), :], g.q_all) * scale + bias

    def sel_mask(g, c, nblk):
        rix = lax.broadcasted_iota(jnp.int32, (nblk * QB, QB), 0)
        per_key_block = QB // SEL_BLOCK
        flags = [g.sel[pl.ds(per_key_block * c + j, 1), :] for j in range(per_key_block * nblk)]
        m = flags[-1]
        for j in reversed(range(len(flags) - 1)):
            m = jnp.where(rix < (j + 1) * SEL_BLOCK, flags[j], m)
        return m > 0.5

    lower = row <= col

    n_far = jnp.maximum(i - 1, 0)
    wide = 4

    def far_tile(c, nblk):
        off = pl.multiple_of(c * QB, QB)
        attend([(g, 0, scores(g, g.ksn, off, nblk * QB, g.bfar[0]), sel_mask(g, c, nblk),
                 g.vst[:, pl.ds(off, nblk * QB)]) for g in grp])

    def far_body(p, carry):
        far_tile(p * wide, wide)
        return carry
    lax.fori_loop(0, n_far // wide, far_body, 0)
    rem = n_far % wide

    @pl.when(rem >= 2)
    def _():
        far_tile(n_far - rem, 2)

    @pl.when(rem % 2 == 1)
    def _():
        far_tile(n_far - 1, 1)

    @pl.when(i >= 1)
    def _():
        off = pl.multiple_of((i - 1) * QB, QB)
        after_first = lax.broadcasted_iota(jnp.int32, (2 * QB, QB), 0) - QB
        tcol = lax.broadcasted_iota(jnp.int32, (2 * QB, QB), 1)
        causal = after_first <= tcol
        jobs = []
        for g in grp:
            bias = jnp.concatenate([g.boff[0], g.bdiag[0]], axis=0)
            jobs.append((g, 0, scores(g, g.ksn, off, 2 * QB, bias), None, g.vst[:, pl.ds(off, 2 * QB)]))
            jobs.append((g, 1, scores(g, g.kwn, off, 2 * QB, bias), causal, g.vwt[:, pl.ds(off, 2 * QB)]))
        jobs = [(g, br, s, (sel_mask(g, i - 1, 2) & causal) if br == 0 else m, vt) for g, br, s, m, vt in jobs]
        attend(jobs)

    @pl.when(i == 0)
    def _():
        jobs = []
        for g in grp:
            jobs.append((g, 0, scores(g, g.ksn, 0, QB, g.bdiag[0]), sel_mask(g, 0, 1) & lower, g.vst[:, 0:QB]))
            jobs.append((g, 1, scores(g, g.kwn, 0, QB, g.bdiag[0]), lower, g.vwt[:, 0:QB]))
        attend(jobs)

    for g in grp:
        gate_t = jax.nn.sigmoid(g.gate[...].T[:4 * R, :])
        for r, hs in enumerate(heads):
            g_cmp = gate_t[3 * r:3 * r + 1, :]
            g_sel = gate_t[3 * r + 1:3 * r + 2, :] * (1.0 / jnp.maximum(g.l[r], TINY))
            g_win = gate_t[3 * r + 2:3 * r + 3, :] * (1.0 / jnp.maximum(g.l[R + r], TINY))
            o_t = g_cmp * g.ocmp[:, hs] + g_sel * g.acc[0, :, hs] + g_win * g.acc[1, :, hs]
            g.o[:, hs] = o_t.T.astype(g.o.dtype)


def nsa_attention(pa, pvt, kcmp, vcmp_t, pg, tables, q_gain, k_gain, bsz, seq):
    nqb = seq // Q_BLOCK
    G, R = NSA_KV_HEADS, NSA_GROUP
    W = R * Q_BLOCK
    bdiag, boff, bfar, bcmp, overlap_t = tables
    n_cmp_rows = seq // CMP_STRIDE

    kv_w = G * HEAD_DIM

    def k_spec(seg):
        return pl.BlockSpec((seq, kv_w), lambda b, i, s=_A_BLK[seg] // G: (b, s))

    def vt_spec(seg):
        return pl.BlockSpec((kv_w, seq), lambda b, i, s=_T_BLK[seg] // G: (s, b))

    def whole(x):
        return pl.BlockSpec(x.shape, lambda b, i: (0,) * x.ndim)

    return pl.pallas_call(
        functools.partial(_nsa_kernel, seq=seq),
        grid=(bsz, nqb),
        in_specs=[pl.BlockSpec((NSA_Q, Q_BLOCK), lambda b, i, s=_T_BLK[_Q_N] * LANE // NSA_Q: (s, b * nqb + i)),
                  pl.BlockSpec((1, n_cmp_rows, kv_w), lambda b, i: (b, 0, 0)),
                  pl.BlockSpec((1, kv_w, n_cmp_rows), lambda b, i: (b, 0, 0)),
                  k_spec(_KS), vt_spec(_VS), k_spec(_KW), vt_spec(_VW),
                  pl.BlockSpec((Q_BLOCK, G * LANE), lambda b, i, s=B_COLS // (G * LANE): (b * nqb + i, s)),
                  whole(bdiag), whole(boff), whole(bfar), whole(bcmp),
                  pl.BlockSpec((HEAD_DIM, Q_BLOCK), lambda b, i: (0, 0)),
                  pl.BlockSpec((3, HEAD_DIM), lambda b, i: (0, 0)),
                  pl.BlockSpec((LANE, LANE), lambda b, i: (0, 0))],
        out_specs=pl.BlockSpec((Q_BLOCK, NSA_Q), lambda b, i: (b * nqb + i, 0)),
        out_shape=jax.ShapeDtypeStruct((bsz * seq, NSA_Q), _MXU),
        scratch_shapes=[pltpu.VMEM((seq, kv_w), _MXU),
                        pltpu.VMEM((seq, kv_w), _MXU),
                        pltpu.VMEM((G, 2 * R, 1, Q_BLOCK), _F32),
                        pltpu.VMEM((G, 2 * R, 1, Q_BLOCK), _F32),
                        pltpu.VMEM((G, 2, HEAD_DIM, W), _F32),
                        pltpu.VMEM((G, HEAD_DIM, W), _F32),
                        pltpu.VMEM((G, seq // SEL_BLOCK, Q_BLOCK), _F32)],
        compiler_params=_params("parallel", "arbitrary"),
        name="nsa_attention",
    )(pvt, kcmp, vcmp_t, pa, pvt, pa, pvt, pg, bdiag, boff, bfar, bcmp,
      jnp.broadcast_to(q_gain.reshape(HEAD_DIM, 1), (HEAD_DIM, Q_BLOCK)), k_gain, overlap_t)


def _t5_bucket(dist):
    n = jnp.maximum(dist, 0)
    exact = REL_BUCKETS // 2
    big = exact + (jnp.log(jnp.maximum(n, 1).astype(jnp.float32) / exact)
                   / math.log(REL_MAX_DIST / exact) * (REL_BUCKETS - exact)).astype(jnp.int32)
    return jnp.where(n < exact, n, jnp.minimum(big, REL_BUCKETS - 1))


def nsa_tables(rel_bias, seq):
    G, R = NSA_KV_HEADS, NSA_GROUP
    nqb = seq // Q_BLOCK
    tab_h = rel_bias.T.astype(_F32)
    s = np.arange(Q_BLOCK)[:, None]
    t = np.arange(Q_BLOCK)[None, :]

    def group_tiles(x, lead):
        nl = len(lead)
        rows = x.shape[-2]
        x = x.reshape((G, R) + lead + (rows, Q_BLOCK))
        perm = (0,) + tuple(range(2, 2 + nl)) + (2 + nl, 1, 3 + nl)
        return x.transpose(perm).reshape((G,) + lead + (rows, R * Q_BLOCK))

    bdiag = group_tiles(jnp.take(tab_h, _t5_bucket(jnp.asarray(t - s)), axis=1), ())
    boff = group_tiles(jnp.take(tab_h, _t5_bucket(jnp.asarray(Q_BLOCK + t - s)), axis=1), ())
    bfar = group_tiles(jnp.take(tab_h, _t5_bucket(jnp.full((1, Q_BLOCK), 2 * Q_BLOCK)), axis=1), ())
    lead = (nqb - 1) * Q_BLOCK // CMP_STRIDE
    rows_c = -(-(seq // CMP_STRIDE + lead) // 8) * 8
    m = np.arange(rows_c)[:, None]
    dist_c = t - (CMP_STRIDE * (m - lead) + CMP_BLOCK - 1)
    bcmp = group_tiles(jnp.take(tab_h, _t5_bucket(jnp.asarray(dist_c)), axis=1), ())
    n_cmp = (seq - CMP_BLOCK) // CMP_STRIDE + 1
    n_sel = seq // SEL_BLOCK
    c_start = np.arange(LANE) * CMP_STRIDE
    s_start = np.arange(LANE) * SEL_BLOCK
    overlap = ((c_start[:, None] < s_start[None, :] + SEL_BLOCK)
               & (c_start[:, None] + CMP_BLOCK > s_start[None, :])
               & (np.arange(LANE)[:, None] < n_cmp) & (np.arange(LANE)[None, :] < n_sel))
    return bdiag, boff, bfar, bcmp, jnp.asarray(overlap.T.astype(np.float32)).astype(_MXU)


def _sb_kernel(q_ref, k_ref, vt_ref, o_ref, acc_ref, rest_ref, *, heads):
    QB = Q_BLOCK
    i = pl.program_id(2)
    scale = HEAD_DIM ** -0.5
    row = lax.broadcasted_iota(jnp.int32, (QB, QB), 0)
    col = lax.broadcasted_iota(jnp.int32, (QB, QB), 1)
    strict = row < col
    wide = 2

    def later_matrix(rows):
        r = lax.broadcasted_iota(jnp.int32, (rows, rows), 0)
        c = lax.broadcasted_iota(jnp.int32, (rows, rows), 1)
        return jnp.where(c > r, 1.0, 0.0).astype(_MXU)

    def block(c, first, nblk=1):
        rows = nblk * QB
        later = later_matrix(rows)
        off = pl.multiple_of(c * QB, QB)
        hss = [slice(h * HEAD_DIM, (h + 1) * HEAD_DIM) for h in range(heads)]
        zs = [_dot(k_ref[pl.ds(off, rows), hs], q_ref[hs, :]) * scale for hs in hss]
        log_bs, pieces, totals = [], [], []
        for z in zs:
            log_b, log_1mb = _log_sigmoid_pair(z)
            if first:
                log_1mb = jnp.where(strict, log_1mb, 0.0)
            log_bs.append(log_b)
            pieces.append(_split(log_1mb, 2))
            totals.append(jnp.sum(log_1mb, axis=0, keepdims=True))
        betweens = [_dot(later, p[0]) + _dot(later, p[1]) for p in pieces]
        probs = []
        for h in range(heads):
            if first:
                probs.append(jnp.where(strict, jnp.exp(log_bs[h] + betweens[h]), 0.0).astype(_MXU))
                rest_ref[h] = totals[h]
            else:
                rest = rest_ref[h]
                probs.append(jnp.exp(log_bs[h] + betweens[h] + rest).astype(_MXU))
                rest_ref[h] = rest + totals[h]
        upds = [_dot(vt_ref[hs, pl.ds(off, rows)], probs[h]) for h, hs in enumerate(hss)]
        for h in range(heads):
            acc_ref[h] = upds[h] if first else acc_ref[h] + upds[h]

    block(i, True)

    def body(n, carry):
        block(i - wide * (n + 1), False, wide)
        return carry
    lax.fori_loop(0, i // wide, body, 0)

    def tail(n, carry):
        block(i % wide - 1 - n, False)
        return carry
    lax.fori_loop(0, i % wide, tail, 0)
    for h in range(heads):
        o_ref[:, h * HEAD_DIM:(h + 1) * HEAD_DIM] = acc_ref[h].T.astype(o_ref.dtype)


def stick_breaking(pa, pvt, bsz, seq, heads=SB_HEADS):
    nqb = seq // Q_BLOCK
    w = heads * HEAD_DIM
    qb, kb, vb = _T_BLK[_Q_S] * LANE // w, _A_BLK[_K_S] * LANE // w, _T_BLK[_V_S] * LANE // w
    return pl.pallas_call(
        functools.partial(_sb_kernel, heads=heads),
        grid=(bsz, SB_HEADS // heads, nqb),
        in_specs=[pl.BlockSpec((w, Q_BLOCK), lambda b, h, i: (qb + h, b * nqb + i)),
                  pl.BlockSpec((seq, w), lambda b, h, i: (b, kb + h)),
                  pl.BlockSpec((w, seq), lambda b, h, i: (vb + h, b))],
        out_specs=pl.BlockSpec((Q_BLOCK, w), lambda b, h, i: (b * nqb + i, h)),
        out_shape=jax.ShapeDtypeStruct((bsz * seq, SB_W), _MXU),
        scratch_shapes=[pltpu.VMEM((heads, HEAD_DIM, Q_BLOCK), _F32),
                        pltpu.VMEM((heads, 1, Q_BLOCK), _F32)],
        compiler_params=_params("parallel", "parallel", "arbitrary"),
        name="stick_breaking",
    )(pvt, pa, pvt)


def _hgrn_kernel(q_ref, f_ref, i_ref, g_ref, llb_ref, l1lb_ref, omlb_ref, ng_ref, o_ref, state_ref, *, seq, heads):
    C, SUB = HG_CHUNK, HG_SUB
    nsub = C // SUB
    hss = [slice(h * HG_DK, (h + 1) * HG_DK) for h in range(heads)]
    row = lax.broadcasted_iota(jnp.int32, (C, C), 0)
    col = lax.broadcasted_iota(jnp.int32, (C, C), 1)
    upto = jnp.where(col <= row, 1.0, 0.0).astype(_MXU)
    sub_t = lax.broadcasted_iota(jnp.int32, (SUB, 1), 0)
    sub_c = lax.broadcasted_iota(jnp.int32, (SUB, C), 1)
    state_ref[...] = jnp.zeros(state_ref.shape, _F32)

    def chunk(n, carry):
        off = pl.multiple_of(n * C, C)
        fps = [f_ref[pl.ds(off, C), hs] for hs in hss]
        pieces = []
        for fp, hs in zip(fps, hss):
            log_sig, _ = _log_sigmoid_pair(fp)
            y = l1lb_ref[:, hs] + log_sig
            a = llb_ref[:, hs]
            log_f = jnp.maximum(a, y) + jnp.log1p(jnp.exp(-jnp.abs(a - y)))
            pieces.append(_split(log_f, 3))
        bs = [_dot(upto, p[0]) + _dot(upto, p[1]) + _dot(upto, p[2]) for p in pieces]
        qs = [q_ref[pl.ds(off, C), hs].astype(_F32) for hs in hss]
        ks = [omlb_ref[:, hs] * jax.nn.sigmoid(-fp) for fp, hs in zip(fps, hss)]
        vs = [i_ref[pl.ds(off, C), hs].astype(_F32) for hs in hss]

        outs = [_dot_nt((q * jnp.exp(b)).astype(_MXU), state_ref[h].astype(_MXU))
                for h, (q, b) in enumerate(zip(qs, bs))]

        lhs, rhs = [], []
        for q, k, b in zip(qs, ks, bs):
            for s_i in range(1, nsub):
                lo = s_i * SUB
                b_ref = b[lo - 1:lo, :]
                lhs.append((q[lo:lo + SUB, :] * jnp.exp(b[lo:lo + SUB, :] - b_ref)).astype(_MXU))
                rhs.append((k * jnp.exp(jnp.minimum(b_ref - b, 0.0))).astype(_MXU))
        a_blocks = [_dot_nt(x, y) for x, y in zip(lhs, rhs)]
        intra = []
        for h in range(heads):
            rows = [jnp.zeros((SUB, C), _F32)]
            for s_i in range(1, nsub):
                rows.append(jnp.where(sub_c < s_i * SUB, a_blocks[h * (nsub - 1) + s_i - 1], 0.0))
            intra.append(jnp.concatenate(rows, axis=0).astype(_MXU))
        outs = [o + _dot(a, v.astype(_MXU)) for o, a, v in zip(outs, intra, vs)]

        for h, (q, k, v, b) in enumerate(zip(qs, ks, vs, bs)):
            diag = []
            for s_i in range(nsub):
                lo = s_i * SUB
                b_i = b[lo:lo + SUB, :]
                q_i = q[lo:lo + SUB, :]
                o_d = jnp.zeros((SUB, HG_DV), _F32)
                for s in range(SUB):
                    w = jnp.exp(b_i - b[lo + s:lo + s + 1, :])
                    a_col = jnp.sum(q_i * k[lo + s:lo + s + 1, :] * w, axis=-1, keepdims=True)
                    a_col = jnp.where(sub_t >= s, a_col, 0.0)
                    o_d = o_d + a_col * v[lo + s:lo + s + 1, :]
                diag.append(o_d)
            outs[h] = outs[h] + jnp.concatenate(diag, axis=0)

        k_decs = [(k * jnp.exp(b[C - 1:C, :] - b)).astype(_MXU) for k, b in zip(ks, bs)]
        grown = [_dot(v.T.astype(_MXU), kd) for v, kd in zip(vs, k_decs)]
        for h, (b, hs) in enumerate(zip(bs, hss)):
            state_ref[h] = state_ref[h] * jnp.exp(b[C - 1:C, :]) + grown[h]
            gate = jax.nn.sigmoid(g_ref[pl.ds(off, C), hs])
            o_ref[pl.ds(off, C), hs] = (_rms(outs[h], ng_ref[...]) * gate).astype(o_ref.dtype)
        return carry

    lax.fori_loop(0, seq // C, chunk, 0)


def hgrn2(pa, pb, lb, norm_gain, bsz, seq, heads=4):
    lb = lb.reshape(1, HG_KW).astype(_F32)
    log_lb = jnp.log(lb)
    log_1mlb = jnp.log1p(-lb)
    one_mlb = 1.0 - lb
    w = heads * HG_DK

    def seq_spec(blk):
        return pl.BlockSpec((seq, w), lambda b, h, s=blk * LANE // w: (b, s + h))

    head_vec = pl.BlockSpec((1, w), lambda b, h: (0, h))
    return pl.pallas_call(
        functools.partial(_hgrn_kernel, seq=seq, heads=heads),
        grid=(bsz, HG_HEADS // heads),
        in_specs=[seq_spec(_A_BLK[_Q_H]), seq_spec(_B_BLK[_F_H]), seq_spec(_A_BLK[_I_H]), seq_spec(_B_BLK[_G_H]),
                  head_vec, head_vec, head_vec,
                  pl.BlockSpec((1, HG_DV), lambda b, h: (0, 0))],
        out_specs=pl.BlockSpec((seq, w), lambda b, h: (b, h)),
        out_shape=jax.ShapeDtypeStruct((bsz * seq, HG_VW), _MXU),
        scratch_shapes=[pltpu.VMEM((heads, HG_DV, HG_DK), _F32)],
        compiler_params=_params("parallel", "parallel"),
        name="hgrn2",
    )(pa, pb, pa, pb, log_lb, log_1mlb, one_mlb, norm_gain.reshape(1, HG_DV))


def _merge_kernel(o0_ref, o1_ref, o2_ref, w0_ref, w1_ref, w2_ref, g0_ref, g1_ref, g2_ref, out_ref):
    ys = [_dot(o_ref[...], w_ref[0]) for o_ref, w_ref in ((o0_ref, w0_ref), (o1_ref, w1_ref), (o2_ref, w2_ref))]
    acc = jax.nn.sigmoid(g0_ref[...].astype(_F32)) * ys[0]
    acc = acc + jax.nn.sigmoid(g1_ref[...].astype(_F32)) * ys[1]
    acc = acc + jax.nn.sigmoid(g2_ref[...].astype(_F32)) * ys[2]
    out_ref[...] = acc.astype(out_ref.dtype)


def merge_branches(o_nsa, o_sb, o_hg, w_branch, gates, tm=1024, tn=512):
    m = o_nsa.shape[0]
    nj = D_MODEL // tn
    gblk = 0

    def o_spec():
        return pl.BlockSpec((tm, BRANCH_W), lambda i, j: (i, 0))

    def w_spec(n):
        return pl.BlockSpec((1, BRANCH_W, tn), lambda i, j, n=n: (n, 0, j))

    def g_spec(n):
        return pl.BlockSpec((tm, tn), lambda i, j, n=n: (i, gblk + n * nj + j))

    return pl.pallas_call(
        _merge_kernel,
        grid=(m // tm, nj),
        in_specs=[o_spec(), o_spec(), o_spec(), w_spec(0), w_spec(1), w_spec(2), g_spec(0), g_spec(1), g_spec(2)],
        out_specs=pl.BlockSpec((tm, tn), lambda i, j: (i, j)),
        out_shape=jax.ShapeDtypeStruct((m, D_MODEL), _MXU),
        compiler_params=_params("parallel", "arbitrary"),
        name="merge_branches",
    )(o_nsa, o_sb, o_hg, w_branch, w_branch, w_branch, gates, gates, gates)


def _ffn_up_kernel(x_ref, halo_ref, gain_ref, wg_ref, wv_ref, cwg_ref, cwv_ref, cbg_ref, cbv_ref, o_ref, h_ref,
                   *, tiles_per_seq, sub):
    i = pl.program_id(0)

    @pl.when(pl.program_id(1) == 0)
    def _():
        h_ref[...] = _rms(x_ref[...], gain_ref[...]).astype(_MXU)

    first = (i % tiles_per_seq) == 0
    h = h_ref[...]
    halo = _rms(halo_ref[...], gain_ref[...]).astype(_MXU)
    tm = h.shape[0]
    hrows = halo.shape[0]
    row = lax.broadcasted_iota(jnp.int32, (tm, 1), 0)
    cols = [slice(c * sub, (c + 1) * sub) for c in range(wg_ref.shape[1] // sub)]

    prods = [[(_dot(h, w_ref[:, cs]), jnp.where(first, 0.0, _dot(halo, w_ref[:, cs]))) for w_ref in (wg_ref, wv_ref)]
             for cs in cols]

    def conv(u, uh, cw, cb):
        u1 = jnp.where(row == 0, uh[hrows - 1:hrows, :], pltpu.roll(u, 1, axis=0))
        u2 = jnp.where(row == 0, uh[hrows - 2:hrows - 1, :],
                       jnp.where(row == 1, uh[hrows - 1:hrows, :], pltpu.roll(u, 2, axis=0)))
        return cb + u2 * cw[0:1, :] + u1 * cw[1:2, :] + u * cw[2:3, :]

    for cs, ((ug, uhg), (uv, uhv)) in zip(cols, prods):
        gate = conv(ug, uhg, cwg_ref[:, cs], cbg_ref[:, cs])
        val = conv(uv, uhv, cwv_ref[:, cs], cbv_ref[:, cs])
        o_ref[:, cs] = (jax.nn.silu(gate) * val).astype(o_ref.dtype)


def ffn_up(x, gain, w_up, conv_w, conv_b, seq, tm=1024, tn=512, sub=256):
    m, k = x.shape
    nj = D_FF // tn
    hrows = 8
    conv_b = conv_b.reshape(1, 2 * D_FF)
    return pl.pallas_call(
        functools.partial(_ffn_up_kernel, tiles_per_seq=seq // tm, sub=sub),
        grid=(m // tm, nj),
        in_specs=[pl.BlockSpec((tm, k), lambda i, j: (i, 0)),
                  pl.BlockSpec((hrows, k), lambda i, j: (jnp.maximum(i * (tm // hrows) - 1, 0), 0)),
                  pl.BlockSpec((1, k), lambda i, j: (0, 0)),
                  pl.BlockSpec((k, tn), lambda i, j: (0, j)),
                  pl.BlockSpec((k, tn), lambda i, j: (0, nj + j)),
                  pl.BlockSpec((CONV_W, tn), lambda i, j: (0, j)),
                  pl.BlockSpec((CONV_W, tn), lambda i, j: (0, nj + j)),
                  pl.BlockSpec((1, tn), lambda i, j: (0, j)),
                  pl.BlockSpec((1, tn), lambda i, j: (0, nj + j))],
        out_specs=pl.BlockSpec((tm, tn), lambda i, j: (i, j)),
        out_shape=jax.ShapeDtypeStruct((m, D_FF), _MXU),
        scratch_shapes=[pltpu.VMEM((tm, k), _MXU)],
        compiler_params=_params("parallel", "arbitrary"),
        name="ffn_up_conv",
    )(x, x, gain.reshape(1, k), w_up, w_up, conv_w, conv_w, conv_b, conv_b)


def _gather_cols(w, segs):
    return jnp.concatenate([w[:, _SPLIT_OFF[s]:_SPLIT_OFF[s + 1]] for s in segs], axis=1)


def _nsa_gate_cols(w):
    g = w[:, _SPLIT_OFF[_G_N]:_SPLIT_OFF[_G_N + 1]].reshape(-1, NSA_KV_HEADS, NSA_GROUP * 3)
    g = jnp.pad(g, ((0, 0), (0, 0), (0, LANE - NSA_GROUP * 3)))
    return g.reshape(-1, NSA_KV_HEADS * LANE)


def kernel(x, norm_attn, w_in, nsa_q_gain, nsa_k_gain, cmp_pos, cmp_w1, cmp_w2, rel_bias, hg_lower_bound,
           hg_norm_gain, w_branch, w_out, norm_ffn, w_up, conv_w, conv_b, w_down):
    bsz, seq, d = x.shape
    m = bsz * seq
    lb_all = jnp.cumsum(jax.nn.softmax(hg_lower_bound.astype(_F32), axis=0), axis=0)
    lb_all = jnp.maximum(lb_all - lb_all[0:1], 0.0)
    tables = nsa_tables(rel_bias, seq)
    xf = x.reshape(m, d)
    for l in range(w_in.shape[0]):
        wl = w_in[l]
        h = rmsnorm_cast(xf, norm_attn[l])
        pa = matmul(h, _gather_cols(wl, _A_ORDER).astype(_MXU), _MXU, tm=512, tn=A_COLS // 2, name="proj_a")
        w_b = jnp.concatenate([_gather_cols(wl, _B_ORDER), _nsa_gate_cols(wl)], axis=1).astype(_MXU)
        pb = matmul(h, w_b, _F32, tn=w_b.shape[1] // 3, name="proj_b")
        gates = matmul(h, _gather_cols(wl, (_GATES,)).astype(_MXU), _MXU, tn=1024, name="proj_gates")
        pkc = matmul(h, _gather_cols(wl, (_KC,)).astype(_MXU), _MXU, name="proj_kc")
        pvc = matmul(h, _gather_cols(wl, (_VC,)).astype(_MXU), _MXU, name="proj_vc")
        blk_rows = seq // CMP_STRIDE
        kcmp = compress(pkc.reshape(bsz, blk_rows, CMP_STRIDE * NSA_KV), cmp_w1[l, 0], cmp_w2[l, 0], cmp_pos[l, 0],
                        nsa_k_gain[l, 0], True)
        vcmp_t = compress(pvc.reshape(bsz, blk_rows, CMP_STRIDE * NSA_KV), cmp_w1[l, 1], cmp_w2[l, 1], cmp_pos[l, 1],
                          nsa_k_gain[l, 0], False)
        pvt = matmul_nt(_gather_cols(wl, _T_ORDER).T.astype(_MXU), h, _MXU, name="proj_vt")
        o_nsa = nsa_attention(pa, pvt, kcmp, vcmp_t, pb, tables, nsa_q_gain[l], nsa_k_gain[l], bsz, seq)
        o_sb = stick_breaking(pa, pvt, bsz, seq)
        o_hg = hgrn2(pa, pb, lb_all[l], hg_norm_gain[l], bsz, seq)
        merged = merge_branches(o_nsa, o_sb, o_hg, w_branch[l].astype(_MXU), gates)
        xf = matmul(merged, w_out[l].astype(_MXU), _F32, residual=xf, tn=1024, name="out_proj")
        act = ffn_up(xf, norm_ffn[l], w_up[l].astype(_MXU), conv_w[l], conv_b[l], seq)
        xf = matmul(act, w_down[l].astype(_MXU), _F32, residual=xf, tm=512, tn=1024, name="ffn_down")
    return xf.reshape(bsz, seq, d)
```

```python
import functools
import math
import types

import numpy as np
import jax
import jax.numpy as jnp
from jax import lax
from jax.experimental import pallas as pl
from jax.experimental.pallas import tpu as pltpu

D_MODEL = 2048
DEPTH = 4
HEAD_DIM = 128
BRANCH_W = D_MODEL // 2
N_BRANCH = 3
NSA_HEADS = BRANCH_W // HEAD_DIM
NSA_KV_HEADS = NSA_HEADS // 4
NSA_GROUP = NSA_HEADS // NSA_KV_HEADS
CMP_BLOCK = 32
CMP_STRIDE = 16
SEL_BLOCK = 64
SEL_TOPK = 16
WINDOW = 512
Q_BLOCK = 128
SB_HEADS = BRANCH_W // HEAD_DIM
HG_DK = 128
HG_DV = 128
HG_HEADS = BRANCH_W // HG_DV
HG_CHUNK = 64
HG_SUB = 8
D_FF = 256 * ((8 * D_MODEL // 3 + 255) // 256)
CONV_W = 3
REL_BUCKETS = 32
REL_MAX_DIST = 128
EPS = 1e-6
NEG = -1e30
TINY = 1e-30

NSA_Q = NSA_HEADS * HEAD_DIM
NSA_KV = NSA_KV_HEADS * HEAD_DIM
NSA_GATE = 3 * NSA_HEADS
SB_W = SB_HEADS * HEAD_DIM
HG_KW = HG_HEADS * HG_DK
HG_VW = HG_HEADS * HG_DV
SPLIT_SIZES = (NSA_Q,) + (NSA_KV,) * 6 + (NSA_GATE,) + (SB_W,) * 3 + (HG_KW, HG_KW, HG_VW, HG_VW, N_BRANCH * D_MODEL)
(_Q_N, _KC, _VC, _KS, _VS, _KW, _VW, _G_N, _Q_S, _K_S, _V_S, _Q_H, _F_H, _I_H, _G_H, _GATES) = range(16)
_SPLIT_OFF = np.concatenate([[0], np.cumsum(SPLIT_SIZES)]).tolist()

LANE = 128
_A_ORDER = (_K_S, _Q_H, _I_H, _KS, _KW)
_A_BLK = {}
_off = 0
for _s in _A_ORDER:
    _A_BLK[_s] = _off // LANE
    _off += SPLIT_SIZES[_s]
A_COLS = _off
_T_ORDER = (_V_S, _Q_S, _Q_N, _VS, _VW)
_T_BLK = {}
_off = 0
for _s in _T_ORDER:
    _T_BLK[_s] = _off // LANE
    _off += SPLIT_SIZES[_s]
T_ROWS = _off
_B_ORDER = (_F_H, _G_H)
_B_BLK = {}
_off = 0
for _s in _B_ORDER:
    _B_BLK[_s] = _off // LANE
    _off += SPLIT_SIZES[_s]
B_COLS = _off

VMEM_LIMIT = 48 * 1024 * 1024

_MXU = jnp.bfloat16
_F32 = jnp.float32


def _dot(a, b):
    return jnp.dot(a, b, preferred_element_type=_F32)


def _dot_nt(a, b):
    return lax.dot_general(a, b, (((1,), (1,)), ((), ())), preferred_element_type=_F32)


def _split(x, parts):
    out = []
    r = x
    for _ in range(parts):
        h = r.astype(_MXU)
        out.append(h)
        r = r - h.astype(_F32)
    return out


def _params(*sem):
    return pltpu.CompilerParams(dimension_semantics=sem, vmem_limit_bytes=VMEM_LIMIT)


def _rms(x, gain):
    return x * lax.rsqrt(jnp.mean(x * x, axis=-1, keepdims=True) + EPS) * gain


def _log_sigmoid_pair(z):
    lp = jnp.log(1.0 + jnp.exp(-jnp.abs(z)))
    return jnp.minimum(z, 0.0) - lp, jnp.minimum(-z, 0.0) - lp


def _rmsnorm_kernel(x_ref, g_ref, o_ref):
    o_ref[...] = _rms(x_ref[...], g_ref[...]).astype(o_ref.dtype)


def rmsnorm_cast(x2d, gain):
    m, d = x2d.shape
    tm = 512
    return pl.pallas_call(
        _rmsnorm_kernel,
        grid=(m // tm,),
        in_specs=[pl.BlockSpec((tm, d), lambda i: (i, 0)),
                  pl.BlockSpec((1, d), lambda i: (0, 0))],
        out_specs=pl.BlockSpec((tm, d), lambda i: (i, 0)),
        out_shape=jax.ShapeDtypeStruct((m, d), _MXU),
        compiler_params=_params("parallel"),
        name="rmsnorm_cast",
    )(x2d, gain.reshape(1, d))


def _mm_kernel(a_ref, w_ref, o_ref):
    o_ref[...] = _dot(a_ref[...], w_ref[...]).astype(o_ref.dtype)


def _mm_res_kernel(a_ref, w_ref, r_ref, o_ref):
    o_ref[...] = (r_ref[...] + _dot(a_ref[...], w_ref[...])).astype(o_ref.dtype)


def matmul(a, w, out_dtype, residual=None, tm=1024, tn=512, name="matmul"):
    m, k = a.shape
    n = w.shape[1]
    tm = min(tm, m)
    tn = min(tn, n)
    in_specs = [pl.BlockSpec((tm, k), lambda i, j: (i, 0)),
                pl.BlockSpec((k, tn), lambda i, j: (0, j))]
    args = [a, w]
    kern = _mm_kernel
    if residual is not None:
        in_specs.append(pl.BlockSpec((tm, tn), lambda i, j: (i, j)))
        args.append(residual)
        kern = _mm_res_kernel
    return pl.pallas_call(
        kern,
        grid=(m // tm, n // tn),
        in_specs=in_specs,
        out_specs=pl.BlockSpec((tm, tn), lambda i, j: (i, j)),
        out_shape=jax.ShapeDtypeStruct((m, n), out_dtype),
        compiler_params=_params("parallel", "arbitrary"),
        name=name,
    )(*args)


def _mm_nt_kernel(wt_ref, a_ref, o_ref):
    o_ref[...] = _dot_nt(wt_ref[...], a_ref[...]).astype(o_ref.dtype)


def matmul_nt(wt, a, out_dtype, tm=1024, tn=512, name="matmul_nt"):
    n, k = wt.shape
    m = a.shape[0]
    tm = min(tm, m)
    tn = min(tn, n)
    return pl.pallas_call(
        _mm_nt_kernel,
        grid=(m // tm, n // tn),
        in_specs=[pl.BlockSpec((tn, k), lambda i, j: (j, 0)),
                  pl.BlockSpec((tm, k), lambda i, j: (i, 0))],
        out_specs=pl.BlockSpec((tn, tm), lambda i, j: (j, i)),
        out_shape=jax.ShapeDtypeStruct((n, m), out_dtype),
        compiler_params=_params("parallel", "arbitrary"),
        name=name,
    )(wt, a)


def _compress_kernel(x_ref, wa_ref, wb_ref, pa_ref, pb_ref, w2_ref, gain_ref, o_ref, *, normalize):
    x = x_ref[0].astype(_F32)
    u = _dot((x + pa_ref[...]).astype(_MXU), wa_ref[...])
    v = _dot((x + pb_ref[...]).astype(_MXU), wb_ref[...])
    hid = jax.nn.gelu(u + pltpu.roll(v, v.shape[0] - 1, axis=0)).astype(_MXU)
    if normalize:
        out = _dot(hid, w2_ref[...])
        out = jnp.concatenate(
            [_rms(out[:, g * HEAD_DIM:(g + 1) * HEAD_DIM], gain_ref[...]) for g in range(NSA_KV_HEADS)], axis=1)
    else:
        out = _dot_nt(w2_ref[...], hid)
    o_ref[0] = out.astype(o_ref.dtype)


def compress(x, w1, w2, pos, gain, normalize):
    bsz = x.shape[0]
    nrow = x.shape[1]
    kdim = x.shape[2]
    half = CMP_BLOCK // 2
    eye = jnp.eye(NSA_KV_HEADS, dtype=w1.dtype)

    def embed(w):
        return jnp.einsum('lde,gh->lgdhe', w, eye).reshape(kdim, NSA_KV).astype(_MXU)

    def tile_pos(p):
        return jnp.broadcast_to(p[:, None, :], (half, NSA_KV_HEADS, HEAD_DIM)).reshape(1, kdim)

    w2b = jnp.einsum('de,gh->gdhe', w2, eye).reshape(NSA_KV, NSA_KV).astype(_MXU)
    out_block = (1, nrow, NSA_KV)
    if not normalize:
        w2b = w2b.T
        out_block = (1, NSA_KV, nrow)
    return pl.pallas_call(
        functools.partial(_compress_kernel, normalize=normalize),
        grid=(bsz,),
        in_specs=[pl.BlockSpec((1, nrow, kdim), lambda b: (b, 0, 0)),
                  pl.BlockSpec((kdim, NSA_KV), lambda b: (0, 0)),
                  pl.BlockSpec((kdim, NSA_KV), lambda b: (0, 0)),
                  pl.BlockSpec((1, kdim), lambda b: (0, 0)),
                  pl.BlockSpec((1, kdim), lambda b: (0, 0)),
                  pl.BlockSpec((NSA_KV, NSA_KV), lambda b: (0, 0)),
                  pl.BlockSpec((1, HEAD_DIM), lambda b: (0, 0))],
        out_specs=pl.BlockSpec(out_block, lambda b: (b, 0, 0)),
        out_shape=jax.ShapeDtypeStruct((bsz,) + out_block[1:], _MXU),
        compiler_params=_params("parallel"),
        name="nsa_compress",
    )(x, embed(w1[:half]), embed(w1[half:]), tile_pos(pos[:half]), tile_pos(pos[half:]), w2b,
      gain.reshape(1, HEAD_DIM))


def _nsa_kernel(q_ref, kc_ref, vct_ref, ks_ref, vst_ref, kw_ref, vwt_ref, gate_ref,
                bdiag_ref, boff_ref, bfar_ref, bcmp_ref, qg_ref, kg_ref, ovt_ref,
                o_ref, ksn_ref, kwn_ref, m_ref, l_ref, acc_ref, ocmp_ref, sel_ref, *, seq):
    G, R, QB, D = NSA_KV_HEADS, NSA_GROUP, Q_BLOCK, HEAD_DIM
    W = R * QB
    i = pl.program_id(1)

    @pl.when(i == 0)
    def _():
        def body(c, carry):
            off = pl.multiple_of(c * QB, QB)
            for g in range(G):
                gs = slice(g * D, (g + 1) * D)
                ksn_ref[pl.ds(off, QB), gs] = _rms(ks_ref[pl.ds(off, QB), gs].astype(_F32), kg_ref[1:2, :]).astype(_MXU)
                kwn_ref[pl.ds(off, QB), gs] = _rms(kw_ref[pl.ds(off, QB), gs].astype(_F32), kg_ref[2:3, :]).astype(_MXU)
            return carry
        lax.fori_loop(0, seq // QB, body, 0)

    groups = []
    for g in range(G):
        gs = slice(g * D, (g + 1) * D)
        one = slice(g, g + 1)
        groups.append(types.SimpleNamespace(
            q=q_ref.at[g * R * D:(g + 1) * R * D, :], kc=kc_ref.at[:, :, gs], vct=vct_ref.at[:, gs, :],
            vst=vst_ref.at[gs, :], vwt=vwt_ref.at[gs, :], gate=gate_ref.at[:, gs],
            bdiag=bdiag_ref.at[one], boff=boff_ref.at[one], bfar=bfar_ref.at[one], bcmp=bcmp_ref.at[one],
            o=o_ref.at[:, g * W:(g + 1) * W], ksn=ksn_ref.at[:, gs], kwn=kwn_ref.at[:, gs],
            m=m_ref.at[g], l=l_ref.at[g], acc=acc_ref.at[g], ocmp=ocmp_ref.at[g], sel=sel_ref.at[g]))
    _nsa_groups(i, groups, qg_ref, ovt_ref, seq=seq)


def _nsa_groups(i, grp, qg_ref, ovt_ref, *, seq):
    R = NSA_GROUP
    QB = Q_BLOCK
    scale = HEAD_DIM ** -0.5
    sel_shift = int(math.log2(SEL_BLOCK))

    heads = [slice(r * QB, (r + 1) * QB) for r in range(R)]

    def norm_t(x):
        return x * lax.rsqrt(jnp.mean(x * x, axis=0, keepdims=True) + EPS) * qg_ref[...]

    for g in grp:
        g.q_all = jnp.concatenate(
            [norm_t(g.q[r * HEAD_DIM:(r + 1) * HEAD_DIM, :].astype(_F32)).astype(_MXU) for r in range(R)], axis=1)

    row = lax.broadcasted_iota(jnp.int32, (QB, QB), 0)
    col = lax.broadcasted_iota(jnp.int32, (QB, QB), 1)

    for g in grp:
        g.m[...] = jnp.full(g.m.shape, NEG, _F32)
        g.l[...] = jnp.zeros(g.l.shape, _F32)
        g.acc[...] = jnp.zeros(g.acc.shape, _F32)

    def attend_pre(jobs):
        staged = []
        for g, branch, s_all, mask, vt in jobs:
            p_parts, alphas = [], []
            for r, hs in enumerate(heads):
                slot = branch * R + r
                s = s_all[:, hs]
                if mask is not None:
                    s = jnp.where(mask, s, NEG)
                m_old = g.m[slot]
                m_new = jnp.maximum(m_old, jnp.max(s, axis=0, keepdims=True))
                alpha = jnp.exp(m_old - m_new)
                p = jnp.exp(s - m_new)
                if mask is not None:
                    p = jnp.where(mask, p, 0.0)
                g.l[slot] = alpha * g.l[slot] + jnp.sum(p, axis=0, keepdims=True)
                g.m[slot] = m_new
                p_parts.append(p.astype(_MXU))
                alphas.append(alpha)
            staged.append((g, branch, vt, jnp.concatenate(p_parts, axis=1), alphas))
        return staged

    def attend_post(staged):
        upds = [_dot(vt, p_all) for _, _, vt, p_all, _ in staged]
        for (g, branch, _, _, alphas), upd in zip(staged, upds):
            for r, hs in enumerate(heads):
                g.acc[branch, :, hs] = alphas[r] * g.acc[branch, :, hs] + upd[:, hs]

    def attend(jobs):
        attend_post(attend_pre(jobs))

    valid_c = i * QB + col >= CMP_STRIDE * row + (CMP_BLOCK - 1)
    shift = QB // CMP_STRIDE
    off_c = pl.multiple_of((seq // QB - 1 - i) * shift, shift)
    s_cs = [_dot(g.kc[0], g.q_all) * scale + g.bcmp[0, pl.ds(off_c, seq // CMP_STRIDE), :] for g in grp]

    nwb = WINDOW // QB
    rows_w = (nwb - 1) * QB
    start_w = jnp.maximum(i - nwb, 0)
    off_w = pl.multiple_of(start_w * QB, QB)
    key_w = start_w * QB + lax.broadcasted_iota(jnp.int32, (rows_w, QB), 0)
    qry_w = i * QB + lax.broadcasted_iota(jnp.int32, (rows_w, QB), 1)
    mask_w = (key_w > qry_w - WINDOW) & (key_w < (i - 1) * QB)
    s_ws = [_dot(g.kwn[pl.ds(off_w, rows_w), :], g.q_all) * scale + g.bfar[0] for g in grp]

    p_cmp, psums = [], []
    for s_c in s_cs:
        p_parts = []
        psum = jnp.zeros((QB, QB), _F32)
        for hs in heads:
            s = jnp.where(valid_c, s_c[:, hs], NEG)
            e = jnp.where(valid_c, jnp.exp(s - jnp.max(s, axis=0, keepdims=True)), 0.0)
            p = e * (1.0 / jnp.maximum(jnp.sum(e, axis=0, keepdims=True), TINY))
            psum = psum + p
            p_parts.append(p.astype(_MXU))
        p_cmp.append(jnp.concatenate(p_parts, axis=1))
        psums.append(_split(psum, 2))
    staged_w = attend_pre([(g, 1, s_w, mask_w, g.vwt[:, pl.ds(off_w, rows_w)]) for g, s_w in zip(grp, s_ws)])
    for g, p_all in zip(grp, p_cmp):
        g.ocmp[...] = _dot(g.vct[0], p_all)
    ovt = ovt_ref[...]
    imps = [_dot(ovt, hi) + _dot(ovt, lo) for hi, lo in psums]
    attend_post(staged_w)

    n_sel = seq // SEL_BLOCK
    jrow = lax.broadcasted_iota(jnp.int32, (n_sel, QB), 0)
    tcol = lax.broadcasted_iota(jnp.int32, (n_sel, QB), 1)
    qblk = lax.shift_right_arithmetic(i * QB + tcol, sel_shift)
    causal_b = jrow <= qblk
    forced = causal_b & ((jrow == 0) | (jrow >= qblk - 1))
    for g, imp in zip(grp, imps):
        score = jnp.where(forced, jnp.inf, jnp.where(causal_b, imp[:n_sel, :], -jnp.inf))
        rank = jnp.zeros((n_sel, QB), jnp.int32)
        for jj in range(n_sel):
            sc = score[jj:jj + 1, :]
            beats = (sc > score) | ((sc == score) & (jrow > jj))
            rank = rank + beats.astype(jnp.int32)
        g.sel[...] = jnp.where((rank < SEL_TOPK) & causal_b, 1.0, 0.0)

    def scores(g, k_ref, off, rows, bias):
        return _dot(k_ref[pl.ds(off, rows), :], g.q_all) * scale + bias

    def sel_mask(g, c, nblk):
        rix = lax.broadcasted_iota(jnp.int32, (nblk * QB, QB), 0)
        per_key_block = QB // SEL_BLOCK
        flags = [g.sel[pl.ds(per_key_block * c + j, 1), :] for j in range(per_key_block * nblk)]
        m = flags[-1]
        for j in reversed(range(len(flags) - 1)):
            m = jnp.where(rix < (j + 1) * SEL_BLOCK, flags[j], m)
        return m > 0.5

    lower = row <= col

    n_far = jnp.maximum(i - 1, 0)
    wide = 4

    def far_tile(c, nblk):
        off = pl.multiple_of(c * QB, QB)
        attend([(g, 0, scores(g, g.ksn, off, nblk * QB, g.bfar[0]), sel_mask(g, c, nblk),
                 g.vst[:, pl.ds(off, nblk * QB)]) for g in grp])

    def far_body(p, carry):
        far_tile(p * wide, wide)
        return carry
    lax.fori_loop(0, n_far // wide, far_body, 0)
    rem = n_far % wide

    @pl.when(rem >= 2)
    def _():
        far_tile(n_far - rem, 2)

    @pl.when(rem % 2 == 1)
    def _():
        far_tile(n_far - 1, 1)

    @pl.when(i >= 1)
    def _():
        off = pl.multiple_of((i - 1) * QB, QB)
        after_first = lax.broadcasted_iota(jnp.int32, (2 * QB, QB), 0) - QB
        tcol = lax.broadcasted_iota(jnp.int32, (2 * QB, QB), 1)
        causal = after_first <= tcol
        jobs = []
        for g in grp:
            bias = jnp.concatenate([g.boff[0], g.bdiag[0]], axis=0)
            jobs.append((g, 0, scores(g, g.ksn, off, 2 * QB, bias), None, g.vst[:, pl.ds(off, 2 * QB)]))
            jobs.append((g, 1, scores(g, g.kwn, off, 2 * QB, bias), causal, g.vwt[:, pl.ds(off, 2 * QB)]))
        jobs = [(g, br, s, (sel_mask(g, i - 1, 2) & causal) if br == 0 else m, vt) for g, br, s, m, vt in jobs]
        attend(jobs)

    @pl.when(i == 0)
    def _():
        jobs = []
        for g in grp:
            jobs.append((g, 0, scores(g, g.ksn, 0, QB, g.bdiag[0]), sel_mask(g, 0, 1) & lower, g.vst[:, 0:QB]))
            jobs.append((g, 1, scores(g, g.kwn, 0, QB, g.bdiag[0]), lower, g.vwt[:, 0:QB]))
        attend(jobs)

    for g in grp:
        gate_t = jax.nn.sigmoid(g.gate[...].T[:4 * R, :])
        for r, hs in enumerate(heads):
            g_cmp = gate_t[3 * r:3 * r + 1, :]
            g_sel = gate_t[3 * r + 1:3 * r + 2, :] * (1.0 / jnp.maximum(g.l[r], TINY))
            g_win = gate_t[3 * r + 2:3 * r + 3, :] * (1.0 / jnp.maximum(g.l[R + r], TINY))
            o_t = g_cmp * g.ocmp[:, hs] + g_sel * g.acc[0, :, hs] + g_win * g.acc[1, :, hs]
            g.o[:, hs] = o_t.T.astype(g.o.dtype)


def nsa_attention(pa, pvt, kcmp, vcmp_t, pg, tables, q_gain, k_gain, bsz, seq):
    nqb = seq // Q_BLOCK
    G, R = NSA_KV_HEADS, NSA_GROUP
    W = R * Q_BLOCK
    bdiag, boff, bfar, bcmp, overlap_t = tables
    n_cmp_rows = seq // CMP_STRIDE

    kv_w = G * HEAD_DIM

    def k_spec(seg):
        return pl.BlockSpec((seq, kv_w), lambda b, i, s=_A_BLK[seg] // G: (b, s))

    def vt_spec(seg):
        return pl.BlockSpec((kv_w, seq), lambda b, i, s=_T_BLK[seg] // G: (s, b))

    def whole(x):
        return pl.BlockSpec(x.shape, lambda b, i: (0,) * x.ndim)

    return pl.pallas_call(
        functools.partial(_nsa_kernel, seq=seq),
        grid=(bsz, nqb),
        in_specs=[pl.BlockSpec((NSA_Q, Q_BLOCK), lambda b, i, s=_T_BLK[_Q_N] * LANE // NSA_Q: (s, b * nqb + i)),
                  pl.BlockSpec((1, n_cmp_rows, kv_w), lambda b, i: (b, 0, 0)),
                  pl.BlockSpec((1, kv_w, n_cmp_rows), lambda b, i: (b, 0, 0)),
                  k_spec(_KS), vt_spec(_VS), k_spec(_KW), vt_spec(_VW),
                  pl.BlockSpec((Q_BLOCK, G * LANE), lambda b, i, s=B_COLS // (G * LANE): (b * nqb + i, s)),
                  whole(bdiag), whole(boff), whole(bfar), whole(bcmp),
                  pl.BlockSpec((HEAD_DIM, Q_BLOCK), lambda b, i: (0, 0)),
                  pl.BlockSpec((3, HEAD_DIM), lambda b, i: (0, 0)),
                  pl.BlockSpec((LANE, LANE), lambda b, i: (0, 0))],
        out_specs=pl.BlockSpec((Q_BLOCK, NSA_Q), lambda b, i: (b * nqb + i, 0)),
        out_shape=jax.ShapeDtypeStruct((bsz * seq, NSA_Q), _MXU),
        scratch_shapes=[pltpu.VMEM((seq, kv_w), _MXU),
                        pltpu.VMEM((seq, kv_w), _MXU),
                        pltpu.VMEM((G, 2 * R, 1, Q_BLOCK), _F32),
                        pltpu.VMEM((G, 2 * R, 1, Q_BLOCK), _F32),
                        pltpu.VMEM((G, 2, HEAD_DIM, W), _F32),
                        pltpu.VMEM((G, HEAD_DIM, W), _F32),
                        pltpu.VMEM((G, seq // SEL_BLOCK, Q_BLOCK), _F32)],
        compiler_params=_params("parallel", "arbitrary"),
        name="nsa_attention",
    )(pvt, kcmp, vcmp_t, pa, pvt, pa, pvt, pg, bdiag, boff, bfar, bcmp,
      jnp.broadcast_to(q_gain.reshape(HEAD_DIM, 1), (HEAD_DIM, Q_BLOCK)), k_gain, overlap_t)


def _t5_bucket(dist):
    n = jnp.maximum(dist, 0)
    exact = REL_BUCKETS // 2
    big = exact + (jnp.log(jnp.maximum(n, 1).astype(jnp.float32) / exact)
                   / math.log(REL_MAX_DIST / exact) * (REL_BUCKETS - exact)).astype(jnp.int32)
    return jnp.where(n < exact, n, jnp.minimum(big, REL_BUCKETS - 1))


def nsa_tables(rel_bias, seq):
    G, R = NSA_KV_HEADS, NSA_GROUP
    nqb = seq // Q_BLOCK
    tab_h = rel_bias.T.astype(_F32)
    s = np.arange(Q_BLOCK)[:, None]
    t = np.arange(Q_BLOCK)[None, :]

    def group_tiles(x, lead):
        nl = len(lead)
        rows = x.shape[-2]
        x = x.reshape((G, R) + lead + (rows, Q_BLOCK))
        perm = (0,) + tuple(range(2, 2 + nl)) + (2 + nl, 1, 3 + nl)
        return x.transpose(perm).reshape((G,) + lead + (rows, R * Q_BLOCK))

    bdiag = group_tiles(jnp.take(tab_h, _t5_bucket(jnp.asarray(t - s)), axis=1), ())
    boff = group_tiles(jnp.take(tab_h, _t5_bucket(jnp.asarray(Q_BLOCK + t - s)), axis=1), ())
    bfar = group_tiles(jnp.take(tab_h, _t5_bucket(jnp.full((1, Q_BLOCK), 2 * Q_BLOCK)), axis=1), ())
    lead = (nqb - 1) * Q_BLOCK // CMP_STRIDE
    rows_c = -(-(seq // CMP_STRIDE + lead) // 8) * 8
    m = np.arange(rows_c)[:, None]
    dist_c = t - (CMP_STRIDE * (m - lead) + CMP_BLOCK - 1)
    bcmp = group_tiles(jnp.take(tab_h, _t5_bucket(jnp.asarray(dist_c)), axis=1), ())
    n_cmp = (seq - CMP_BLOCK) // CMP_STRIDE + 1
    n_sel = seq // SEL_BLOCK
    c_start = np.arange(LANE) * CMP_STRIDE
    s_start = np.arange(LANE) * SEL_BLOCK
    overlap = ((c_start[:, None] < s_start[None, :] + SEL_BLOCK)
               & (c_start[:, None] + CMP_BLOCK > s_start[None, :])
               & (np.arange(LANE)[:, None] < n_cmp) & (np.arange(LANE)[None, :] < n_sel))
    return bdiag, boff, bfar, bcmp, jnp.asarray(overlap.T.astype(np.float32)).astype(_MXU)


def _sb_kernel(q_ref, k_ref, vt_ref, o_ref, acc_ref, rest_ref, *, heads):
    QB = Q_BLOCK
    i = pl.program_id(2)
    scale = HEAD_DIM ** -0.5
    row = lax.broadcasted_iota(jnp.int32, (QB, QB), 0)
    col = lax.broadcasted_iota(jnp.int32, (QB, QB), 1)
    strict = row < col
    wide = 4

    pad = 8
    r_l = lax.broadcasted_iota(jnp.int32, (QB + pad, 2 * QB), 0)
    c_l = lax.broadcasted_iota(jnp.int32, (QB + pad, 2 * QB), 1)
    later = jnp.where((jnp.where(c_l >= QB, c_l - QB, c_l) > r_l) | (r_l >= QB), 1.0, 0.0).astype(_MXU)

    def block(c, first, nblk=1):
        rows = nblk * QB
        off = pl.multiple_of(c * QB, QB)
        hss = [slice(h * HEAD_DIM, (h + 1) * HEAD_DIM) for h in range(heads)]
        zs = [_dot(k_ref[pl.ds(off, rows), hs], q_ref[hs, :]) * scale for hs in hss]
        log_bs, pieces = [], []
        for z in zs:
            log_b, _ = _log_sigmoid_pair(z)
            log_1mb = log_b - z
            if first:
                log_1mb = jnp.where(strict, log_1mb, 0.0)
            log_bs.append(log_b)
            pieces.append([jnp.concatenate(_split(log_1mb[j * QB:(j + 1) * QB, :], 2), axis=0) for j in range(nblk)])
        sums = [[_dot(later, p) for p in ps] for ps in pieces]
        betweens, totals = [], []
        for per_block in sums:
            after = None
            parts = []
            for s in reversed(per_block):
                parts.append(s[:QB, :] if after is None else s[:QB, :] + after)
                after = s[QB:QB + 1, :] if after is None else after + s[QB:QB + 1, :]
            betweens.append(parts[0] if nblk == 1 else jnp.concatenate(parts[::-1], axis=0))
            totals.append(after)
        probs = []
        for h in range(heads):
            if first:
                probs.append(jnp.where(strict, jnp.exp(log_bs[h] + betweens[h]), 0.0).astype(_MXU))
                rest_ref[h] = totals[h]
            else:
                rest = rest_ref[h]
                probs.append(jnp.exp(log_bs[h] + betweens[h] + rest).astype(_MXU))
                rest_ref[h] = rest + totals[h]
        upds = [_dot(vt_ref[hs, pl.ds(off, rows)], probs[h]) for h, hs in enumerate(hss)]
        for h in range(heads):
            acc_ref[h] = upds[h] if first else acc_ref[h] + upds[h]

    block(i, True)

    def body(n, carry):
        block(i - wide * (n + 1), False, wide)
        return carry
    lax.fori_loop(0, i // wide, body, 0)
    rem = i % wide

    @pl.when(rem >= 2)
    def _():
        block(rem - 2, False, 2)

    @pl.when(rem % 2 == 1)
    def _():
        block(0, False)
    for h in range(heads):
        o_ref[:, h * HEAD_DIM:(h + 1) * HEAD_DIM] = acc_ref[h].T.astype(o_ref.dtype)


def stick_breaking(pa, pvt, bsz, seq, heads=SB_HEADS):
    nqb = seq // Q_BLOCK
    w = heads * HEAD_DIM
    qb, kb, vb = _T_BLK[_Q_S] * LANE // w, _A_BLK[_K_S] * LANE // w, _T_BLK[_V_S] * LANE // w
    return pl.pallas_call(
        functools.partial(_sb_kernel, heads=heads),
        grid=(bsz, SB_HEADS // heads, nqb),
        in_specs=[pl.BlockSpec((w, Q_BLOCK), lambda b, h, i: (qb + h, b * nqb + i)),
                  pl.BlockSpec((seq, w), lambda b, h, i: (b, kb + h)),
                  pl.BlockSpec((w, seq), lambda b, h, i: (vb + h, b))],
        out_specs=pl.BlockSpec((Q_BLOCK, w), lambda b, h, i: (b * nqb + i, h)),
        out_shape=jax.ShapeDtypeStruct((bsz * seq, SB_W), _MXU),
        scratch_shapes=[pltpu.VMEM((heads, HEAD_DIM, Q_BLOCK), _F32),
                        pltpu.VMEM((heads, 1, Q_BLOCK), _F32)],
        compiler_params=_params("parallel", "parallel", "arbitrary"),
        name="stick_breaking",
    )(pvt, pa, pvt)


def _hgrn_kernel(q_ref, f_ref, i_ref, g_ref, llb_ref, l1lb_ref, omlb_ref, ng_ref, o_ref, state_ref, *, seq, heads):
    C, SUB = HG_CHUNK, HG_SUB
    nsub = C // SUB
    hss = [slice(h * HG_DK, (h + 1) * HG_DK) for h in range(heads)]
    row = lax.broadcasted_iota(jnp.int32, (C, C), 0)
    col = lax.broadcasted_iota(jnp.int32, (C, C), 1)
    upto = jnp.where(col <= row, 1.0, 0.0).astype(_MXU)
    sub_t = lax.broadcasted_iota(jnp.int32, (SUB, 1), 0)
    sub_c = lax.broadcasted_iota(jnp.int32, (SUB, C), 1)
    state_ref[...] = jnp.zeros(state_ref.shape, _F32)

    def chunk(n, carry):
        off = pl.multiple_of(n * C, C)
        fps = [f_ref[pl.ds(off, C), hs] for hs in hss]
        pieces = []
        for fp, hs in zip(fps, hss):
            log_sig, _ = _log_sigmoid_pair(fp)
            y = l1lb_ref[:, hs] + log_sig
            a = llb_ref[:, hs]
            log_f = jnp.maximum(a, y) + jnp.log1p(jnp.exp(-jnp.abs(a - y)))
            pieces.append(_split(log_f, 3))
        bs = [_dot(upto, p[0]) + _dot(upto, p[1]) + _dot(upto, p[2]) for p in pieces]
        qs = [q_ref[pl.ds(off, C), hs].astype(_F32) for hs in hss]
        ks = [omlb_ref[:, hs] * jax.nn.sigmoid(-fp) for fp, hs in zip(fps, hss)]
        vs = [i_ref[pl.ds(off, C), hs].astype(_F32) for hs in hss]

        outs = [_dot_nt((q * jnp.exp(b)).astype(_MXU), state_ref[h].astype(_MXU))
                for h, (q, b) in enumerate(zip(qs, bs))]

        lhs, rhs = [], []
        for q, k, b in zip(qs, ks, bs):
            for s_i in range(1, nsub):
                lo = s_i * SUB
                b_ref = b[lo - 1:lo, :]
                lhs.append((q[lo:lo + SUB, :] * jnp.exp(b[lo:lo + SUB, :] - b_ref)).astype(_MXU))
                rhs.append((k * jnp.exp(jnp.minimum(b_ref - b, 0.0))).astype(_MXU))
        a_blocks = [_dot_nt(x, y) for x, y in zip(lhs, rhs)]
        intra = []
        for h in range(heads):
            rows = [jnp.zeros((SUB, C), _F32)]
            for s_i in range(1, nsub):
                rows.append(jnp.where(sub_c < s_i * SUB, a_blocks[h * (nsub - 1) + s_i - 1], 0.0))
            intra.append(jnp.concatenate(rows, axis=0).astype(_MXU))
        outs = [o + _dot(a, v.astype(_MXU)) for o, a, v in zip(outs, intra, vs)]

        for h, (q, k, v, b) in enumerate(zip(qs, ks, vs, bs)):
            diag = []
            for s_i in range(nsub):
                lo = s_i * SUB
                b_i = b[lo:lo + SUB, :]
                q_i = q[lo:lo + SUB, :]
                o_d = jnp.zeros((SUB, HG_DV), _F32)
                for s in range(SUB):
                    w = jnp.exp(b_i - b[lo + s:lo + s + 1, :])
                    a_col = jnp.sum(q_i * k[lo + s:lo + s + 1, :] * w, axis=-1, keepdims=True)
                    a_col = jnp.where(sub_t >= s, a_col, 0.0)
                    o_d = o_d + a_col * v[lo + s:lo + s + 1, :]
                diag.append(o_d)
            outs[h] = outs[h] + jnp.concatenate(diag, axis=0)

        k_decs = [(k * jnp.exp(b[C - 1:C, :] - b)).astype(_MXU) for k, b in zip(ks, bs)]
        grown = [_dot(v.T.astype(_MXU), kd) for v, kd in zip(vs, k_decs)]
        for h, (b, hs) in enumerate(zip(bs, hss)):
            state_ref[h] = state_ref[h] * jnp.exp(b[C - 1:C, :]) + grown[h]
            gate = jax.nn.sigmoid(g_ref[pl.ds(off, C), hs])
            o_ref[pl.ds(off, C), hs] = (_rms(outs[h], ng_ref[...]) * gate).astype(o_ref.dtype)
        return carry

    lax.fori_loop(0, seq // C, chunk, 0)


def hgrn2(pa, pb, lb, norm_gain, bsz, seq, heads=4):
    lb = lb.reshape(1, HG_KW).astype(_F32)
    log_lb = jnp.log(lb)
    log_1mlb = jnp.log1p(-lb)
    one_mlb = 1.0 - lb
    w = heads * HG_DK

    def seq_spec(blk):
        return pl.BlockSpec((seq, w), lambda b, h, s=blk * LANE // w: (b, s + h))

    head_vec = pl.BlockSpec((1, w), lambda b, h: (0, h))
    return pl.pallas_call(
        functools.partial(_hgrn_kernel, seq=seq, heads=heads),
        grid=(bsz, HG_HEADS // heads),
        in_specs=[seq_spec(_A_BLK[_Q_H]), seq_spec(_B_BLK[_F_H]), seq_spec(_A_BLK[_I_H]), seq_spec(_B_BLK[_G_H]),
                  head_vec, head_vec, head_vec,
                  pl.BlockSpec((1, HG_DV), lambda b, h: (0, 0))],
        out_specs=pl.BlockSpec((seq, w), lambda b, h: (b, h)),
        out_shape=jax.ShapeDtypeStruct((bsz * seq, HG_VW), _MXU),
        scratch_shapes=[pltpu.VMEM((heads, HG_DV, HG_DK), _F32)],
        compiler_params=_params("parallel", "parallel"),
        name="hgrn2",
    )(pa, pb, pa, pb, log_lb, log_1mlb, one_mlb, norm_gain.reshape(1, HG_DV))


def _merge_kernel(o0_ref, o1_ref, o2_ref, w0_ref, w1_ref, w2_ref, g0_ref, g1_ref, g2_ref, out_ref):
    ys = [_dot(o_ref[...], w_ref[0]) for o_ref, w_ref in ((o0_ref, w0_ref), (o1_ref, w1_ref), (o2_ref, w2_ref))]
    acc = jax.nn.sigmoid(g0_ref[...].astype(_F32)) * ys[0]
    acc = acc + jax.nn.sigmoid(g1_ref[...].astype(_F32)) * ys[1]
    acc = acc + jax.nn.sigmoid(g2_ref[...].astype(_F32)) * ys[2]
    out_ref[...] = acc.astype(out_ref.dtype)


def merge_branches(o_nsa, o_sb, o_hg, w_branch, gates, tm=1024, tn=512):
    m = o_nsa.shape[0]
    nj = D_MODEL // tn
    gblk = 0

    def o_spec():
        return pl.BlockSpec((tm, BRANCH_W), lambda i, j: (i, 0))

    def w_spec(n):
        return pl.BlockSpec((1, BRANCH_W, tn), lambda i, j, n=n: (n, 0, j))

    def g_spec(n):
        return pl.BlockSpec((tm, tn), lambda i, j, n=n: (i, gblk + n * nj + j))

    return pl.pallas_call(
        _merge_kernel,
        grid=(m // tm, nj),
        in_specs=[o_spec(), o_spec(), o_spec(), w_spec(0), w_spec(1), w_spec(2), g_spec(0), g_spec(1), g_spec(2)],
        out_specs=pl.BlockSpec((tm, tn), lambda i, j: (i, j)),
        out_shape=jax.ShapeDtypeStruct((m, D_MODEL), _MXU),
        compiler_params=_params("parallel", "arbitrary"),
        name="merge_branches",
    )(o_nsa, o_sb, o_hg, w_branch, w_branch, w_branch, gates, gates, gates)


def _ffn_up_kernel(x_ref, halo_ref, gain_ref, wg_ref, wv_ref, cwg_ref, cwv_ref, cbg_ref, cbv_ref, o_ref, h_ref,
                   *, tiles_per_seq, sub):
    i = pl.program_id(0)

    @pl.when(pl.program_id(1) == 0)
    def _():
        h_ref[...] = _rms(x_ref[...], gain_ref[...]).astype(_MXU)

    first = (i % tiles_per_seq) == 0
    h = h_ref[...]
    halo = _rms(halo_ref[...], gain_ref[...]).astype(_MXU)
    tm = h.shape[0]
    hrows = halo.shape[0]
    row = lax.broadcasted_iota(jnp.int32, (tm, 1), 0)
    cols = [slice(c * sub, (c + 1) * sub) for c in range(wg_ref.shape[1] // sub)]

    prods = [[(_dot(h, w_ref[:, cs]), jnp.where(first, 0.0, _dot(halo, w_ref[:, cs]))) for w_ref in (wg_ref, wv_ref)]
             for cs in cols]

    def conv(u, uh, cw, cb):
        u1 = jnp.where(row == 0, uh[hrows - 1:hrows, :], pltpu.roll(u, 1, axis=0))
        u2 = jnp.where(row == 0, uh[hrows - 2:hrows - 1, :],
                       jnp.where(row == 1, uh[hrows - 1:hrows, :], pltpu.roll(u, 2, axis=0)))
        return cb + u2 * cw[0:1, :] + u1 * cw[1:2, :] + u * cw[2:3, :]

    for cs, ((ug, uhg), (uv, uhv)) in zip(cols, prods):
        gate = conv(ug, uhg, cwg_ref[:, cs], cbg_ref[:, cs])
        val = conv(uv, uhv, cwv_ref[:, cs], cbv_ref[:, cs])
        o_ref[:, cs] = (jax.nn.silu(gate) * val).astype(o_ref.dtype)


def ffn_up(x, gain, w_up, conv_w, conv_b, seq, tm=1024, tn=512, sub=256):
    m, k = x.shape
    nj = D_FF // tn
    hrows = 8
    conv_b = conv_b.reshape(1, 2 * D_FF)
    return pl.pallas_call(
        functools.partial(_ffn_up_kernel, tiles_per_seq=seq // tm, sub=sub),
        grid=(m // tm, nj),
        in_specs=[pl.BlockSpec((tm, k), lambda i, j: (i, 0)),
                  pl.BlockSpec((hrows, k), lambda i, j: (jnp.maximum(i * (tm // hrows) - 1, 0), 0)),
                  pl.BlockSpec((1, k), lambda i, j: (0, 0)),
                  pl.BlockSpec((k, tn), lambda i, j: (0, j)),
                  pl.BlockSpec((k, tn), lambda i, j: (0, nj + j)),
                  pl.BlockSpec((CONV_W, tn), lambda i, j: (0, j)),
                  pl.BlockSpec((CONV_W, tn), lambda i, j: (0, nj + j)),
                  pl.BlockSpec((1, tn), lambda i, j: (0, j)),
                  pl.BlockSpec((1, tn), lambda i, j: (0, nj + j))],
        out_specs=pl.BlockSpec((tm, tn), lambda i, j: (i, j)),
        out_shape=jax.ShapeDtypeStruct((m, D_FF), _MXU),
        scratch_shapes=[pltpu.VMEM((tm, k), _MXU)],
        compiler_params=_params("parallel", "arbitrary"),
        name="ffn_up_conv",
    )(x, x, gain.reshape(1, k), w_up, w_up, conv_w, conv_w, conv_b, conv_b)


def _gather_cols(w, segs):
    return jnp.concatenate([w[:, _SPLIT_OFF[s]:_SPLIT_OFF[s + 1]] for s in segs], axis=1)


def _nsa_gate_cols(w):
    g = w[:, _SPLIT_OFF[_G_N]:_SPLIT_OFF[_G_N + 1]].reshape(-1, NSA_KV_HEADS, NSA_GROUP * 3)
    g = jnp.pad(g, ((0, 0), (0, 0), (0, LANE - NSA_GROUP * 3)))
    return g.reshape(-1, NSA_KV_HEADS * LANE)


def kernel(x, norm_attn, w_in, nsa_q_gain, nsa_k_gain, cmp_pos, cmp_w1, cmp_w2, rel_bias, hg_lower_bound,
           hg_norm_gain, w_branch, w_out, norm_ffn, w_up, conv_w, conv_b, w_down):
    bsz, seq, d = x.shape
    m = bsz * seq
    lb_all = jnp.cumsum(jax.nn.softmax(hg_lower_bound.astype(_F32), axis=0), axis=0)
    lb_all = jnp.maximum(lb_all - lb_all[0:1], 0.0)
    tables = nsa_tables(rel_bias, seq)
    xf = x.reshape(m, d)
    for l in range(w_in.shape[0]):
        wl = w_in[l]
        h = rmsnorm_cast(xf, norm_attn[l])
        pa = matmul(h, _gather_cols(wl, _A_ORDER).astype(_MXU), _MXU, tm=512, tn=A_COLS // 2, name="proj_a")
        w_b = jnp.concatenate([_gather_cols(wl, _B_ORDER), _nsa_gate_cols(wl)], axis=1).astype(_MXU)
        pb = matmul(h, w_b, _F32, tn=w_b.shape[1] // 3, name="proj_b")
        gates = matmul(h, _gather_cols(wl, (_GATES,)).astype(_MXU), _MXU, tn=1024, name="proj_gates")
        pkc = matmul(h, _gather_cols(wl, (_KC,)).astype(_MXU), _MXU, name="proj_kc")
        pvc = matmul(h, _gather_cols(wl, (_VC,)).astype(_MXU), _MXU, name="proj_vc")
        blk_rows = seq // CMP_STRIDE
        kcmp = compress(pkc.reshape(bsz, blk_rows, CMP_STRIDE * NSA_KV), cmp_w1[l, 0], cmp_w2[l, 0], cmp_pos[l, 0],
                        nsa_k_gain[l, 0], True)
        vcmp_t = compress(pvc.reshape(bsz, blk_rows, CMP_STRIDE * NSA_KV), cmp_w1[l, 1], cmp_w2[l, 1], cmp_pos[l, 1],
                          nsa_k_gain[l, 0], False)
        pvt = matmul_nt(_gather_cols(wl, _T_ORDER).T.astype(_MXU), h, _MXU, name="proj_vt")
        o_nsa = nsa_attention(pa, pvt, kcmp, vcmp_t, pb, tables, nsa_q_gain[l], nsa_k_gain[l], bsz, seq)
        o_sb = stick_breaking(pa, pvt, bsz, seq)
        o_hg = hgrn2(pa, pb, lb_all[l], hg_norm_gain[l], bsz, seq)
        merged = merge_branches(o_nsa, o_sb, o_hg, w_branch[l].astype(_MXU), gates)
        xf = matmul(merged, w_out[l].astype(_MXU), _F32, residual=xf, tn=1024, name="out_proj")
        act = ffn_up(xf, norm_ffn[l], w_up[l].astype(_MXU), conv_w[l], conv_b[l], seq)
        xf = matmul(act, w_down[l].astype(_MXU), _F32, residual=xf, tm=512, tn=1024, name="ffn_down")
    return xf.reshape(bsz, seq, d)
```

```python
import functools
import math
import types

import numpy as np
import jax
import jax.numpy as jnp
from jax import lax
from jax.experimental import pallas as pl
from jax.experimental.pallas import tpu as pltpu

D_MODEL = 2048
DEPTH = 4
HEAD_DIM = 128
BRANCH_W = D_MODEL // 2
N_BRANCH = 3
NSA_HEADS = BRANCH_W // HEAD_DIM
NSA_KV_HEADS = NSA_HEADS // 4
NSA_GROUP = NSA_HEADS // NSA_KV_HEADS
CMP_BLOCK = 32
CMP_STRIDE = 16
SEL_BLOCK = 64
SEL_TOPK = 16
WINDOW = 512
Q_BLOCK = 128
SB_HEADS = BRANCH_W // HEAD_DIM
HG_DK = 128
HG_DV = 128
HG_HEADS = BRANCH_W // HG_DV
HG_CHUNK = 64
HG_SUB = 8
D_FF = 256 * ((8 * D_MODEL // 3 + 255) // 256)
CONV_W = 3
REL_BUCKETS = 32
REL_MAX_DIST = 128
EPS = 1e-6
NEG = -1e30
TINY = 1e-30

NSA_Q = NSA_HEADS * HEAD_DIM
NSA_KV = NSA_KV_HEADS * HEAD_DIM
NSA_GATE = 3 * NSA_HEADS
SB_W = SB_HEADS * HEAD_DIM
HG_KW = HG_HEADS * HG_DK
HG_VW = HG_HEADS * HG_DV
SPLIT_SIZES = (NSA_Q,) + (NSA_KV,) * 6 + (NSA_GATE,) + (SB_W,) * 3 + (HG_KW, HG_KW, HG_VW, HG_VW, N_BRANCH * D_MODEL)
(_Q_N, _KC, _VC, _KS, _VS, _KW, _VW, _G_N, _Q_S, _K_S, _V_S, _Q_H, _F_H, _I_H, _G_H, _GATES) = range(16)
_SPLIT_OFF = np.concatenate([[0], np.cumsum(SPLIT_SIZES)]).tolist()

LANE = 128
_A_ORDER = (_K_S, _Q_H, _I_H, _KS, _KW)
_A_BLK = {}
_off = 0
for _s in _A_ORDER:
    _A_BLK[_s] = _off // LANE
    _off += SPLIT_SIZES[_s]
A_COLS = _off
_T_ORDER = (_V_S, _Q_S, _Q_N, _VS, _VW)
_T_BLK = {}
_off = 0
for _s in _T_ORDER:
    _T_BLK[_s] = _off // LANE
    _off += SPLIT_SIZES[_s]
T_ROWS = _off
_B_ORDER = (_F_H, _G_H)
_B_BLK = {}
_off = 0
for _s in _B_ORDER:
    _B_BLK[_s] = _off // LANE
    _off += SPLIT_SIZES[_s]
B_COLS = _off

VMEM_LIMIT = 48 * 1024 * 1024

_MXU = jnp.bfloat16
_F32 = jnp.float32


def _dot(a, b):
    return jnp.dot(a, b, preferred_element_type=_F32)


def _dot_nt(a, b):
    return lax.dot_general(a, b, (((1,), (1,)), ((), ())), preferred_element_type=_F32)


def _split(x, parts):
    out = []
    r = x
    for _ in range(parts):
        h = r.astype(_MXU)
        out.append(h)
        r = r - h.astype(_F32)
    return out


def _params(*sem):
    return pltpu.CompilerParams(dimension_semantics=sem, vmem_limit_bytes=VMEM_LIMIT)


def _rms(x, gain):
    return x * lax.rsqrt(jnp.mean(x * x, axis=-1, keepdims=True) + EPS) * gain


def _log_sigmoid_pair(z):
    lp = jnp.log(1.0 + jnp.exp(-jnp.abs(z)))
    return jnp.minimum(z, 0.0) - lp, jnp.minimum(-z, 0.0) - lp


def _rmsnorm_kernel(x_ref, g_ref, o_ref):
    o_ref[...] = _rms(x_ref[...], g_ref[...]).astype(o_ref.dtype)


def rmsnorm_cast(x2d, gain):
    m, d = x2d.shape
    tm = 512
    return pl.pallas_call(
        _rmsnorm_kernel,
        grid=(m // tm,),
        in_specs=[pl.BlockSpec((tm, d), lambda i: (i, 0)),
                  pl.BlockSpec((1, d), lambda i: (0, 0))],
        out_specs=pl.BlockSpec((tm, d), lambda i: (i, 0)),
        out_shape=jax.ShapeDtypeStruct((m, d), _MXU),
        compiler_params=_params("parallel"),
        name="rmsnorm_cast",
    )(x2d, gain.reshape(1, d))


def _mm_kernel(a_ref, w_ref, o_ref):
    o_ref[...] = _dot(a_ref[...], w_ref[...]).astype(o_ref.dtype)


def _mm_res_kernel(a_ref, w_ref, r_ref, o_ref):
    o_ref[...] = (r_ref[...] + _dot(a_ref[...], w_ref[...])).astype(o_ref.dtype)


def matmul(a, w, out_dtype, residual=None, tm=1024, tn=512, name="matmul"):
    m, k = a.shape
    n = w.shape[1]
    tm = min(tm, m)
    tn = min(tn, n)
    in_specs = [pl.BlockSpec((tm, k), lambda i, j: (i, 0)),
                pl.BlockSpec((k, tn), lambda i, j: (0, j))]
    args = [a, w]
    kern = _mm_kernel
    if residual is not None:
        in_specs.append(pl.BlockSpec((tm, tn), lambda i, j: (i, j)))
        args.append(residual)
        kern = _mm_res_kernel
    return pl.pallas_call(
        kern,
        grid=(m // tm, n // tn),
        in_specs=in_specs,
        out_specs=pl.BlockSpec((tm, tn), lambda i, j: (i, j)),
        out_shape=jax.ShapeDtypeStruct((m, n), out_dtype),
        compiler_params=_params("parallel", "arbitrary"),
        name=name,
    )(*args)


def _mm_nt_kernel(wt_ref, a_ref, o_ref):
    o_ref[...] = _dot_nt(wt_ref[...], a_ref[...]).astype(o_ref.dtype)


def matmul_nt(wt, a, out_dtype, tm=1024, tn=512, name="matmul_nt"):
    n, k = wt.shape
    m = a.shape[0]
    tm = min(tm, m)
    tn = min(tn, n)
    return pl.pallas_call(
        _mm_nt_kernel,
        grid=(m // tm, n // tn),
        in_specs=[pl.BlockSpec((tn, k), lambda i, j: (j, 0)),
                  pl.BlockSpec((tm, k), lambda i, j: (i, 0))],
        out_specs=pl.BlockSpec((tn, tm), lambda i, j: (j, i)),
        out_shape=jax.ShapeDtypeStruct((n, m), out_dtype),
        compiler_params=_params("parallel", "arbitrary"),
        name=name,
    )(wt, a)


def _compress_kernel(x_ref, wa_ref, wb_ref, pa_ref, pb_ref, w2_ref, gain_ref, o_ref, *, normalize):
    x = x_ref[0].astype(_F32)
    u = _dot((x + pa_ref[...]).astype(_MXU), wa_ref[...])
    v = _dot((x + pb_ref[...]).astype(_MXU), wb_ref[...])
    hid = jax.nn.gelu(u + pltpu.roll(v, v.shape[0] - 1, axis=0)).astype(_MXU)
    if normalize:
        out = _dot(hid, w2_ref[...])
        out = jnp.concatenate(
            [_rms(out[:, g * HEAD_DIM:(g + 1) * HEAD_DIM], gain_ref[...]) for g in range(NSA_KV_HEADS)], axis=1)
    else:
        out = _dot_nt(w2_ref[...], hid)
    o_ref[0] = out.astype(o_ref.dtype)


def compress(x, w1, w2, pos, gain, normalize):
    bsz = x.shape[0]
    nrow = x.shape[1]
    kdim = x.shape[2]
    half = CMP_BLOCK // 2
    eye = jnp.eye(NSA_KV_HEADS, dtype=w1.dtype)

    def embed(w):
        return jnp.einsum('lde,gh->lgdhe', w, eye).reshape(kdim, NSA_KV).astype(_MXU)

    def tile_pos(p):
        return jnp.broadcast_to(p[:, None, :], (half, NSA_KV_HEADS, HEAD_DIM)).reshape(1, kdim)

    w2b = jnp.einsum('de,gh->gdhe', w2, eye).reshape(NSA_KV, NSA_KV).astype(_MXU)
    out_block = (1, nrow, NSA_KV)
    if not normalize:
        w2b = w2b.T
        out_block = (1, NSA_KV, nrow)
    return pl.pallas_call(
        functools.partial(_compress_kernel, normalize=normalize),
        grid=(bsz,),
        in_specs=[pl.BlockSpec((1, nrow, kdim), lambda b: (b, 0, 0)),
                  pl.BlockSpec((kdim, NSA_KV), lambda b: (0, 0)),
                  pl.BlockSpec((kdim, NSA_KV), lambda b: (0, 0)),
                  pl.BlockSpec((1, kdim), lambda b: (0, 0)),
                  pl.BlockSpec((1, kdim), lambda b: (0, 0)),
                  pl.BlockSpec((NSA_KV, NSA_KV), lambda b: (0, 0)),
                  pl.BlockSpec((1, HEAD_DIM), lambda b: (0, 0))],
        out_specs=pl.BlockSpec(out_block, lambda b: (b, 0, 0)),
        out_shape=jax.ShapeDtypeStruct((bsz,) + out_block[1:], _MXU),
        compiler_params=_params("parallel"),
        name="nsa_compress",
    )(x, embed(w1[:half]), embed(w1[half:]), tile_pos(pos[:half]), tile_pos(pos[half:]), w2b,
      gain.reshape(1, HEAD_DIM))


def _nsa_kernel(q_ref, kc_ref, vct_ref, ks_ref, vst_ref, kw_ref, vwt_ref, gate_ref,
                bdiag_ref, boff_ref, bfar_ref, bcmp_ref, qg_ref, kg_ref, ovt_ref,
                o_ref, ksn_ref, kwn_ref, m_ref, l_ref, acc_ref, ocmp_ref, sel_ref, *, seq):
    G, R, QB, D = NSA_KV_HEADS, NSA_GROUP, Q_BLOCK, HEAD_DIM
    W = R * QB
    i = pl.program_id(1)

    @pl.when(i == 0)
    def _():
        def body(c, carry):
            off = pl.multiple_of(c * QB, QB)
            for g in range(G):
                gs = slice(g * D, (g + 1) * D)
                ksn_ref[pl.ds(off, QB), gs] = _rms(ks_ref[pl.ds(off, QB), gs].astype(_F32), kg_ref[1:2, :]).astype(_MXU)
                kwn_ref[pl.ds(off, QB), gs] = _rms(kw_ref[pl.ds(off, QB), gs].astype(_F32), kg_ref[2:3, :]).astype(_MXU)
            return carry
        lax.fori_loop(0, seq // QB, body, 0)

    groups = []
    for g in range(G):
        gs = slice(g * D, (g + 1) * D)
        one = slice(g, g + 1)
        groups.append(types.SimpleNamespace(
            q=q_ref.at[g * R * D:(g + 1) * R * D, :], kc=kc_ref.at[:, :, gs], vct=vct_ref.at[:, gs, :],
            vst=vst_ref.at[gs, :], vwt=vwt_ref.at[gs, :], gate=gate_ref.at[:, gs],
            bdiag=bdiag_ref.at[one], boff=boff_ref.at[one], bfar=bfar_ref.at[one], bcmp=bcmp_ref.at[one],
            o=o_ref.at[:, g * W:(g + 1) * W], ksn=ksn_ref.at[:, gs], kwn=kwn_ref.at[:, gs],
            m=m_ref.at[g], l=l_ref.at[g], acc=acc_ref.at[g], ocmp=ocmp_ref.at[g], sel=sel_ref.at[g]))
    _nsa_groups(i, groups, qg_ref, ovt_ref, seq=seq)


def _nsa_groups(i, grp, qg_ref, ovt_ref, *, seq):
    R = NSA_GROUP
    QB = Q_BLOCK
    scale = HEAD_DIM ** -0.5
    sel_shift = int(math.log2(SEL_BLOCK))

    heads = [slice(r * QB, (r + 1) * QB) for r in range(R)]

    def norm_t(x):
        return x * lax.rsqrt(jnp.mean(x * x, axis=0, keepdims=True) + EPS) * qg_ref[...]

    for g in grp:
        g.q_all = jnp.concatenate(
            [norm_t(g.q[r * HEAD_DIM:(r + 1) * HEAD_DIM, :].astype(_F32)).astype(_MXU) for r in range(R)], axis=1)

    row = lax.broadcasted_iota(jnp.int32, (QB, QB), 0)
    col = lax.broadcasted_iota(jnp.int32, (QB, QB), 1)

    for g in grp:
        g.m[...] = jnp.full(g.m.shape, NEG, _F32)
        g.l[...] = jnp.zeros(g.l.shape, _F32)
        g.acc[...] = jnp.zeros(g.acc.shape, _F32)

    def attend_pre(jobs):
        staged = []
        for g, branch, s_all, mask, vt in jobs:
            p_parts, alphas = [], []
            for r, hs in enumerate(heads):
                slot = branch * R + r
                s = s_all[:, hs]
                if mask is not None:
                    s = jnp.where(mask, s, NEG)
                m_old = g.m[slot]
                m_new = jnp.maximum(m_old, jnp.max(s, axis=0, keepdims=True))
                alpha = jnp.exp(m_old - m_new)
                p = jnp.exp(s - m_new)
                if mask is not None:
                    p = jnp.where(mask, p, 0.0)
                g.l[slot] = alpha * g.l[slot] + jnp.sum(p, axis=0, keepdims=True)
                g.m[slot] = m_new
                p_parts.append(p.astype(_MXU))
                alphas.append(alpha)
            staged.append((g, branch, vt, jnp.concatenate(p_parts, axis=1), alphas))
        return staged

    def attend_post(staged):
        upds = [_dot(vt, p_all) for _, _, vt, p_all, _ in staged]
        for (g, branch, _, _, alphas), upd in zip(staged, upds):
            for r, hs in enumerate(heads):
                g.acc[branch, :, hs] = alphas[r] * g.acc[branch, :, hs] + upd[:, hs]

    def attend(jobs):
        attend_post(attend_pre(jobs))

    valid_c = i * QB + col >= CMP_STRIDE * row + (CMP_BLOCK - 1)
    shift = QB // CMP_STRIDE
    off_c = pl.multiple_of((seq // QB - 1 - i) * shift, shift)
    s_cs = [_dot(g.kc[0], g.q_all) * scale + g.bcmp[0, pl.ds(off_c, seq // CMP_STRIDE), :] for g in grp]

    nwb = WINDOW // QB
    rows_w = (nwb - 1) * QB
    start_w = jnp.maximum(i - nwb, 0)
    off_w = pl.multiple_of(start_w * QB, QB)
    key_w = start_w * QB + lax.broadcasted_iota(jnp.int32, (rows_w, QB), 0)
    qry_w = i * QB + lax.broadcasted_iota(jnp.int32, (rows_w, QB), 1)
    mask_w = (key_w > qry_w - WINDOW) & (key_w < (i - 1) * QB)
    s_ws = [_dot(g.kwn[pl.ds(off_w, rows_w), :], g.q_all) * scale + g.bfar[0] for g in grp]

    p_cmp, psums = [], []
    for s_c in s_cs:
        p_parts = []
        psum = jnp.zeros((QB, QB), _F32)
        for hs in heads:
            s = jnp.where(valid_c, s_c[:, hs], NEG)
            e = jnp.where(valid_c, jnp.exp(s - jnp.max(s, axis=0, keepdims=True)), 0.0)
            p = e * (1.0 / jnp.maximum(jnp.sum(e, axis=0, keepdims=True), TINY))
            psum = psum + p
            p_parts.append(p.astype(_MXU))
        p_cmp.append(jnp.concatenate(p_parts, axis=1))
        psums.append(_split(psum, 2))
    staged_w = attend_pre([(g, 1, s_w, mask_w, g.vwt[:, pl.ds(off_w, rows_w)]) for g, s_w in zip(grp, s_ws)])
    for g, p_all in zip(grp, p_cmp):
        g.ocmp[...] = _dot(g.vct[0], p_all)
    ovt = ovt_ref[...]
    imps = [_dot(ovt, hi) + _dot(ovt, lo) for hi, lo in psums]
    attend_post(staged_w)

    n_sel = seq // SEL_BLOCK
    jrow = lax.broadcasted_iota(jnp.int32, (n_sel, QB), 0)
    tcol = lax.broadcasted_iota(jnp.int32, (n_sel, QB), 1)
    qblk = lax.shift_right_arithmetic(i * QB + tcol, sel_shift)
    causal_b = jrow <= qblk
    forced = causal_b & ((jrow == 0) | (jrow >= qblk - 1))
    for g, imp in zip(grp, imps):
        score = jnp.where(forced, jnp.inf, jnp.where(causal_b, imp[:n_sel, :], -jnp.inf))
        rank = jnp.zeros((n_sel, QB), jnp.int32)
        for jj in range(n_sel):
            sc = score[jj:jj + 1, :]
            beats = (sc > score) | ((sc == score) & (jrow > jj))
            rank = rank + beats.astype(jnp.int32)
        g.sel[...] = jnp.where((rank < SEL_TOPK) & causal_b, 1.0, 0.0)

    def scores(g, k_ref, off, rows, bias):
        return _dot(k_ref[pl.ds(off, rows), :], g.q_all) * scale + bias

    def sel_mask(g, c, nblk):
        rix = lax.broadcasted_iota(jnp.int32, (nblk * QB, QB), 0)
        per_key_block = QB // SEL_BLOCK
        flags = [g.sel[pl.ds(per_key_block * c + j, 1), :] for j in range(per_key_block * nblk)]
        m = flags[-1]
        for j in reversed(range(len(flags) - 1)):
            m = jnp.where(rix < (j + 1) * SEL_BLOCK, flags[j], m)
        return m > 0.5

    lower = row <= col

    n_far = jnp.maximum(i - 1, 0)
    wide = 4

    def far_tile(c, nblk):
        off = pl.multiple_of(c * QB, QB)
        attend([(g, 0, scores(g, g.ksn, off, nblk * QB, g.bfar[0]), sel_mask(g, c, nblk),
                 g.vst[:, pl.ds(off, nblk * QB)]) for g in grp])

    def far_body(p, carry):
        far_tile(p * wide, wide)
        return carry
    lax.fori_loop(0, n_far // wide, far_body, 0)
    rem = n_far % wide

    @pl.when(rem >= 2)
    def _():
        far_tile(n_far - rem, 2)

    @pl.when(rem % 2 == 1)
    def _():
        far_tile(n_far - 1, 1)

    @pl.when(i >= 1)
    def _():
        off = pl.multiple_of((i - 1) * QB, QB)
        after_first = lax.broadcasted_iota(jnp.int32, (2 * QB, QB), 0) - QB
        tcol = lax.broadcasted_iota(jnp.int32, (2 * QB, QB), 1)
        causal = after_first <= tcol
        jobs = []
        for g in grp:
            bias = jnp.concatenate([g.boff[0], g.bdiag[0]], axis=0)
            jobs.append((g, 0, scores(g, g.ksn, off, 2 * QB, bias), None, g.vst[:, pl.ds(off, 2 * QB)]))
            jobs.append((g, 1, scores(g, g.kwn, off, 2 * QB, bias), causal, g.vwt[:, pl.ds(off, 2 * QB)]))
        jobs = [(g, br, s, (sel_mask(g, i - 1, 2) & causal) if br == 0 else m, vt) for g, br, s, m, vt in jobs]
        attend(jobs)

    @pl.when(i == 0)
    def _():
        jobs = []
        for g in grp:
            jobs.append((g, 0, scores(g, g.ksn, 0, QB, g.bdiag[0]), sel_mask(g, 0, 1) & lower, g.vst[:, 0:QB]))
            jobs.append((g, 1, scores(g, g.kwn, 0, QB, g.bdiag[0]), lower, g.vwt[:, 0:QB]))
        attend(jobs)

    for g in grp:
        gate_t = jax.nn.sigmoid(g.gate[...].T[:4 * R, :])
        for r, hs in enumerate(heads):
            g_cmp = gate_t[3 * r:3 * r + 1, :]
            g_sel = gate_t[3 * r + 1:3 * r + 2, :] * (1.0 / jnp.maximum(g.l[r], TINY))
            g_win = gate_t[3 * r + 2:3 * r + 3, :] * (1.0 / jnp.maximum(g.l[R + r], TINY))
            o_t = g_cmp * g.ocmp[:, hs] + g_sel * g.acc[0, :, hs] + g_win * g.acc[1, :, hs]
            g.o[:, hs] = o_t.T.astype(g.o.dtype)


def nsa_attention(pa, pvt, kcmp, vcmp_t, pg, tables, q_gain, k_gain, bsz, seq):
    nqb = seq // Q_BLOCK
    G, R = NSA_KV_HEADS, NSA_GROUP
    W = R * Q_BLOCK
    bdiag, boff, bfar, bcmp, overlap_t = tables
    n_cmp_rows = seq // CMP_STRIDE

    kv_w = G * HEAD_DIM

    def k_spec(seg):
        return pl.BlockSpec((seq, kv_w), lambda b, i, s=_A_BLK[seg] // G: (b, s))

    def vt_spec(seg):
        return pl.BlockSpec((kv_w, seq), lambda b, i, s=_T_BLK[seg] // G: (s, b))

    def whole(x):
        return pl.BlockSpec(x.shape, lambda b, i: (0,) * x.ndim)

    return pl.pallas_call(
        functools.partial(_nsa_kernel, seq=seq),
        grid=(bsz, nqb),
        in_specs=[pl.BlockSpec((NSA_Q, Q_BLOCK), lambda b, i, s=_T_BLK[_Q_N] * LANE // NSA_Q: (s, b * nqb + i)),
                  pl.BlockSpec((1, n_cmp_rows, kv_w), lambda b, i: (b, 0, 0)),
                  pl.BlockSpec((1, kv_w, n_cmp_rows), lambda b, i: (b, 0, 0)),
                  k_spec(_KS), vt_spec(_VS), k_spec(_KW), vt_spec(_VW),
                  pl.BlockSpec((Q_BLOCK, G * LANE), lambda b, i, s=B_COLS // (G * LANE): (b * nqb + i, s)),
                  whole(bdiag), whole(boff), whole(bfar), whole(bcmp),
                  pl.BlockSpec((HEAD_DIM, Q_BLOCK), lambda b, i: (0, 0)),
                  pl.BlockSpec((3, HEAD_DIM), lambda b, i: (0, 0)),
                  pl.BlockSpec((LANE, LANE), lambda b, i: (0, 0))],
        out_specs=pl.BlockSpec((Q_BLOCK, NSA_Q), lambda b, i: (b * nqb + i, 0)),
        out_shape=jax.ShapeDtypeStruct((bsz * seq, NSA_Q), _MXU),
        scratch_shapes=[pltpu.VMEM((seq, kv_w), _MXU),
                        pltpu.VMEM((seq, kv_w), _MXU),
                        pltpu.VMEM((G, 2 * R, 1, Q_BLOCK), _F32),
                        pltpu.VMEM((G, 2 * R, 1, Q_BLOCK), _F32),
                        pltpu.VMEM((G, 2, HEAD_DIM, W), _F32),
                        pltpu.VMEM((G, HEAD_DIM, W), _F32),
                        pltpu.VMEM((G, seq // SEL_BLOCK, Q_BLOCK), _F32)],
        compiler_params=_params("parallel", "arbitrary"),
        name="nsa_attention",
    )(pvt, kcmp, vcmp_t, pa, pvt, pa, pvt, pg, bdiag, boff, bfar, bcmp,
      jnp.broadcast_to(q_gain.reshape(HEAD_DIM, 1), (HEAD_DIM, Q_BLOCK)), k_gain, overlap_t)


def _t5_bucket(dist):
    n = jnp.maximum(dist, 0)
    exact = REL_BUCKETS // 2
    big = exact + (jnp.log(jnp.maximum(n, 1).astype(jnp.float32) / exact)
                   / math.log(REL_MAX_DIST / exact) * (REL_BUCKETS - exact)).astype(jnp.int32)
    return jnp.where(n < exact, n, jnp.minimum(big, REL_BUCKETS - 1))


def nsa_tables(rel_bias, seq):
    G, R = NSA_KV_HEADS, NSA_GROUP
    nqb = seq // Q_BLOCK
    tab_h = rel_bias.T.astype(_F32)
    s = np.arange(Q_BLOCK)[:, None]
    t = np.arange(Q_BLOCK)[None, :]

    def group_tiles(x, lead):
        nl = len(lead)
        rows = x.shape[-2]
        x = x.reshape((G, R) + lead + (rows, Q_BLOCK))
        perm = (0,) + tuple(range(2, 2 + nl)) + (2 + nl, 1, 3 + nl)
        return x.transpose(perm).reshape((G,) + lead + (rows, R * Q_BLOCK))

    bdiag = group_tiles(jnp.take(tab_h, _t5_bucket(jnp.asarray(t - s)), axis=1), ())
    boff = group_tiles(jnp.take(tab_h, _t5_bucket(jnp.asarray(Q_BLOCK + t - s)), axis=1), ())
    bfar = group_tiles(jnp.take(tab_h, _t5_bucket(jnp.full((1, Q_BLOCK), 2 * Q_BLOCK)), axis=1), ())
    lead = (nqb - 1) * Q_BLOCK // CMP_STRIDE
    rows_c = -(-(seq // CMP_STRIDE + lead) // 8) * 8
    m = np.arange(rows_c)[:, None]
    dist_c = t - (CMP_STRIDE * (m - lead) + CMP_BLOCK - 1)
    bcmp = group_tiles(jnp.take(tab_h, _t5_bucket(jnp.asarray(dist_c)), axis=1), ())
    n_cmp = (seq - CMP_BLOCK) // CMP_STRIDE + 1
    n_sel = seq // SEL_BLOCK
    c_start = np.arange(LANE) * CMP_STRIDE
    s_start = np.arange(LANE) * SEL_BLOCK
    overlap = ((c_start[:, None] < s_start[None, :] + SEL_BLOCK)
               & (c_start[:, None] + CMP_BLOCK > s_start[None, :])
               & (np.arange(LANE)[:, None] < n_cmp) & (np.arange(LANE)[None, :] < n_sel))
    return bdiag, boff, bfar, bcmp, jnp.asarray(overlap.T.astype(np.float32)).astype(_MXU)


def _sb_kernel(q_ref, k_ref, vt_ref, o_ref, acc_ref, rest_ref, *, heads):
    QB = Q_BLOCK
    i = pl.program_id(2)
    scale = HEAD_DIM ** -0.5
    row = lax.broadcasted_iota(jnp.int32, (QB, QB), 0)
    col = lax.broadcasted_iota(jnp.int32, (QB, QB), 1)
    strict = row < col
    wide = 4

    pad = 8
    r_l = lax.broadcasted_iota(jnp.int32, (QB + pad, 2 * QB), 0)
    c_l = lax.broadcasted_iota(jnp.int32, (QB + pad, 2 * QB), 1)
    later = jnp.where((jnp.where(c_l >= QB, c_l - QB, c_l) > r_l) | (r_l >= QB), 1.0, 0.0).astype(_MXU)

    def block(c, first, nblk=1):
        rows = nblk * QB
        off = pl.multiple_of(c * QB, QB)
        hss = [slice(h * HEAD_DIM, (h + 1) * HEAD_DIM) for h in range(heads)]
        zs = [_dot(k_ref[pl.ds(off, rows), hs], q_ref[hs, :]) * scale for hs in hss]
        log_bs, pieces = [], []
        for z in zs:
            log_b, _ = _log_sigmoid_pair(z)
            log_1mb = log_b - z
            if first:
                log_1mb = jnp.where(strict, log_1mb, 0.0)
            log_bs.append(log_b)
            pieces.append([jnp.concatenate(_split(log_1mb[j * QB:(j + 1) * QB, :], 2), axis=0) for j in range(nblk)])
        sums = [[_dot(later, p) for p in ps] for ps in pieces]
        betweens, totals = [], []
        for per_block in sums:
            after = None
            parts = []
            for s in reversed(per_block):
                parts.append(s[:QB, :] if after is None else s[:QB, :] + after)
                after = s[QB:QB + 1, :] if after is None else after + s[QB:QB + 1, :]
            betweens.append(parts[0] if nblk == 1 else jnp.concatenate(parts[::-1], axis=0))
            totals.append(after)
        probs = []
        for h in range(heads):
            if first:
                probs.append(jnp.where(strict, jnp.exp(log_bs[h] + betweens[h]), 0.0).astype(_MXU))
                rest_ref[h] = totals[h]
            else:
                rest = rest_ref[h]
                probs.append(jnp.exp(log_bs[h] + betweens[h] + rest).astype(_MXU))
                rest_ref[h] = rest + totals[h]
        upds = [_dot(vt_ref[hs, pl.ds(off, rows)], probs[h]) for h, hs in enumerate(hss)]
        for h in range(heads):
            acc_ref[h] = upds[h] if first else acc_ref[h] + upds[h]

    block(i, True)

    def body(n, carry):
        block(i - wide * (n + 1), False, wide)
        return carry
    lax.fori_loop(0, i // wide, body, 0)
    rem = i % wide

    @pl.when(rem >= 2)
    def _():
        block(rem - 2, False, 2)

    @pl.when(rem % 2 == 1)
    def _():
        block(0, False)
    for h in range(heads):
        o_ref[:, h * HEAD_DIM:(h + 1) * HEAD_DIM] = acc_ref[h].T.astype(o_ref.dtype)


def stick_breaking(pa, pvt, bsz, seq, heads=SB_HEADS):
    nqb = seq // Q_BLOCK
    w = heads * HEAD_DIM
    qb, kb, vb = _T_BLK[_Q_S] * LANE // w, _A_BLK[_K_S] * LANE // w, _T_BLK[_V_S] * LANE // w
    return pl.pallas_call(
        functools.partial(_sb_kernel, heads=heads),
        grid=(bsz, SB_HEADS // heads, nqb),
        in_specs=[pl.BlockSpec((w, Q_BLOCK), lambda b, h, i: (qb + h, b * nqb + i)),
                  pl.BlockSpec((seq, w), lambda b, h, i: (b, kb + h)),
                  pl.BlockSpec((w, seq), lambda b, h, i: (vb + h, b))],
        out_specs=pl.BlockSpec((Q_BLOCK, w), lambda b, h, i: (b * nqb + i, h)),
        out_shape=jax.ShapeDtypeStruct((bsz * seq, SB_W), _MXU),
        scratch_shapes=[pltpu.VMEM((heads, HEAD_DIM, Q_BLOCK), _F32),
                        pltpu.VMEM((heads, 1, Q_BLOCK), _F32)],
        compiler_params=_params("parallel", "parallel", "arbitrary"),
        name="stick_breaking",
    )(pvt, pa, pvt)


def _hgrn_kernel(q_ref, f_ref, i_ref, g_ref, llb_ref, l1lb_ref, omlb_ref, ng_ref, o_ref, state_ref, *, seq, heads):
    C, SUB = HG_CHUNK, HG_SUB
    nsub = C // SUB
    hss = [slice(h * HG_DK, (h + 1) * HG_DK) for h in range(heads)]
    row = lax.broadcasted_iota(jnp.int32, (C, C), 0)
    col = lax.broadcasted_iota(jnp.int32, (C, C), 1)
    upto = jnp.where(col <= row, 1.0, 0.0).astype(_MXU)
    sub_t = lax.broadcasted_iota(jnp.int32, (SUB, 1), 0)
    sub_c = lax.broadcasted_iota(jnp.int32, (SUB, C), 1)
    state_ref[...] = jnp.zeros(state_ref.shape, _F32)

    def chunk(n, carry):
        off = pl.multiple_of(n * C, C)
        fps = [f_ref[pl.ds(off, C), hs] for hs in hss]
        pieces = []
        for fp, hs in zip(fps, hss):
            log_sig, _ = _log_sigmoid_pair(fp)
            y = l1lb_ref[:, hs] + log_sig
            a = llb_ref[:, hs]
            log_f = jnp.maximum(a, y) + jnp.log1p(jnp.exp(-jnp.abs(a - y)))
            pieces.append(_split(log_f, 3))
        bs = [_dot(upto, p[0]) + _dot(upto, p[1]) + _dot(upto, p[2]) for p in pieces]
        qs = [q_ref[pl.ds(off, C), hs].astype(_F32) for hs in hss]
        ks = [omlb_ref[:, hs] * jax.nn.sigmoid(-fp) for fp, hs in zip(fps, hss)]
        vs = [i_ref[pl.ds(off, C), hs].astype(_F32) for hs in hss]

        outs = [_dot_nt((q * jnp.exp(b)).astype(_MXU), state_ref[h].astype(_MXU))
                for h, (q, b) in enumerate(zip(qs, bs))]

        lhs, rhs = [], []
        for q, k, b in zip(qs, ks, bs):
            for s_i in range(1, nsub):
                lo = s_i * SUB
                b_ref = b[lo - 1:lo, :]
                lhs.append((q[lo:lo + SUB, :] * jnp.exp(b[lo:lo + SUB, :] - b_ref)).astype(_MXU))
                rhs.append((k * jnp.exp(jnp.minimum(b_ref - b, 0.0))).astype(_MXU))
        a_blocks = [_dot_nt(x, y) for x, y in zip(lhs, rhs)]
        intra = []
        for h in range(heads):
            rows = [jnp.zeros((SUB, C), _F32)]
            for s_i in range(1, nsub):
                rows.append(jnp.where(sub_c < s_i * SUB, a_blocks[h * (nsub - 1) + s_i - 1], 0.0))
            intra.append(jnp.concatenate(rows, axis=0).astype(_MXU))
        outs = [o + _dot(a, v.astype(_MXU)) for o, a, v in zip(outs, intra, vs)]

        for h, (q, k, v, b) in enumerate(zip(qs, ks, vs, bs)):
            diag = []
            for s_i in range(nsub):
                lo = s_i * SUB
                b_i = b[lo:lo + SUB, :]
                q_i = q[lo:lo + SUB, :]
                o_d = jnp.zeros((SUB, HG_DV), _F32)
                for s in range(SUB):
                    w = jnp.exp(b_i - b[lo + s:lo + s + 1, :])
                    a_col = jnp.sum(q_i * k[lo + s:lo + s + 1, :] * w, axis=-1, keepdims=True)
                    a_col = jnp.where(sub_t >= s, a_col, 0.0)
                    o_d = o_d + a_col * v[lo + s:lo + s + 1, :]
                diag.append(o_d)
            outs[h] = outs[h] + jnp.concatenate(diag, axis=0)

        k_decs = [(k * jnp.exp(b[C - 1:C, :] - b)).astype(_MXU) for k, b in zip(ks, bs)]
        grown = [_dot(v.T.astype(_MXU), kd) for v, kd in zip(vs, k_decs)]
        for h, (b, hs) in enumerate(zip(bs, hss)):
            state_ref[h] = state_ref[h] * jnp.exp(b[C - 1:C, :]) + grown[h]
            gate = jax.nn.sigmoid(g_ref[pl.ds(off, C), hs])
            o_ref[pl.ds(off, C), hs] = (_rms(outs[h], ng_ref[...]) * gate).astype(o_ref.dtype)
        return carry

    lax.fori_loop(0, seq // C, chunk, 0)


def hgrn2(pa, pb, lb, norm_gain, bsz, seq, heads=4):
    lb = lb.reshape(1, HG_KW).astype(_F32)
    log_lb = jnp.log(lb)
    log_1mlb = jnp.log1p(-lb)
    one_mlb = 1.0 - lb
    w = heads * HG_DK

    def seq_spec(blk):
        return pl.BlockSpec((seq, w), lambda b, h, s=blk * LANE // w: (b, s + h))

    head_vec = pl.BlockSpec((1, w), lambda b, h: (0, h))
    return pl.pallas_call(
        functools.partial(_hgrn_kernel, seq=seq, heads=heads),
        grid=(bsz, HG_HEADS // heads),
        in_specs=[seq_spec(_A_BLK[_Q_H]), seq_spec(_B_BLK[_F_H]), seq_spec(_A_BLK[_I_H]), seq_spec(_B_BLK[_G_H]),
                  head_vec, head_vec, head_vec,
                  pl.BlockSpec((1, HG_DV), lambda b, h: (0, 0))],
        out_specs=pl.BlockSpec((seq, w), lambda b, h: (b, h)),
        out_shape=jax.ShapeDtypeStruct((bsz * seq, HG_VW), _MXU),
        scratch_shapes=[pltpu.VMEM((heads, HG_DV, HG_DK), _F32)],
        compiler_params=_params("parallel", "parallel"),
        name="hgrn2",
    )(pa, pb, pa, pb, log_lb, log_1mlb, one_mlb, norm_gain.reshape(1, HG_DV))


def _merge_kernel(o0_ref, o1_ref, o2_ref, w0_ref, w1_ref, w2_ref, g0_ref, g1_ref, g2_ref, out_ref):
    ys = [_dot(o_ref[...], w_ref[0]) for o_ref, w_ref in ((o0_ref, w0_ref), (o1_ref, w1_ref), (o2_ref, w2_ref))]
    acc = jax.nn.sigmoid(g0_ref[...].astype(_F32)) * ys[0]
    acc = acc + jax.nn.sigmoid(g1_ref[...].astype(_F32)) * ys[1]
    acc = acc + jax.nn.sigmoid(g2_ref[...].astype(_F32)) * ys[2]
    out_ref[...] = acc.astype(out_ref.dtype)


def merge_branches(o_nsa, o_sb, o_hg, w_branch, gates, tm=1024, tn=512):
    m = o_nsa.shape[0]
    nj = D_MODEL // tn
    gblk = 0

    def o_spec():
        return pl.BlockSpec((tm, BRANCH_W), lambda i, j: (i, 0))

    def w_spec(n):
        return pl.BlockSpec((1, BRANCH_W, tn), lambda i, j, n=n: (n, 0, j))

    def g_spec(n):
        return pl.BlockSpec((tm, tn), lambda i, j, n=n: (i, gblk + n * nj + j))

    return pl.pallas_call(
        _merge_kernel,
        grid=(m // tm, nj),
        in_specs=[o_spec(), o_spec(), o_spec(), w_spec(0), w_spec(1), w_spec(2), g_spec(0), g_spec(1), g_spec(2)],
        out_specs=pl.BlockSpec((tm, tn), lambda i, j: (i, j)),
        out_shape=jax.ShapeDtypeStruct((m, D_MODEL), _MXU),
        compiler_params=_params("parallel", "arbitrary"),
        name="merge_branches",
    )(o_nsa, o_sb, o_hg, w_branch, w_branch, w_branch, gates, gates, gates)


def _ffn_up_kernel(x_ref, halo_ref, gain_ref, wg_ref, wv_ref, cwg_ref, cwv_ref, cbg_ref, cbv_ref, o_ref, h_ref,
                   *, tiles_per_seq, sub):
    i = pl.program_id(0)

    @pl.when(pl.program_id(1) == 0)
    def _():
        h_ref[...] = _rms(x_ref[...], gain_ref[...]).astype(_MXU)

    first = (i % tiles_per_seq) == 0
    h = h_ref[...]
    halo = _rms(halo_ref[...], gain_ref[...]).astype(_MXU)
    tm = h.shape[0]
    hrows = halo.shape[0]
    row = lax.broadcasted_iota(jnp.int32, (tm, 1), 0)
    cols = [slice(c * sub, (c + 1) * sub) for c in range(wg_ref.shape[1] // sub)]

    prods = [[(_dot(h, w_ref[:, cs]), jnp.where(first, 0.0, _dot(halo, w_ref[:, cs]))) for w_ref in (wg_ref, wv_ref)]
             for cs in cols]

    def conv(u, uh, cw, cb):
        u1 = jnp.where(row == 0, uh[hrows - 1:hrows, :], pltpu.roll(u, 1, axis=0))
        u2 = jnp.where(row == 0, uh[hrows - 2:hrows - 1, :],
                       jnp.where(row == 1, uh[hrows - 1:hrows, :], pltpu.roll(u, 2, axis=0)))
        return cb + u2 * cw[0:1, :] + u1 * cw[1:2, :] + u * cw[2:3, :]

    for cs, ((ug, uhg), (uv, uhv)) in zip(cols, prods):
        gate = conv(ug, uhg, cwg_ref[:, cs], cbg_ref[:, cs])
        val = conv(uv, uhv, cwv_ref[:, cs], cbv_ref[:, cs])
        o_ref[:, cs] = (jax.nn.silu(gate) * val).astype(o_ref.dtype)


def ffn_up(x, gain, w_up, conv_w, conv_b, seq, tm=1024, tn=512, sub=256):
    m, k = x.shape
    nj = D_FF // tn
    hrows = 8
    conv_b = conv_b.reshape(1, 2 * D_FF)
    return pl.pallas_call(
        functools.partial(_ffn_up_kernel, tiles_per_seq=seq // tm, sub=sub),
        grid=(m // tm, nj),
        in_specs=[pl.BlockSpec((tm, k), lambda i, j: (i, 0)),
                  pl.BlockSpec((hrows, k), lambda i, j: (jnp.maximum(i * (tm // hrows) - 1, 0), 0)),
                  pl.BlockSpec((1, k), lambda i, j: (0, 0)),
                  pl.BlockSpec((k, tn), lambda i, j: (0, j)),
                  pl.BlockSpec((k, tn), lambda i, j: (0, nj + j)),
                  pl.BlockSpec((CONV_W, tn), lambda i, j: (0, j)),
                  pl.BlockSpec((CONV_W, tn), lambda i, j: (0, nj + j)),
                  pl.BlockSpec((1, tn), lambda i, j: (0, j)),
                  pl.BlockSpec((1, tn), lambda i, j: (0, nj + j))],
        out_specs=pl.BlockSpec((tm, tn), lambda i, j: (i, j)),
        out_shape=jax.ShapeDtypeStruct((m, D_FF), _MXU),
        scratch_shapes=[pltpu.VMEM((tm, k), _MXU)],
        compiler_params=_params("parallel", "arbitrary"),
        name="ffn_up_conv",
    )(x, x, gain.reshape(1, k), w_up, w_up, conv_w, conv_w, conv_b, conv_b)


def _gather_cols(w, segs):
    return jnp.concatenate([w[:, _SPLIT_OFF[s]:_SPLIT_OFF[s + 1]] for s in segs], axis=1)


def _nsa_gate_cols(w):
    g = w[:, _SPLIT_OFF[_G_N]:_SPLIT_OFF[_G_N + 1]].reshape(-1, NSA_KV_HEADS, NSA_GROUP * 3)
    g = jnp.pad(g, ((0, 0), (0, 0), (0, LANE - NSA_GROUP * 3)))
    return g.reshape(-1, NSA_KV_HEADS * LANE)


def kernel(x, norm_attn, w_in, nsa_q_gain, nsa_k_gain, cmp_pos, cmp_w1, cmp_w2, rel_bias, hg_lower_bound,
           hg_norm_gain, w_branch, w_out, norm_ffn, w_up, conv_w, conv_b, w_down):
    bsz, seq, d = x.shape
    m = bsz * seq
    lb_all = jnp.cumsum(jax.nn.softmax(hg_lower_bound.astype(_F32), axis=0), axis=0)
    lb_all = jnp.maximum(lb_all - lb_all[0:1], 0.0)
    tables = nsa_tables(rel_bias, seq)
    xf = x.reshape(m, d)
    for l in range(w_in.shape[0]):
        wl = w_in[l]
        h = rmsnorm_cast(xf, norm_attn[l])
        pa = matmul(h, _gather_cols(wl, _A_ORDER).astype(_MXU), _MXU, tm=1024, tn=A_COLS // 2, name="proj_a")
        w_b = jnp.concatenate([_gather_cols(wl, _B_ORDER), _nsa_gate_cols(wl)], axis=1).astype(_MXU)
        pb = matmul(h, w_b, _F32, tn=w_b.shape[1] // 3, name="proj_b")
        gates = matmul(h, _gather_cols(wl, (_GATES,)).astype(_MXU), _MXU, tn=1024, name="proj_gates")
        pkc = matmul(h, _gather_cols(wl, (_KC,)).astype(_MXU), _MXU, name="proj_kc")
        pvc = matmul(h, _gather_cols(wl, (_VC,)).astype(_MXU), _MXU, name="proj_vc")
        blk_rows = seq // CMP_STRIDE
        kcmp = compress(pkc.reshape(bsz, blk_rows, CMP_STRIDE * NSA_KV), cmp_w1[l, 0], cmp_w2[l, 0], cmp_pos[l, 0],
                        nsa_k_gain[l, 0], True)
        vcmp_t = compress(pvc.reshape(bsz, blk_rows, CMP_STRIDE * NSA_KV), cmp_w1[l, 1], cmp_w2[l, 1], cmp_pos[l, 1],
                          nsa_k_gain[l, 0], False)
        pvt = matmul_nt(_gather_cols(wl, _T_ORDER).T.astype(_MXU), h, _MXU, name="proj_vt")
        o_nsa = nsa_attention(pa, pvt, kcmp, vcmp_t, pb, tables, nsa_q_gain[l], nsa_k_gain[l], bsz, seq)
        o_sb = stick_breaking(pa, pvt, bsz, seq)
        o_hg = hgrn2(pa, pb, lb_all[l], hg_norm_gain[l], bsz, seq)
        merged = merge_branches(o_nsa, o_sb, o_hg, w_branch[l].astype(_MXU), gates)
        xf = matmul(merged, w_out[l].astype(_MXU), _F32, residual=xf, tm=512, tn=D_MODEL, name="out_proj")
        act = ffn_up(xf, norm_ffn[l], w_up[l].astype(_MXU), conv_w[l], conv_b[l], seq)
        xf = matmul(act, w_down[l].astype(_MXU), _F32, residual=xf, tm=1024, tn=512, name="ffn_down")
    return xf.reshape(bsz, seq, d)
```

```python
import functools
import math
import types

import numpy as np
import jax
import jax.numpy as jnp
from jax import lax
from jax.experimental import pallas as pl
from jax.experimental.pallas import tpu as pltpu

D_MODEL = 2048
DEPTH = 4
HEAD_DIM = 128
BRANCH_W = D_MODEL // 2
N_BRANCH = 3
NSA_HEADS = BRANCH_W // HEAD_DIM
NSA_KV_HEADS = NSA_HEADS // 4
NSA_GROUP = NSA_HEADS // NSA_KV_HEADS
CMP_BLOCK = 32
CMP_STRIDE = 16
SEL_BLOCK = 64
SEL_TOPK = 16
WINDOW = 512
Q_BLOCK = 128
SB_HEADS = BRANCH_W // HEAD_DIM
HG_DK = 128
HG_DV = 128
HG_HEADS = BRANCH_W // HG_DV
HG_CHUNK = 64
HG_SUB = 8
D_FF = 256 * ((8 * D_MODEL // 3 + 255) // 256)
CONV_W = 3
REL_BUCKETS = 32
REL_MAX_DIST = 128
EPS = 1e-6
NEG = -1e30
TINY = 1e-30

NSA_Q = NSA_HEADS * HEAD_DIM
NSA_KV = NSA_KV_HEADS * HEAD_DIM
NSA_GATE = 3 * NSA_HEADS
SB_W = SB_HEADS * HEAD_DIM
HG_KW = HG_HEADS * HG_DK
HG_VW = HG_HEADS * HG_DV
SPLIT_SIZES = (NSA_Q,) + (NSA_KV,) * 6 + (NSA_GATE,) + (SB_W,) * 3 + (HG_KW, HG_KW, HG_VW, HG_VW, N_BRANCH * D_MODEL)
(_Q_N, _KC, _VC, _KS, _VS, _KW, _VW, _G_N, _Q_S, _K_S, _V_S, _Q_H, _F_H, _I_H, _G_H, _GATES) = range(16)
_SPLIT_OFF = np.concatenate([[0], np.cumsum(SPLIT_SIZES)]).tolist()

LANE = 128
_A_ORDER = (_K_S, _Q_H, _I_H, _KS, _KW)
_A_BLK = {}
_off = 0
for _s in _A_ORDER:
    _A_BLK[_s] = _off // LANE
    _off += SPLIT_SIZES[_s]
A_COLS = _off
_T_ORDER = (_V_S, _Q_S, _Q_N, _VS, _VW)
_T_BLK = {}
_off = 0
for _s in _T_ORDER:
    _T_BLK[_s] = _off // LANE
    _off += SPLIT_SIZES[_s]
T_ROWS = _off
_B_ORDER = (_F_H, _G_H)
_B_BLK = {}
_off = 0
for _s in _B_ORDER:
    _B_BLK[_s] = _off // LANE
    _off += SPLIT_SIZES[_s]
B_COLS = _off

VMEM_LIMIT = 48 * 1024 * 1024

_MXU = jnp.bfloat16
_F32 = jnp.float32


def _dot(a, b):
    return jnp.dot(a, b, preferred_element_type=_F32)


def _dot_nt(a, b):
    return lax.dot_general(a, b, (((1,), (1,)), ((), ())), preferred_element_type=_F32)


def _split(x, parts):
    out = []
    r = x
    for _ in range(parts):
        h = r.astype(_MXU)
        out.append(h)
        r = r - h.astype(_F32)
    return out


def _params(*sem):
    return pltpu.CompilerParams(dimension_semantics=sem, vmem_limit_bytes=VMEM_LIMIT)


def _rms(x, gain):
    return x * lax.rsqrt(jnp.mean(x * x, axis=-1, keepdims=True) + EPS) * gain


def _log_sigmoid_pair(z):
    lp = jnp.log(1.0 + jnp.exp(-jnp.abs(z)))
    return jnp.minimum(z, 0.0) - lp, jnp.minimum(-z, 0.0) - lp


def _rmsnorm_kernel(x_ref, g_ref, o_ref):
    o_ref[...] = _rms(x_ref[...], g_ref[...]).astype(o_ref.dtype)


def rmsnorm_cast(x2d, gain):
    m, d = x2d.shape
    tm = 512
    return pl.pallas_call(
        _rmsnorm_kernel,
        grid=(m // tm,),
        in_specs=[pl.BlockSpec((tm, d), lambda i: (i, 0)),
                  pl.BlockSpec((1, d), lambda i: (0, 0))],
        out_specs=pl.BlockSpec((tm, d), lambda i: (i, 0)),
        out_shape=jax.ShapeDtypeStruct((m, d), _MXU),
        compiler_params=_params("parallel"),
        name="rmsnorm_cast",
    )(x2d, gain.reshape(1, d))


def _mm_kernel(a_ref, w_ref, o_ref):
    o_ref[...] = _dot(a_ref[...], w_ref[...]).astype(o_ref.dtype)


def _mm_res_kernel(a_ref, w_ref, r_ref, o_ref):
    o_ref[...] = (r_ref[...] + _dot(a_ref[...], w_ref[...])).astype(o_ref.dtype)


def matmul(a, w, out_dtype, residual=None, tm=1024, tn=512, name="matmul"):
    m, k = a.shape
    n = w.shape[1]
    tm = min(tm, m)
    tn = min(tn, n)
    in_specs = [pl.BlockSpec((tm, k), lambda i, j: (i, 0)),
                pl.BlockSpec((k, tn), lambda i, j: (0, j))]
    args = [a, w]
    kern = _mm_kernel
    if residual is not None:
        in_specs.append(pl.BlockSpec((tm, tn), lambda i, j: (i, j)))
        args.append(residual)
        kern = _mm_res_kernel
    return pl.pallas_call(
        kern,
        grid=(m // tm, n // tn),
        in_specs=in_specs,
        out_specs=pl.BlockSpec((tm, tn), lambda i, j: (i, j)),
        out_shape=jax.ShapeDtypeStruct((m, n), out_dtype),
        compiler_params=_params("parallel", "arbitrary"),
        name=name,
    )(*args)


def _mm_nt_kernel(wt_ref, a_ref, o_ref):
    o_ref[...] = _dot_nt(wt_ref[...], a_ref[...]).astype(o_ref.dtype)


def matmul_nt(wt, a, out_dtype, tm=1024, tn=512, name="matmul_nt"):
    n, k = wt.shape
    m = a.shape[0]
    tm = min(tm, m)
    tn = min(tn, n)
    return pl.pallas_call(
        _mm_nt_kernel,
        grid=(m // tm, n // tn),
        in_specs=[pl.BlockSpec((tn, k), lambda i, j: (j, 0)),
                  pl.BlockSpec((tm, k), lambda i, j: (i, 0))],
        out_specs=pl.BlockSpec((tn, tm), lambda i, j: (j, i)),
        out_shape=jax.ShapeDtypeStruct((n, m), out_dtype),
        compiler_params=_params("parallel", "arbitrary"),
        name=name,
    )(wt, a)


def _compress_kernel(x_ref, wa_ref, wb_ref, pa_ref, pb_ref, w2_ref, gain_ref, o_ref, *, normalize):
    x = x_ref[0].astype(_F32)
    u = _dot((x + pa_ref[...]).astype(_MXU), wa_ref[...])
    v = _dot((x + pb_ref[...]).astype(_MXU), wb_ref[...])
    hid = jax.nn.gelu(u + pltpu.roll(v, v.shape[0] - 1, axis=0)).astype(_MXU)
    if normalize:
        out = _dot(hid, w2_ref[...])
        out = jnp.concatenate(
            [_rms(out[:, g * HEAD_DIM:(g + 1) * HEAD_DIM], gain_ref[...]) for g in range(NSA_KV_HEADS)], axis=1)
    else:
        out = _dot_nt(w2_ref[...], hid)
    o_ref[0] = out.astype(o_ref.dtype)


def compress(x, w1, w2, pos, gain, normalize):
    bsz = x.shape[0]
    nrow = x.shape[1]
    kdim = x.shape[2]
    half = CMP_BLOCK // 2
    eye = jnp.eye(NSA_KV_HEADS, dtype=w1.dtype)

    def embed(w):
        return jnp.einsum('lde,gh->lgdhe', w, eye).reshape(kdim, NSA_KV).astype(_MXU)

    def tile_pos(p):
        return jnp.broadcast_to(p[:, None, :], (half, NSA_KV_HEADS, HEAD_DIM)).reshape(1, kdim)

    w2b = jnp.einsum('de,gh->gdhe', w2, eye).reshape(NSA_KV, NSA_KV).astype(_MXU)
    out_block = (1, nrow, NSA_KV)
    if not normalize:
        w2b = w2b.T
        out_block = (1, NSA_KV, nrow)
    return pl.pallas_call(
        functools.partial(_compress_kernel, normalize=normalize),
        grid=(bsz,),
        in_specs=[pl.BlockSpec((1, nrow, kdim), lambda b: (b, 0, 0)),
                  pl.BlockSpec((kdim, NSA_KV), lambda b: (0, 0)),
                  pl.BlockSpec((kdim, NSA_KV), lambda b: (0, 0)),
                  pl.BlockSpec((1, kdim), lambda b: (0, 0)),
                  pl.BlockSpec((1, kdim), lambda b: (0, 0)),
                  pl.BlockSpec((NSA_KV, NSA_KV), lambda b: (0, 0)),
                  pl.BlockSpec((1, HEAD_DIM), lambda b: (0, 0))],
        out_specs=pl.BlockSpec(out_block, lambda b: (b, 0, 0)),
        out_shape=jax.ShapeDtypeStruct((bsz,) + out_block[1:], _MXU),
        compiler_params=_params("parallel"),
        name="nsa_compress",
    )(x, embed(w1[:half]), embed(w1[half:]), tile_pos(pos[:half]), tile_pos(pos[half:]), w2b,
      gain.reshape(1, HEAD_DIM))


def _nsa_kernel(q_ref, kc_ref, vct_ref, ks_ref, vst_ref, kw_ref, vwt_ref, gate_ref,
                bdiag_ref, boff_ref, bfar_ref, bcmp_ref, qg_ref, kg_ref, ovt_ref,
                o_ref, ksn_ref, kwn_ref, m_ref, l_ref, acc_ref, ocmp_ref, sel_ref, *, seq):
    G, R, QB, D = NSA_KV_HEADS, NSA_GROUP, Q_BLOCK, HEAD_DIM
    W = R * QB
    i = pl.program_id(1)

    @pl.when(i == 0)
    def _():
        def body(c, carry):
            off = pl.multiple_of(c * QB, QB)
            for g in range(G):
                gs = slice(g * D, (g + 1) * D)
                ksn_ref[pl.ds(off, QB), gs] = _rms(ks_ref[pl.ds(off, QB), gs].astype(_F32), kg_ref[1:2, :]).astype(_MXU)
                kwn_ref[pl.ds(off, QB), gs] = _rms(kw_ref[pl.ds(off, QB), gs].astype(_F32), kg_ref[2:3, :]).astype(_MXU)
            return carry
        lax.fori_loop(0, seq // QB, body, 0)

    groups = []
    for g in range(G):
        gs = slice(g * D, (g + 1) * D)
        one = slice(g, g + 1)
        groups.append(types.SimpleNamespace(
            q=q_ref.at[g * R * D:(g + 1) * R * D, :], kc=kc_ref.at[:, :, gs], vct=vct_ref.at[:, gs, :],
            vst=vst_ref.at[gs, :], vwt=vwt_ref.at[gs, :], gate=gate_ref.at[:, gs],
            bdiag=bdiag_ref.at[one], boff=boff_ref.at[one], bfar=bfar_ref.at[one], bcmp=bcmp_ref.at[one],
            o=o_ref.at[:, g * W:(g + 1) * W], ksn=ksn_ref.at[:, gs], kwn=kwn_ref.at[:, gs],
            m=m_ref.at[g], l=l_ref.at[g], acc=acc_ref.at[g], ocmp=ocmp_ref.at[g], sel=sel_ref.at[g]))
    _nsa_groups(i, groups, qg_ref, ovt_ref, seq=seq)


def _nsa_groups(i, grp, qg_ref, ovt_ref, *, seq):
    R = NSA_GROUP
    QB = Q_BLOCK
    scale = HEAD_DIM ** -0.5
    sel_shift = int(math.log2(SEL_BLOCK))

    heads = [slice(r * QB, (r + 1) * QB) for r in range(R)]

    def norm_t(x):
        return x * lax.rsqrt(jnp.mean(x * x, axis=0, keepdims=True) + EPS) * qg_ref[...]

    for g in grp:
        g.q_all = jnp.concatenate(
            [norm_t(g.q[r * HEAD_DIM:(r + 1) * HEAD_DIM, :].astype(_F32)).astype(_MXU) for r in range(R)], axis=1)

    row = lax.broadcasted_iota(jnp.int32, (QB, QB), 0)
    col = lax.broadcasted_iota(jnp.int32, (QB, QB), 1)

    for g in grp:
        g.m[...] = jnp.full(g.m.shape, NEG, _F32)
        g.l[...] = jnp.zeros(g.l.shape, _F32)
        g.acc[...] = jnp.zeros(g.acc.shape, _F32)

    def attend_pre(jobs):
        staged = []
        for g, branch, s_all, mask, vt in jobs:
            p_parts, alphas = [], []
            for r, hs in enumerate(heads):
                slot = branch * R + r
                s = s_all[:, hs]
                if mask is not None:
                    s = jnp.where(mask, s, NEG)
                m_old = g.m[slot]
                m_new = jnp.maximum(m_old, jnp.max(s, axis=0, keepdims=True))
                alpha = jnp.exp(m_old - m_new)
                p = jnp.exp(s - m_new)
                if mask is not None:
                    p = jnp.where(mask, p, 0.0)
                g.l[slot] = alpha * g.l[slot] + jnp.sum(p, axis=0, keepdims=True)
                g.m[slot] = m_new
                p_parts.append(p.astype(_MXU))
                alphas.append(alpha)
            staged.append((g, branch, vt, jnp.concatenate(p_parts, axis=1), alphas))
        return staged

    def attend_post(staged):
        upds = [_dot(vt, p_all) for _, _, vt, p_all, _ in staged]
        for (g, branch, _, _, alphas), upd in zip(staged, upds):
            for r, hs in enumerate(heads):
                g.acc[branch, :, hs] = alphas[r] * g.acc[branch, :, hs] + upd[:, hs]

    def attend(jobs):
        attend_post(attend_pre(jobs))

    valid_c = i * QB + col >= CMP_STRIDE * row + (CMP_BLOCK - 1)
    shift = QB // CMP_STRIDE
    off_c = pl.multiple_of((seq // QB - 1 - i) * shift, shift)
    s_cs = [_dot(g.kc[0], g.q_all) * scale + g.bcmp[0, pl.ds(off_c, seq // CMP_STRIDE), :] for g in grp]

    nwb = WINDOW // QB
    rows_w = (nwb - 1) * QB
    start_w = jnp.maximum(i - nwb, 0)
    off_w = pl.multiple_of(start_w * QB, QB)
    key_w = start_w * QB + lax.broadcasted_iota(jnp.int32, (rows_w, QB), 0)
    qry_w = i * QB + lax.broadcasted_iota(jnp.int32, (rows_w, QB), 1)
    mask_w = (key_w > qry_w - WINDOW) & (key_w < (i - 1) * QB)
    s_ws = [_dot(g.kwn[pl.ds(off_w, rows_w), :], g.q_all) * scale + g.bfar[0] for g in grp]

    p_cmp, psums = [], []
    for s_c in s_cs:
        p_parts = []
        psum = jnp.zeros((QB, QB), _F32)
        for hs in heads:
            s = jnp.where(valid_c, s_c[:, hs], NEG)
            e = jnp.where(valid_c, jnp.exp(s - jnp.max(s, axis=0, keepdims=True)), 0.0)
            p = e * (1.0 / jnp.maximum(jnp.sum(e, axis=0, keepdims=True), TINY))
            psum = psum + p
            p_parts.append(p.astype(_MXU))
        p_cmp.append(jnp.concatenate(p_parts, axis=1))
        psums.append(_split(psum, 2))
    staged_w = attend_pre([(g, 1, s_w, mask_w, g.vwt[:, pl.ds(off_w, rows_w)]) for g, s_w in zip(grp, s_ws)])
    for g, p_all in zip(grp, p_cmp):
        g.ocmp[...] = _dot(g.vct[0], p_all)
    ovt = ovt_ref[...]
    imps = [_dot(ovt, hi) + _dot(ovt, lo) for hi, lo in psums]
    attend_post(staged_w)

    n_sel = seq // SEL_BLOCK
    jrow = lax.broadcasted_iota(jnp.int32, (n_sel, QB), 0)
    tcol = lax.broadcasted_iota(jnp.int32, (n_sel, QB), 1)
    qblk = lax.shift_right_arithmetic(i * QB + tcol, sel_shift)
    causal_b = jrow <= qblk
    forced = causal_b & ((jrow == 0) | (jrow >= qblk - 1))
    for g, imp in zip(grp, imps):
        score = jnp.where(forced, jnp.inf, jnp.where(causal_b, imp[:n_sel, :], -jnp.inf))
        rank = jnp.zeros((n_sel, QB), jnp.int32)
        for jj in range(n_sel):
            sc = score[jj:jj + 1, :]
            beats = (sc > score) | ((sc == score) & (jrow > jj))
            rank = rank + beats.astype(jnp.int32)
        g.sel[...] = jnp.where((rank < SEL_TOPK) & causal_b, 1.0, 0.0)

    def scores(g, k_ref, off, rows, bias):
        return _dot(k_ref[pl.ds(off, rows), :], g.q_all) * scale + bias

    def sel_mask(g, c, nblk):
        rix = lax.broadcasted_iota(jnp.int32, (nblk * QB, QB), 0)
        per_key_block = QB // SEL_BLOCK
        flags = [g.sel[pl.ds(per_key_block * c + j, 1), :] for j in range(per_key_block * nblk)]
        m = flags[-1]
        for j in reversed(range(len(flags) - 1)):
            m = jnp.where(rix < (j + 1) * SEL_BLOCK, flags[j], m)
        return m > 0.5

    lower = row <= col

    n_far = jnp.maximum(i - 1, 0)
    wide = 4

    def far_tile(c, nblk):
        off = pl.multiple_of(c * QB, QB)
        attend([(g, 0, scores(g, g.ksn, off, nblk * QB, g.bfar[0]), sel_mask(g, c, nblk),
                 g.vst[:, pl.ds(off, nblk * QB)]) for g in grp])

    def far_body(p, carry):
        far_tile(p * wide, wide)
        return carry
    lax.fori_loop(0, n_far // wide, far_body, 0)
    rem = n_far % wide

    @pl.when(rem >= 2)
    def _():
        far_tile(n_far - rem, 2)

    @pl.when(rem % 2 == 1)
    def _():
        far_tile(n_far - 1, 1)

    @pl.when(i >= 1)
    def _():
        off = pl.multiple_of((i - 1) * QB, QB)
        after_first = lax.broadcasted_iota(jnp.int32, (2 * QB, QB), 0) - QB
        tcol = lax.broadcasted_iota(jnp.int32, (2 * QB, QB), 1)
        causal = after_first <= tcol
        jobs = []
        for g in grp:
            bias = jnp.concatenate([g.boff[0], g.bdiag[0]], axis=0)
            jobs.append((g, 0, scores(g, g.ksn, off, 2 * QB, bias), None, g.vst[:, pl.ds(off, 2 * QB)]))
            jobs.append((g, 1, scores(g, g.kwn, off, 2 * QB, bias), causal, g.vwt[:, pl.ds(off, 2 * QB)]))
        jobs = [(g, br, s, (sel_mask(g, i - 1, 2) & causal) if br == 0 else m, vt) for g, br, s, m, vt in jobs]
        attend(jobs)

    @pl.when(i == 0)
    def _():
        jobs = []
        for g in grp:
            jobs.append((g, 0, scores(g, g.ksn, 0, QB, g.bdiag[0]), sel_mask(g, 0, 1) & lower, g.vst[:, 0:QB]))
            jobs.append((g, 1, scores(g, g.kwn, 0, QB, g.bdiag[0]), lower, g.vwt[:, 0:QB]))
        attend(jobs)

    for g in grp:
        gate_t = jax.nn.sigmoid(g.gate[...].T[:4 * R, :])
        for r, hs in enumerate(heads):
            g_cmp = gate_t[3 * r:3 * r + 1, :]
            g_sel = gate_t[3 * r + 1:3 * r + 2, :] * (1.0 / jnp.maximum(g.l[r], TINY))
            g_win = gate_t[3 * r + 2:3 * r + 3, :] * (1.0 / jnp.maximum(g.l[R + r], TINY))
            o_t = g_cmp * g.ocmp[:, hs] + g_sel * g.acc[0, :, hs] + g_win * g.acc[1, :, hs]
            g.o[:, hs] = o_t.T.astype(g.o.dtype)


def nsa_attention(pa, pvt, kcmp, vcmp_t, pg, tables, q_gain, k_gain, bsz, seq):
    nqb = seq // Q_BLOCK
    G, R = NSA_KV_HEADS, NSA_GROUP
    W = R * Q_BLOCK
    bdiag, boff, bfar, bcmp, overlap_t = tables
    n_cmp_rows = seq // CMP_STRIDE

    kv_w = G * HEAD_DIM

    def k_spec(seg):
        return pl.BlockSpec((seq, kv_w), lambda b, i, s=_A_BLK[seg] // G: (b, s))

    def vt_spec(seg):
        return pl.BlockSpec((kv_w, seq), lambda b, i, s=_T_BLK[seg] // G: (s, b))

    def whole(x):
        return pl.BlockSpec(x.shape, lambda b, i: (0,) * x.ndim)

    return pl.pallas_call(
        functools.partial(_nsa_kernel, seq=seq),
        grid=(bsz, nqb),
        in_specs=[pl.BlockSpec((NSA_Q, Q_BLOCK), lambda b, i, s=_T_BLK[_Q_N] * LANE // NSA_Q: (s, b * nqb + i)),
                  pl.BlockSpec((1, n_cmp_rows, kv_w), lambda b, i: (b, 0, 0)),
                  pl.BlockSpec((1, kv_w, n_cmp_rows), lambda b, i: (b, 0, 0)),
                  k_spec(_KS), vt_spec(_VS), k_spec(_KW), vt_spec(_VW),
                  pl.BlockSpec((Q_BLOCK, G * LANE), lambda b, i, s=B_COLS // (G * LANE): (b * nqb + i, s)),
                  whole(bdiag), whole(boff), whole(bfar), whole(bcmp),
                  pl.BlockSpec((HEAD_DIM, Q_BLOCK), lambda b, i: (0, 0)),
                  pl.BlockSpec((3, HEAD_DIM), lambda b, i: (0, 0)),
                  pl.BlockSpec((LANE, LANE), lambda b, i: (0, 0))],
        out_specs=pl.BlockSpec((Q_BLOCK, NSA_Q), lambda b, i: (b * nqb + i, 0)),
        out_shape=jax.ShapeDtypeStruct((bsz * seq, NSA_Q), _MXU),
        scratch_shapes=[pltpu.VMEM((seq, kv_w), _MXU),
                        pltpu.VMEM((seq, kv_w), _MXU),
                        pltpu.VMEM((G, 2 * R, 1, Q_BLOCK), _F32),
                        pltpu.VMEM((G, 2 * R, 1, Q_BLOCK), _F32),
                        pltpu.VMEM((G, 2, HEAD_DIM, W), _F32),
                        pltpu.VMEM((G, HEAD_DIM, W), _F32),
                        pltpu.VMEM((G, seq // SEL_BLOCK, Q_BLOCK), _F32)],
        compiler_params=_params("parallel", "arbitrary"),
        name="nsa_attention",
    )(pvt, kcmp, vcmp_t, pa, pvt, pa, pvt, pg, bdiag, boff, bfar, bcmp,
      jnp.broadcast_to(q_gain.reshape(HEAD_DIM, 1), (HEAD_DIM, Q_BLOCK)), k_gain, overlap_t)


def _t5_bucket(dist):
    n = jnp.maximum(dist, 0)
    exact = REL_BUCKETS // 2
    big = exact + (jnp.log(jnp.maximum(n, 1).astype(jnp.float32) / exact)
                   / math.log(REL_MAX_DIST / exact) * (REL_BUCKETS - exact)).astype(jnp.int32)
    return jnp.where(n < exact, n, jnp.minimum(big, REL_BUCKETS - 1))


def nsa_tables(rel_bias, seq):
    G, R = NSA_KV_HEADS, NSA_GROUP
    nqb = seq // Q_BLOCK
    tab_h = rel_bias.T.astype(_F32)
    s = np.arange(Q_BLOCK)[:, None]
    t = np.arange(Q_BLOCK)[None, :]

    def group_tiles(x, lead):
        nl = len(lead)
        rows = x.shape[-2]
        x = x.reshape((G, R) + lead + (rows, Q_BLOCK))
        perm = (0,) + tuple(range(2, 2 + nl)) + (2 + nl, 1, 3 + nl)
        return x.transpose(perm).reshape((G,) + lead + (rows, R * Q_BLOCK))

    bdiag = group_tiles(jnp.take(tab_h, _t5_bucket(jnp.asarray(t - s)), axis=1), ())
    boff = group_tiles(jnp.take(tab_h, _t5_bucket(jnp.asarray(Q_BLOCK + t - s)), axis=1), ())
    bfar = group_tiles(jnp.take(tab_h, _t5_bucket(jnp.full((1, Q_BLOCK), 2 * Q_BLOCK)), axis=1), ())
    lead = (nqb - 1) * Q_BLOCK // CMP_STRIDE
    rows_c = -(-(seq // CMP_STRIDE + lead) // 8) * 8
    m = np.arange(rows_c)[:, None]
    dist_c = t - (CMP_STRIDE * (m - lead) + CMP_BLOCK - 1)
    bcmp = group_tiles(jnp.take(tab_h, _t5_bucket(jnp.asarray(dist_c)), axis=1), ())
    n_cmp = (seq - CMP_BLOCK) // CMP_STRIDE + 1
    n_sel = seq // SEL_BLOCK
    c_start = np.arange(LANE) * CMP_STRIDE
    s_start = np.arange(LANE) * SEL_BLOCK
    overlap = ((c_start[:, None] < s_start[None, :] + SEL_BLOCK)
               & (c_start[:, None] + CMP_BLOCK > s_start[None, :])
               & (np.arange(LANE)[:, None] < n_cmp) & (np.arange(LANE)[None, :] < n_sel))
    return bdiag, boff, bfar, bcmp, jnp.asarray(overlap.T.astype(np.float32)).astype(_MXU)


def _sb_kernel(q_ref, k_ref, vt_ref, o_ref, acc_ref, rest_ref, *, heads):
    QB = Q_BLOCK
    i = pl.program_id(2)
    scale = HEAD_DIM ** -0.5
    row = lax.broadcasted_iota(jnp.int32, (QB, QB), 0)
    col = lax.broadcasted_iota(jnp.int32, (QB, QB), 1)
    strict = row < col
    wide = 4

    pad = 8
    r_l = lax.broadcasted_iota(jnp.int32, (QB + pad, 2 * QB), 0)
    c_l = lax.broadcasted_iota(jnp.int32, (QB + pad, 2 * QB), 1)
    later = jnp.where((jnp.where(c_l >= QB, c_l - QB, c_l) > r_l) | (r_l >= QB), 1.0, 0.0).astype(_MXU)

    def block(c, first, nblk=1):
        rows = nblk * QB
        off = pl.multiple_of(c * QB, QB)
        hss = [slice(h * HEAD_DIM, (h + 1) * HEAD_DIM) for h in range(heads)]
        zs = [_dot(k_ref[pl.ds(off, rows), hs], q_ref[hs, :]) * scale for hs in hss]
        log_bs, pieces = [], []
        for z in zs:
            log_b, _ = _log_sigmoid_pair(z)
            log_1mb = log_b - z
            if first:
                log_1mb = jnp.where(strict, log_1mb, 0.0)
            log_bs.append(log_b)
            pieces.append([jnp.concatenate(_split(log_1mb[j * QB:(j + 1) * QB, :], 2), axis=0) for j in range(nblk)])
        sums = [[_dot(later, p) for p in ps] for ps in pieces]
        betweens, totals = [], []
        for per_block in sums:
            after = None
            parts = []
            for s in reversed(per_block):
                parts.append(s[:QB, :] if after is None else s[:QB, :] + after)
                after = s[QB:QB + 1, :] if after is None else after + s[QB:QB + 1, :]
            betweens.append(parts[0] if nblk == 1 else jnp.concatenate(parts[::-1], axis=0))
            totals.append(after)
        probs = []
        for h in range(heads):
            if first:
                probs.append(jnp.where(strict, jnp.exp(log_bs[h] + betweens[h]), 0.0).astype(_MXU))
                rest_ref[h] = totals[h]
            else:
                rest = rest_ref[h]
                probs.append(jnp.exp(log_bs[h] + betweens[h] + rest).astype(_MXU))
                rest_ref[h] = rest + totals[h]
        upds = [_dot(vt_ref[hs, pl.ds(off, rows)], probs[h]) for h, hs in enumerate(hss)]
        for h in range(heads):
            acc_ref[h] = upds[h] if first else acc_ref[h] + upds[h]

    block(i, True)

    def body(n, carry):
        block(i - wide * (n + 1), False, wide)
        return carry
    lax.fori_loop(0, i // wide, body, 0)
    rem = i % wide

    @pl.when(rem >= 2)
    def _():
        block(rem - 2, False, 2)

    @pl.when(rem % 2 == 1)
    def _():
        block(0, False)
    for h in range(heads):
        o_ref[:, h * HEAD_DIM:(h + 1) * HEAD_DIM] = acc_ref[h].T.astype(o_ref.dtype)


def stick_breaking(pa, pvt, bsz, seq, heads=SB_HEADS):
    nqb = seq // Q_BLOCK
    w = heads * HEAD_DIM
    qb, kb, vb = _T_BLK[_Q_S] * LANE // w, _A_BLK[_K_S] * LANE // w, _T_BLK[_V_S] * LANE // w
    return pl.pallas_call(
        functools.partial(_sb_kernel, heads=heads),
        grid=(bsz, SB_HEADS // heads, nqb),
        in_specs=[pl.BlockSpec((w, Q_BLOCK), lambda b, h, i: (qb + h, b * nqb + i)),
                  pl.BlockSpec((seq, w), lambda b, h, i: (b, kb + h)),
                  pl.BlockSpec((w, seq), lambda b, h, i: (vb + h, b))],
        out_specs=pl.BlockSpec((Q_BLOCK, w), lambda b, h, i: (b * nqb + i, h)),
        out_shape=jax.ShapeDtypeStruct((bsz * seq, SB_W), _MXU),
        scratch_shapes=[pltpu.VMEM((heads, HEAD_DIM, Q_BLOCK), _F32),
                        pltpu.VMEM((heads, 1, Q_BLOCK), _F32)],
        compiler_params=_params("parallel", "parallel", "arbitrary"),
        name="stick_breaking",
    )(pvt, pa, pvt)


def _hgrn_kernel(q_ref, f_ref, i_ref, g_ref, llb_ref, l1lb_ref, omlb_ref, ng_ref, o_ref, state_ref, *, seq, heads):
    C, SUB = HG_CHUNK, HG_SUB
    nsub = C // SUB
    hss = [slice(h * HG_DK, (h + 1) * HG_DK) for h in range(heads)]
    row = lax.broadcasted_iota(jnp.int32, (C, C), 0)
    col = lax.broadcasted_iota(jnp.int32, (C, C), 1)
    upto = jnp.where(col <= row, 1.0, 0.0).astype(_MXU)
    sub_t = lax.broadcasted_iota(jnp.int32, (SUB, 1), 0)
    sub_c = lax.broadcasted_iota(jnp.int32, (SUB, C), 1)
    state_ref[...] = jnp.zeros(state_ref.shape, _F32)

    def chunk(n, carry):
        off = pl.multiple_of(n * C, C)
        fps = [f_ref[pl.ds(off, C), hs] for hs in hss]
        pieces = []
        for fp, hs in zip(fps, hss):
            log_sig, _ = _log_sigmoid_pair(fp)
            y = l1lb_ref[:, hs] + log_sig
            a = llb_ref[:, hs]
            log_f = jnp.maximum(a, y) + jnp.log1p(jnp.exp(-jnp.abs(a - y)))
            pieces.append(_split(log_f, 3))
        bs = [_dot(upto, p[0]) + _dot(upto, p[1]) + _dot(upto, p[2]) for p in pieces]
        qs = [q_ref[pl.ds(off, C), hs].astype(_F32) for hs in hss]
        ks = [omlb_ref[:, hs] * jax.nn.sigmoid(-fp) for fp, hs in zip(fps, hss)]
        vs = [i_ref[pl.ds(off, C), hs].astype(_F32) for hs in hss]

        outs = [_dot_nt((q * jnp.exp(b)).astype(_MXU), state_ref[h].astype(_MXU))
                for h, (q, b) in enumerate(zip(qs, bs))]

        lhs, rhs = [], []
        for q, k, b in zip(qs, ks, bs):
            for s_i in range(1, nsub):
                lo = s_i * SUB
                b_ref = b[lo - 1:lo, :]
                lhs.append((q[lo:lo + SUB, :] * jnp.exp(b[lo:lo + SUB, :] - b_ref)).astype(_MXU))
                rhs.append((k * jnp.exp(jnp.minimum(b_ref - b, 0.0))).astype(_MXU))
        a_blocks = [_dot_nt(x, y) for x, y in zip(lhs, rhs)]
        intra = []
        for h in range(heads):
            rows = [jnp.zeros((SUB, C), _F32)]
            for s_i in range(1, nsub):
                rows.append(jnp.where(sub_c < s_i * SUB, a_blocks[h * (nsub - 1) + s_i - 1], 0.0))
            intra.append(jnp.concatenate(rows, axis=0).astype(_MXU))
        outs = [o + _dot(a, v.astype(_MXU)) for o, a, v in zip(outs, intra, vs)]

        for h, (q, k, v, b) in enumerate(zip(qs, ks, vs, bs)):
            diag = []
            for s_i in range(nsub):
                lo = s_i * SUB
                b_i = b[lo:lo + SUB, :]
                q_i = q[lo:lo + SUB, :]
                o_d = jnp.zeros((SUB, HG_DV), _F32)
                for s in range(SUB):
                    w = jnp.exp(b_i - b[lo + s:lo + s + 1, :])
                    a_col = jnp.sum(q_i * k[lo + s:lo + s + 1, :] * w, axis=-1, keepdims=True)
                    a_col = jnp.where(sub_t >= s, a_col, 0.0)
                    o_d = o_d + a_col * v[lo + s:lo + s + 1, :]
                diag.append(o_d)
            outs[h] = outs[h] + jnp.concatenate(diag, axis=0)

        k_decs = [(k * jnp.exp(b[C - 1:C, :] - b)).astype(_MXU) for k, b in zip(ks, bs)]
        grown = [_dot(v.T.astype(_MXU), kd) for v, kd in zip(vs, k_decs)]
        for h, (b, hs) in enumerate(zip(bs, hss)):
            state_ref[h] = state_ref[h] * jnp.exp(b[C - 1:C, :]) + grown[h]
            gate = jax.nn.sigmoid(g_ref[pl.ds(off, C), hs])
            o_ref[pl.ds(off, C), hs] = (_rms(outs[h], ng_ref[...]) * gate).astype(o_ref.dtype)
        return carry

    lax.fori_loop(0, seq // C, chunk, 0)


def hgrn2(pa, pb, lb, norm_gain, bsz, seq, heads=4):
    lb = lb.reshape(1, HG_KW).astype(_F32)
    log_lb = jnp.log(lb)
    log_1mlb = jnp.log1p(-lb)
    one_mlb = 1.0 - lb
    w = heads * HG_DK

    def seq_spec(blk):
        return pl.BlockSpec((seq, w), lambda b, h, s=blk * LANE // w: (b, s + h))

    head_vec = pl.BlockSpec((1, w), lambda b, h: (0, h))
    return pl.pallas_call(
        functools.partial(_hgrn_kernel, seq=seq, heads=heads),
        grid=(bsz, HG_HEADS // heads),
        in_specs=[seq_spec(_A_BLK[_Q_H]), seq_spec(_B_BLK[_F_H]), seq_spec(_A_BLK[_I_H]), seq_spec(_B_BLK[_G_H]),
                  head_vec, head_vec, head_vec,
                  pl.BlockSpec((1, HG_DV), lambda b, h: (0, 0))],
        out_specs=pl.BlockSpec((seq, w), lambda b, h: (b, h)),
        out_shape=jax.ShapeDtypeStruct((bsz * seq, HG_VW), _MXU),
        scratch_shapes=[pltpu.VMEM((heads, HG_DV, HG_DK), _F32)],
        compiler_params=_params("parallel", "parallel"),
        name="hgrn2",
    )(pa, pb, pa, pb, log_lb, log_1mlb, one_mlb, norm_gain.reshape(1, HG_DV))


def _merge_kernel(o0_ref, o1_ref, o2_ref, w0_ref, w1_ref, w2_ref, g0_ref, g1_ref, g2_ref, out_ref):
    ys = [_dot(o_ref[...], w_ref[0]) for o_ref, w_ref in ((o0_ref, w0_ref), (o1_ref, w1_ref), (o2_ref, w2_ref))]
    acc = jax.nn.sigmoid(g0_ref[...].astype(_F32)) * ys[0]
    acc = acc + jax.nn.sigmoid(g1_ref[...].astype(_F32)) * ys[1]
    acc = acc + jax.nn.sigmoid(g2_ref[...].astype(_F32)) * ys[2]
    out_ref[...] = acc.astype(out_ref.dtype)


def merge_branches(o_nsa, o_sb, o_hg, w_branch, gates, tm=1024, tn=512):
    m = o_nsa.shape[0]
    nj = D_MODEL // tn
    gblk = 0

    def o_spec():
        return pl.BlockSpec((tm, BRANCH_W), lambda i, j: (i, 0))

    def w_spec(n):
        return pl.BlockSpec((1, BRANCH_W, tn), lambda i, j, n=n: (n, 0, j))

    def g_spec(n):
        return pl.BlockSpec((tm, tn), lambda i, j, n=n: (i, gblk + n * nj + j))

    return pl.pallas_call(
        _merge_kernel,
        grid=(m // tm, nj),
        in_specs=[o_spec(), o_spec(), o_spec(), w_spec(0), w_spec(1), w_spec(2), g_spec(0), g_spec(1), g_spec(2)],
        out_specs=pl.BlockSpec((tm, tn), lambda i, j: (i, j)),
        out_shape=jax.ShapeDtypeStruct((m, D_MODEL), _MXU),
        compiler_params=_params("parallel", "arbitrary"),
        name="merge_branches",
    )(o_nsa, o_sb, o_hg, w_branch, w_branch, w_branch, gates, gates, gates)


def _ffn_up_kernel(x_ref, halo_ref, gain_ref, wg_ref, wv_ref, cwg_ref, cwv_ref, cbg_ref, cbv_ref, o_ref, h_ref,
                   *, tiles_per_seq, sub):
    i = pl.program_id(0)

    @pl.when(pl.program_id(1) == 0)
    def _():
        h_ref[...] = _rms(x_ref[...], gain_ref[...]).astype(_MXU)

    first = (i % tiles_per_seq) == 0
    h = h_ref[...]
    halo = _rms(halo_ref[...], gain_ref[...]).astype(_MXU)
    tm = h.shape[0]
    hrows = halo.shape[0]
    row = lax.broadcasted_iota(jnp.int32, (tm, 1), 0)
    cols = [slice(c * sub, (c + 1) * sub) for c in range(wg_ref.shape[1] // sub)]

    prods = [[(_dot(h, w_ref[:, cs]), jnp.where(first, 0.0, _dot(halo, w_ref[:, cs]))) for w_ref in (wg_ref, wv_ref)]
             for cs in cols]

    def conv(u, uh, cw, cb):
        u1 = jnp.where(row == 0, uh[hrows - 1:hrows, :], pltpu.roll(u, 1, axis=0))
        u2 = jnp.where(row == 0, uh[hrows - 2:hrows - 1, :],
                       jnp.where(row == 1, uh[hrows - 1:hrows, :], pltpu.roll(u, 2, axis=0)))
        return cb + u2 * cw[0:1, :] + u1 * cw[1:2, :] + u * cw[2:3, :]

    for cs, ((ug, uhg), (uv, uhv)) in zip(cols, prods):
        gate = conv(ug, uhg, cwg_ref[:, cs], cbg_ref[:, cs])
        val = conv(uv, uhv, cwv_ref[:, cs], cbv_ref[:, cs])
        o_ref[:, cs] = (jax.nn.silu(gate) * val).astype(o_ref.dtype)


def ffn_up(x, gain, w_up, conv_w, conv_b, seq, tm=1024, tn=512, sub=256):
    m, k = x.shape
    nj = D_FF // tn
    hrows = 8
    conv_b = conv_b.reshape(1, 2 * D_FF)
    return pl.pallas_call(
        functools.partial(_ffn_up_kernel, tiles_per_seq=seq // tm, sub=sub),
        grid=(m // tm, nj),
        in_specs=[pl.BlockSpec((tm, k), lambda i, j: (i, 0)),
                  pl.BlockSpec((hrows, k), lambda i, j: (jnp.maximum(i * (tm // hrows) - 1, 0), 0)),
                  pl.BlockSpec((1, k), lambda i, j: (0, 0)),
                  pl.BlockSpec((k, tn), lambda i, j: (0, j)),
                  pl.BlockSpec((k, tn), lambda i, j: (0, nj + j)),
                  pl.BlockSpec((CONV_W, tn), lambda i, j: (0, j)),
                  pl.BlockSpec((CONV_W, tn), lambda i, j: (0, nj + j)),
                  pl.BlockSpec((1, tn), lambda i, j: (0, j)),
                  pl.BlockSpec((1, tn), lambda i, j: (0, nj + j))],
        out_specs=pl.BlockSpec((tm, tn), lambda i, j: (i, j)),
        out_shape=jax.ShapeDtypeStruct((m, D_FF), _MXU),
        scratch_shapes=[pltpu.VMEM((tm, k), _MXU)],
        compiler_params=_params("parallel", "arbitrary"),
        name="ffn_up_conv",
    )(x, x, gain.reshape(1, k), w_up, w_up, conv_w, conv_w, conv_b, conv_b)


def _gather_cols(w, segs):
    return jnp.concatenate([w[:, _SPLIT_OFF[s]:_SPLIT_OFF[s + 1]] for s in segs], axis=1)


def _nsa_gate_cols(w):
    g = w[:, _SPLIT_OFF[_G_N]:_SPLIT_OFF[_G_N + 1]].reshape(-1, NSA_KV_HEADS, NSA_GROUP * 3)
    g = jnp.pad(g, ((0, 0), (0, 0), (0, LANE - NSA_GROUP * 3)))
    return g.reshape(-1, NSA_KV_HEADS * LANE)


def kernel(x, norm_attn, w_in, nsa_q_gain, nsa_k_gain, cmp_pos, cmp_w1, cmp_w2, rel_bias, hg_lower_bound,
           hg_norm_gain, w_branch, w_out, norm_ffn, w_up, conv_w, conv_b, w_down):
    bsz, seq, d = x.shape
    m = bsz * seq
    lb_all = jnp.cumsum(jax.nn.softmax(hg_lower_bound.astype(_F32), axis=0), axis=0)
    lb_all = jnp.maximum(lb_all - lb_all[0:1], 0.0)
    tables = nsa_tables(rel_bias, seq)
    xf = x.reshape(m, d)
    for l in range(w_in.shape[0]):
        wl = w_in[l]
        h = rmsnorm_cast(xf, norm_attn[l])
        pa = matmul(h, _gather_cols(wl, _A_ORDER).astype(_MXU), _MXU, tm=1024, tn=A_COLS // 2, name="proj_a")
        w_b = jnp.concatenate([_gather_cols(wl, _B_ORDER), _nsa_gate_cols(wl)], axis=1).astype(_MXU)
        pb = matmul(h, w_b, _F32, tn=w_b.shape[1] // 3, name="proj_b")
        gates = matmul(h, _gather_cols(wl, (_GATES,)).astype(_MXU), _MXU, tn=1024, name="proj_gates")
        pkc = matmul(h, _gather_cols(wl, (_KC,)).astype(_MXU), _MXU, name="proj_kc")
        pvc = matmul(h, _gather_cols(wl, (_VC,)).astype(_MXU), _MXU, name="proj_vc")
        blk_rows = seq // CMP_STRIDE
        kcmp = compress(pkc.reshape(bsz, blk_rows, CMP_STRIDE * NSA_KV), cmp_w1[l, 0], cmp_w2[l, 0], cmp_pos[l, 0],
                        nsa_k_gain[l, 0], True)
        vcmp_t = compress(pvc.reshape(bsz, blk_rows, CMP_STRIDE * NSA_KV), cmp_w1[l, 1], cmp_w2[l, 1], cmp_pos[l, 1],
                          nsa_k_gain[l, 0], False)
        pvt = matmul_nt(_gather_cols(wl, _T_ORDER).T.astype(_MXU), h, _MXU, name="proj_vt")
        o_nsa = nsa_attention(pa, pvt, kcmp, vcmp_t, pb, tables, nsa_q_gain[l], nsa_k_gain[l], bsz, seq)
        o_sb = stick_breaking(pa, pvt, bsz, seq)
        o_hg = hgrn2(pa, pb, lb_all[l], hg_norm_gain[l], bsz, seq)
        merged = merge_branches(o_nsa, o_sb, o_hg, w_branch[l].astype(_MXU), gates, tm=256, tn=D_MODEL)
        xf = matmul(merged, w_out[l].astype(_MXU), _F32, residual=xf, tm=512, tn=D_MODEL, name="out_proj")
        act = ffn_up(xf, norm_ffn[l], w_up[l].astype(_MXU), conv_w[l], conv_b[l], seq)
        xf = matmul(act, w_down[l].astype(_MXU), _F32, residual=xf, tm=1024, tn=512, name="ffn_down")
    return xf.reshape(bsz, seq, d)
```
